```python
import jax, jax.numpy as jnp
from jax import lax
import numpy as np

D_MODEL = 1024
BATCH = 2
SEQ = 16384
DEPTH = 1
DEC_BATCH = 32
DEC_SEQ = 16
PAST_LEN = 2048

CHUNK = 64
LEFT_CHUNKS = 8
BAND_PAST = LEFT_CHUNKS * CHUNK
BAND = (LEFT_CHUNKS + 1) * CHUNK
KV_WINDOW = BAND
N_HEADS = 8
HEAD_DIM = 64
ATT_WIDTH = N_HEADS * HEAD_DIM
MAX_REL = 256
ATT_SCALE = HEAD_DIM ** -0.5
NEG_INF = -1e30
RNN_WIDTH = D_MODEL // 2
RNN_BLOCKS = 8
RNN_BLOCK_W = RNN_WIDTH // RNN_BLOCKS
CONV_W = 4
RG_C = 8.0
PROJ_WIDTH = 3 * ATT_WIDTH + 2 * RNN_WIDTH + 2 * D_MODEL
SPLITS = [ATT_WIDTH, 2 * ATT_WIDTH, 3 * ATT_WIDTH, 3 * ATT_WIDTH + RNN_WIDTH,
          3 * ATT_WIDTH + 2 * RNN_WIDTH, 3 * ATT_WIDTH + 2 * RNN_WIDTH + D_MODEL]
N_GROUPS = 4
EXPERTS_PER_GROUP = 8
N_EXPERTS = N_GROUPS * EXPERTS_PER_GROUP
TOP_K = 2
D_EXPERT = 512
MOE_BLOCK = 128
EPS = 1e-6

kernel_name = "hybrid_chunkband_rglru_hmoe_step"


def rmsnorm(x, g):
    xf = x.astype(jnp.float32)
    y = xf * lax.rsqrt(jnp.mean(xf * xf, axis=-1, keepdims=True) + EPS)
    return (y * g.astype(jnp.float32)).astype(x.dtype)


def rel_bias_lookup(rel_bias, dist):
    idx = jnp.clip(dist, -MAX_REL, MAX_REL) + MAX_REL
    return rel_bias[:, idx].astype(jnp.float32)


def band_attention_prompt(q, k, v, rel_bias):
    b, seq = q.shape[:2]
    nc = seq // CHUNK
    qc = q.reshape(b, nc, CHUNK, N_HEADS, HEAD_DIM)
    pad = ((0, 0), (LEFT_CHUNKS, 0), (0, 0), (0, 0), (0, 0))
    kp = jnp.pad(k.reshape(b, nc, CHUNK, N_HEADS, HEAD_DIM), pad)
    vp = jnp.pad(v.reshape(b, nc, CHUNK, N_HEADS, HEAD_DIM), pad)
    kb = jnp.concatenate([kp[:, j:j + nc] for j in range(LEFT_CHUNKS + 1)], axis=2)
    vb = jnp.concatenate([vp[:, j:j + nc] for j in range(LEFT_CHUNKS + 1)], axis=2)
    qi = jnp.arange(CHUNK)[:, None]
    kj = jnp.arange(BAND)[None, :]
    bias = rel_bias_lookup(rel_bias, qi + BAND_PAST - kj)
    valid = kj[None] >= ((LEFT_CHUNKS - jnp.arange(nc)) * CHUNK)[:, None, None]
    s = jnp.einsum('bnqhd,bnkhd->bnhqk', qc, kb, preferred_element_type=jnp.float32) * ATT_SCALE + bias
    s = jnp.where(valid[None, :, None], s, NEG_INF)
    p = jax.nn.softmax(s, axis=-1).astype(v.dtype)
    o = jnp.einsum('bnhqk,bnkhd->bnqhd', p, vb)
    return o.reshape(b, seq, ATT_WIDTH)


def band_attention_sample(q, k_new, v_new, k_cache, v_cache, rel_bias):
    bd, t = q.shape[:2]
    L = k_cache.shape[1]
    k_all = jnp.concatenate([k_cache, k_new], axis=1)
    v_all = jnp.concatenate([v_cache, v_new], axis=1)
    qi = jnp.arange(t)[:, None]
    kj = jnp.arange(L + t)[None, :]
    bias = rel_bias_lookup(rel_bias, qi + L - kj)
    valid = kj >= L - BAND_PAST
    s = jnp.einsum('bqhd,bkhd->bhqk', q, k_all, preferred_element_type=jnp.float32) * ATT_SCALE + bias
    s = jnp.where(valid, s, NEG_INF)
    p = jax.nn.softmax(s, axis=-1).astype(v_all.dtype)
    o = jnp.einsum('bhqk,bkhd->bqhd', p, v_all)
    return o.reshape(bd, t, ATT_WIDTH), k_all[:, t:], v_all[:, t:]


def causal_conv(x, x_prev, w, b):
    t = x.shape[1]
    xp = jnp.concatenate([x_prev, x], axis=1)
    y = b + xp[:, 0:t] * w[0]
    for i in range(1, CONV_W):
        y = y + xp[:, i:i + t] * w[i]
    return y, xp[:, t:]


def linear_scan(a, b, h0):
    b = b.at[:, 0].add(a[:, 0] * h0)
    def combine(l, r):
        return (l[0] * r[0], r[0] * l[1] + r[1])
    _, h = lax.associative_scan(combine, (a, b), axis=1)
    return h


def rg_lru(xc, h0, w_a, b_a, w_x, b_x, lam):
    bsz, t, _ = xc.shape
    xf = xc.astype(jnp.float32)
    xb = xf.reshape(bsz, t, RNN_BLOCKS, RNN_BLOCK_W)
    r = jax.nn.sigmoid(jnp.einsum('btnc,ncd->btnd', xb, w_a.astype(jnp.float32)).reshape(bsz, t, RNN_WIDTH) + b_a.astype(jnp.float32))
    i = jax.nn.sigmoid(jnp.einsum('btnc,ncd->btnd', xb, w_x.astype(jnp.float32)).reshape(bsz, t, RNN_WIDTH) + b_x.astype(jnp.float32))
    log_a = -RG_C * r * jax.nn.softplus(-lam.astype(jnp.float32))
    a = jnp.exp(log_a)
    inp = jnp.sqrt(-jnp.expm1(2.0 * log_a)) * (i * xf)
    h = linear_scan(a, inp, h0.astype(jnp.float32))
    return h.astype(xc.dtype), h[:, -1].astype(xc.dtype)


def hier_route(xf, w_rg, b_rg, w_re, b_re):
    n = xf.shape[0]
    pg = jax.nn.softmax((xf @ w_rg).astype(jnp.float32) + b_rg.astype(jnp.float32), axis=-1)
    p_grp, grp = lax.top_k(pg, 1)
    p_grp, grp = p_grp[:, 0], grp[:, 0]
    le = ((xf @ w_re).astype(jnp.float32) + b_re.astype(jnp.float32)).reshape(n, N_GROUPS, EXPERTS_PER_GROUP)
    le_g = jnp.einsum('ng,nge->ne', jax.nn.one_hot(grp, N_GROUPS, dtype=jnp.float32), le)
    pe = jax.nn.softmax(le_g, axis=-1)
    top_p, top_i = lax.top_k(pe, TOP_K)
    top_p = top_p / jnp.sum(top_p, axis=-1, keepdims=True)
    experts = grp[:, None] * EXPERTS_PER_GROUP + top_i
    weights = p_grp[:, None] * top_p
    return experts.astype(jnp.int32), weights


def moe_apply(xf, experts, weights, w_gate, w_up, w_down):
    n, d = xf.shape
    a = n * TOP_K
    flat_e = experts.reshape(a)
    flat_tok = jnp.arange(a, dtype=jnp.int32) // TOP_K
    flat_w = weights.reshape(a)
    order = jnp.argsort(flat_e)
    e_sorted = flat_e[order]
    counts = jnp.bincount(flat_e, length=N_EXPERTS).astype(jnp.int32)
    padded = (counts + MOE_BLOCK - 1) // MOE_BLOCK * MOE_BLOCK
    pad_end = jnp.cumsum(padded)
    pad_start = pad_end - padded
    start = jnp.cumsum(counts) - counts
    dest = pad_start[e_sorted] + jnp.arange(a, dtype=jnp.int32) - start[e_sorted]
    n_blocks = -(-a // MOE_BLOCK) + N_EXPERTS
    rows = n_blocks * MOE_BLOCK
    tok_pad = jnp.full((rows,), n, jnp.int32).at[dest].set(flat_tok[order])
    w_pad = jnp.zeros((rows,), jnp.float32).at[dest].set(flat_w[order])
    block_start = jnp.arange(n_blocks, dtype=jnp.int32) * MOE_BLOCK
    block_expert = jnp.minimum(jnp.searchsorted(pad_end, block_start, side='right'), N_EXPERTS - 1)
    x_ext = jnp.concatenate([xf, jnp.zeros((1, d), xf.dtype)], axis=0)

    def expert_block(args):
        tok, e = args
        xb = x_ext[tok]
        hid = jax.nn.silu(xb @ w_gate[e]) * (xb @ w_up[e])
        return hid @ w_down[e]

    out = lax.map(expert_block, (tok_pad.reshape(n_blocks, MOE_BLOCK), block_expert))
    out = out.reshape(rows, d) * w_pad[:, None].astype(out.dtype)
    return jnp.zeros_like(x_ext).at[tok_pad].add(out)[:n]


def trunk_layer(x, k_cache, v_cache, conv_prev, h0, norm1_g, w_in, b_in, rel_bias, w_att_branch,
                conv_w, conv_b, w_rec_gate, b_rec_gate, w_in_gate, b_in_gate, lru_lambda, w_rnn_branch,
                w_out, norm2_g, w_rg, b_rg, w_re, b_re, w_e_gate, w_e_up, w_e_down):
    bsz, t, _ = x.shape
    xn = rmsnorm(x, norm1_g)
    proj = xn @ w_in + b_in
    q, k, v, rx, rgate, ga, gr = jnp.split(proj, SPLITS, axis=-1)
    q = q.reshape(bsz, t, N_HEADS, HEAD_DIM)
    k = k.reshape(bsz, t, N_HEADS, HEAD_DIM)
    v = v.reshape(bsz, t, N_HEADS, HEAD_DIM)
    if k_cache is None:
        att = band_attention_prompt(q, k, v, rel_bias)
        keep = min(KV_WINDOW, t)
        k_state, v_state = k[:, t - keep:], v[:, t - keep:]
    else:
        att, k_state, v_state = band_attention_sample(q, k, v, k_cache, v_cache, rel_bias)
    xc, conv_state = causal_conv(rx, conv_prev, conv_w, conv_b)
    hseq, h_last = rg_lru(xc, h0, w_rec_gate, b_rec_gate, w_in_gate, b_in_gate, lru_lambda)
    rnn = hseq * jax.nn.gelu(rgate)
    ya = att @ w_att_branch
    yr = rnn @ w_rnn_branch
    mixed = jax.nn.sigmoid(ga) * ya + jax.nn.sigmoid(gr) * yr
    h = x + mixed @ w_out
    hn = rmsnorm(h, norm2_g).reshape(bsz * t, D_MODEL)
    experts, gates = hier_route(hn, w_rg, b_rg, w_re, b_re)
    moe = moe_apply(hn, experts, gates, w_e_gate, w_e_up, w_e_down)
    h = h + moe.reshape(bsz, t, D_MODEL)
    return h, k_state, v_state, conv_state, h_last


def setup_inputs(seed: int = 0) -> dict:
    key = jax.random.key(seed)
    ks = jax.random.split(key, 32)
    f32 = jnp.float32

    def nrm(k, shape, scale):
        return jax.random.normal(k, shape, f32) * scale

    kv_len = min(KV_WINDOW, PAST_LEN)
    u = jax.random.uniform(ks[14], (DEPTH, RNN_WIDTH), f32, minval=0.9, maxval=0.999)
    a0 = u ** (1.0 / RG_C)
    lru_lambda = jnp.log(a0) - jnp.log1p(-a0)
    return {
        "x_prompt": nrm(ks[0], (BATCH, SEQ, D_MODEL), 1.0),
        "x_sample": nrm(ks[1], (DEC_BATCH, DEC_SEQ, D_MODEL), 1.0),
        "cache_k": nrm(ks[2], (DEPTH, DEC_BATCH, kv_len, N_HEADS, HEAD_DIM), 1.0),
        "cache_v": nrm(ks[3], (DEPTH, DEC_BATCH, kv_len, N_HEADS, HEAD_DIM), 1.0),
        "state_conv": nrm(ks[4], (DEPTH, DEC_BATCH, CONV_W - 1, RNN_WIDTH), 1.0),
        "state_h": nrm(ks[5], (DEPTH, DEC_BATCH, RNN_WIDTH), 0.5),
        "norm1_g": 1.0 + nrm(ks[6], (DEPTH, D_MODEL), 0.05),
        "w_in": nrm(ks[7], (DEPTH, D_MODEL, PROJ_WIDTH), D_MODEL ** -0.5),
        "b_in": nrm(ks[8], (DEPTH, PROJ_WIDTH), 0.02),
        "rel_bias": nrm(ks[9], (DEPTH, N_HEADS, 2 * MAX_REL + 1), 0.1),
        "w_att_branch": nrm(ks[10], (DEPTH, ATT_WIDTH, D_MODEL), ATT_WIDTH ** -0.5),
        "conv_w": nrm(ks[11], (DEPTH, CONV_W, RNN_WIDTH), CONV_W ** -0.5),
        "conv_b": nrm(ks[12], (DEPTH, RNN_WIDTH), 0.02),
        "w_rec_gate": nrm(ks[13], (DEPTH, RNN_BLOCKS, RNN_BLOCK_W, RNN_BLOCK_W), RNN_BLOCK_W ** -0.5),
        "b_rec_gate": nrm(ks[15], (DEPTH, RNN_WIDTH), 0.02),
        "w_in_gate": nrm(ks[16], (DEPTH, RNN_BLOCKS, RNN_BLOCK_W, RNN_BLOCK_W), RNN_BLOCK_W ** -0.5),
        "b_in_gate": nrm(ks[17], (DEPTH, RNN_WIDTH), 0.02),
        "lru_lambda": lru_lambda,
        "w_rnn_branch": nrm(ks[18], (DEPTH, RNN_WIDTH, D_MODEL), RNN_WIDTH ** -0.5),
        "w_out": nrm(ks[19], (DEPTH, D_MODEL, D_MODEL), D_MODEL ** -0.5),
        "norm2_g": 1.0 + nrm(ks[20], (DEPTH, D_MODEL), 0.05),
        "w_router_group": nrm(ks[21], (DEPTH, D_MODEL, N_GROUPS), D_MODEL ** -0.5),
        "b_router_group": nrm(ks[22], (DEPTH, N_GROUPS), 0.01),
        "w_router_expert": nrm(ks[23], (DEPTH, D_MODEL, N_EXPERTS), D_MODEL ** -0.5),
        "b_router_expert": nrm(ks[24], (DEPTH, N_EXPERTS), 0.01),
        "w_e_gate": nrm(ks[25], (DEPTH, N_EXPERTS, D_MODEL, D_EXPERT), D_MODEL ** -0.5),
        "w_e_up": nrm(ks[26], (DEPTH, N_EXPERTS, D_MODEL, D_EXPERT), D_MODEL ** -0.5),
        "w_e_down": nrm(ks[27], (DEPTH, N_EXPERTS, D_EXPERT, D_MODEL), D_EXPERT ** -0.5),
        "final_norm_g": 1.0 + nrm(ks[28], (D_MODEL,), 0.05),
    }


def reference(x_prompt, x_sample, cache_k, cache_v, state_conv, state_h, norm1_g, w_in, b_in, rel_bias,
              w_att_branch, conv_w, conv_b, w_rec_gate, b_rec_gate, w_in_gate, b_in_gate, lru_lambda,
              w_rnn_branch, w_out, norm2_g, w_router_group, b_router_group, w_router_expert,
              b_router_expert, w_e_gate, w_e_up, w_e_down, final_norm_g):
    hp, hs = x_prompt, x_sample
    bsz = x_prompt.shape[0]
    kp_l, vp_l, cp_l, rp_l = [], [], [], []
    ks_l, vs_l, cs_l, rs_l = [], [], [], []
    for l in range(DEPTH):
        lw = (norm1_g[l], w_in[l], b_in[l], rel_bias[l], w_att_branch[l], conv_w[l], conv_b[l],
              w_rec_gate[l], b_rec_gate[l], w_in_gate[l], b_in_gate[l], lru_lambda[l], w_rnn_branch[l],
              w_out[l], norm2_g[l], w_router_group[l], b_router_group[l], w_router_expert[l],
              b_router_expert[l], w_e_gate[l], w_e_up[l], w_e_down[l])
        conv0 = jnp.zeros((bsz, CONV_W - 1, RNN_WIDTH), x_prompt.dtype)
        h0 = jnp.zeros((bsz, RNN_WIDTH), x_prompt.dtype)
        hp, kp, vp, cp, rp = trunk_layer(hp, None, None, conv0, h0, *lw)
        hs, ks_, vs_, cs, rs = trunk_layer(hs, cache_k[l], cache_v[l], state_conv[l], state_h[l], *lw)
        kp_l.append(kp); vp_l.append(vp); cp_l.append(cp); rp_l.append(rp)
        ks_l.append(ks_); vs_l.append(vs_); cs_l.append(cs); rs_l.append(rs)
    y_prompt = rmsnorm(hp, final_norm_g)
    y_sample = rmsnorm(hs, final_norm_g)
    new_k_prompt = jnp.stack(kp_l)
    new_v_prompt = jnp.stack(vp_l)
    new_conv_prompt = jnp.stack(cp_l)
    new_h_prompt = jnp.stack(rp_l)
    new_k_sample = jnp.stack(ks_l)
    new_v_sample = jnp.stack(vs_l)
    new_conv_sample = jnp.stack(cs_l)
    new_h_sample = jnp.stack(rs_l)
    return (y_prompt, y_sample, new_k_prompt, new_v_prompt, new_conv_prompt, new_h_prompt,
            new_k_sample, new_v_sample, new_conv_sample, new_h_sample)
```

```python
import functools

import jax
import jax.numpy as jnp
from jax import lax
from jax.experimental import pallas as pl
from jax.experimental.pallas import tpu as pltpu

CHUNK = 64
LEFT_CHUNKS = 8
BAND_PAST = LEFT_CHUNKS * CHUNK
N_HEADS = 8
HEAD_DIM = 64
MAX_REL = 256
NEG_INF = -1e30
RNN_BLOCKS = 8
CONV_W = 4
RG_C = 8.0
N_GROUPS = 4
EXPERTS_PER_GROUP = 8
N_EXPERTS = N_GROUPS * EXPERTS_PER_GROUP
TOP_K = 2
EPS = 1e-6

LANES = 128
SUBLANES = 8
ROW_TILE = 512
Q_BLOCK = 256
KEY_BLOCKS = 3
EXPERT_BLOCK = 256
VMEM_LIMIT = 56 * 1024 * 1024

BF16 = jnp.bfloat16
F32 = jnp.float32


def _params(n_axes, vmem=VMEM_LIMIT):
    return pltpu.CompilerParams(dimension_semantics=("arbitrary",) * n_axes, vmem_limit_bytes=vmem)


def _const_spec(shape):
    zeros = (0,) * len(shape)
    return pl.BlockSpec(shape, lambda *_: zeros, pipeline_mode=pl.Buffered(1))


def _rmsnorm(x, g):
    return x * lax.rsqrt(jnp.mean(x * x, axis=-1, keepdims=True) + EPS) * g


def _proj_kernel(n_prompt_tiles, att_w, rnn_w, d_model,
                 xp_ref, xs_ref, g_ref, w_ref, b_ref,
                 q_ref, k_ref, v_ref, kv32_ref, rx_ref, rg_ref, ga_ref, gr_ref):
    i = pl.program_id(0)
    x = jnp.where(i < n_prompt_tiles, xp_ref[...], xs_ref[...])
    xn = _rmsnorm(x, g_ref[...]).astype(BF16)

    def seg(lo, width):
        return jnp.dot(xn, w_ref[:, lo:lo + width], preferred_element_type=F32) + b_ref[:, lo:lo + width]

    q = seg(0, att_w)
    k = seg(att_w, att_w)
    v = seg(2 * att_w, att_w)
    q_ref[...] = (q * (HEAD_DIM ** -0.5)).astype(BF16)
    k_ref[...] = k.astype(BF16)
    v_ref[...] = v.astype(BF16)
    kv32_ref[:, :att_w] = k
    kv32_ref[:, att_w:] = v
    o = 3 * att_w
    rx_ref[...] = seg(o, rnn_w)
    rg_ref[...] = seg(o + rnn_w, rnn_w).astype(BF16)
    ga_ref[...] = seg(o + 2 * rnn_w, d_model).astype(BF16)
    gr_ref[...] = seg(o + 2 * rnn_w + d_model, d_model).astype(BF16)


def _proj(xp, xs, g1, w_in, b_in, att_w, rnn_w):
    n_p, d = xp.shape
    n_s = xs.shape[0]
    assert n_p % ROW_TILE == 0 and n_s == ROW_TILE
    n = n_p + n_s
    npt = n_p // ROW_TILE
    pw = w_in.shape[1]
    row = lambda w: pl.BlockSpec((ROW_TILE, w), lambda i: (i, 0))
    outs = [(att_w, BF16), (att_w, BF16), (att_w, BF16), (2 * att_w, F32), (rnn_w, F32), (rnn_w, BF16),
            (d, BF16), (d, BF16)]
    return pl.pallas_call(
        functools.partial(_proj_kernel, npt, att_w, rnn_w, d),
        grid=(n // ROW_TILE,),
        in_specs=[pl.BlockSpec((ROW_TILE, d), lambda i: (jnp.minimum(i, npt - 1), 0)),
                  pl.BlockSpec((ROW_TILE, d), lambda i: (0, 0)),
                  _const_spec((1, d)), _const_spec((d, pw)), _const_spec((1, pw))],
        out_specs=[row(w) for w, _ in outs],
        out_shape=[jax.ShapeDtypeStruct((n, w), dt) for w, dt in outs],
        compiler_params=_params(1),
        name="proj",
    )(xp, xs, g1, w_in, b_in)


def _attn_p_kernel(q_ref, k0_ref, k1_ref, k2_ref, v0_ref, v1_ref, v2_ref, bias_ref, o_ref):
    i = pl.program_id(1)
    kwin = jnp.concatenate([k0_ref[...], k1_ref[...], k2_ref[...]], axis=0)
    vwin = jnp.concatenate([v0_ref[...], v1_ref[...], v2_ref[...]], axis=0)
    nk = kwin.shape[0]
    col = lax.broadcasted_iota(jnp.int32, (1, nk), 1)
    start_mask = jnp.where(col // Q_BLOCK + i - (KEY_BLOCKS - 1) >= 0, 0.0, NEG_INF).astype(F32)
    lane = lax.broadcasted_iota(jnp.int32, (1, LANES), 1)
    for pair in range(N_HEADS // 2):
        sl = slice(pair * LANES, (pair + 1) * LANES)
        q2 = q_ref[:, sl]
        k2 = kwin[:, sl]
        v2 = vwin[:, sl]
        acc = jnp.zeros((Q_BLOCK, LANES), F32)
        for half in range(2):
            hmask = (lane // HEAD_DIM) == half
            qm = jnp.where(hmask, q2, jnp.zeros_like(q2))
            vm = jnp.where(hmask, v2, jnp.zeros_like(v2))
            s = lax.dot_general(qm, k2, (((1,), (1,)), ((), ())), preferred_element_type=F32)
            s = s + bias_ref[2 * pair + half] + start_mask
            m = jnp.max(s, axis=-1, keepdims=True)
            e = jnp.exp(s - m)
            l = jnp.sum(e, axis=-1, keepdims=True)
            o = jnp.dot(e.astype(BF16), vm, preferred_element_type=F32)
            acc = acc + o / l
        o_ref[:, sl] = acc.astype(BF16)


def _attn_prompt(q, k, v, bias, batch, seq):
    w = q.shape[1]
    nqb = seq // Q_BLOCK
    kspec = lambda back: pl.BlockSpec((Q_BLOCK, w), lambda b, i: (b * nqb + jnp.maximum(i - back, 0), 0))
    return pl.pallas_call(
        _attn_p_kernel,
        grid=(batch, nqb),
        in_specs=[pl.BlockSpec((Q_BLOCK, w), lambda b, i: (b * nqb + i, 0)),
                  kspec(2), kspec(1), kspec(0), kspec(2), kspec(1), kspec(0),
                  _const_spec(bias.shape)],
        out_specs=pl.BlockSpec((Q_BLOCK, w), lambda b, i: (b * nqb + i, 0)),
        out_shape=jax.ShapeDtypeStruct((batch * seq, w), BF16),
        compiler_params=_params(2),
        name="attn_prompt",
    )(q, k, k, k, v, v, v, bias)


def _attn_s_kernel(t, q_ref, k_ref, v_ref, kv32_ref, ck_ref, cv_ref, bias_ref, o_ref, nk_ref, nv_ref):
    cache_len = ck_ref.shape[1]
    w = q_ref.shape[1]
    ck = ck_ref[0]
    cv = cv_ref[0]
    kall = jnp.concatenate([ck.astype(BF16), k_ref[...]], axis=0)
    vall = jnp.concatenate([cv.astype(BF16), v_ref[...]], axis=0)
    q = q_ref[...]
    head_of_lane = lax.broadcasted_iota(jnp.int32, (1, w), 1) // HEAD_DIM
    qs = jnp.concatenate([jnp.where(head_of_lane == h, q, jnp.zeros_like(q)) for h in range(N_HEADS)], axis=0)
    s = lax.dot_general(qs, kall, (((1,), (1,)), ((), ())), preferred_element_type=F32) + bias_ref[...]
    m = jnp.max(s, axis=-1, keepdims=True)
    e = jnp.exp(s - m)
    l = jnp.sum(e, axis=-1, keepdims=True)
    o_all = jnp.dot(e.astype(BF16), vall, preferred_element_type=F32) / l
    o = jnp.zeros((t, w), F32)
    for h in range(N_HEADS):
        o = o + jnp.where(head_of_lane == h, o_all[h * t:(h + 1) * t], 0.0)
    o_ref[...] = o.astype(BF16)
    nk_ref[0, :cache_len - t, :] = ck[t:]
    nk_ref[0, cache_len - t:, :] = kv32_ref[:, :w]
    nv_ref[0, :cache_len - t, :] = cv[t:]
    nv_ref[0, cache_len - t:, :] = kv32_ref[:, w:]


def _attn_sample(q, k, v, kv32, cache_k, cache_v, bias, row0, dec_b, t):
    w = q.shape[1]
    cache_len = cache_k.shape[1]
    assert row0 % t == 0
    blk0 = row0 // t
    tok = lambda width: pl.BlockSpec((t, width), lambda b: (blk0 + b, 0))
    cache = pl.BlockSpec((1, cache_len, w), lambda b: (b, 0, 0))
    return pl.pallas_call(
        functools.partial(_attn_s_kernel, t),
        grid=(dec_b,),
        in_specs=[tok(w), tok(w), tok(w), tok(2 * w), cache, cache, _const_spec(bias.shape)],
        out_specs=[pl.BlockSpec((t, w), lambda b: (b, 0)), cache, cache],
        out_shape=[jax.ShapeDtypeStruct((dec_b * t, w), BF16),
                   jax.ShapeDtypeStruct(cache_k.shape, F32), jax.ShapeDtypeStruct(cache_v.shape, F32)],
        compiler_params=_params(1),
        name="attn_sample",
    )(q, k, v, kv32, cache_k, cache_v, bias)


def _rnn_kernel(tm, rx_ref, rg_ref, conv0_ref, h0_ref, cw_ref, cb_ref, wa_ref, ba_ref, wx_ref, bx_ref, lam_ref,
                rnn_ref, conv_out_ref, h_out_ref, ext_ref, a_ref, b_ref, hs_ref, h_ref):
    j = pl.program_id(1)

    @pl.when(j == 0)
    def _():
        ext_ref[:SUBLANES, :] = conv0_ref[0]
        h_ref[...] = h0_ref[0]

    ext_ref[SUBLANES:, :] = rx_ref[...]
    cw = cw_ref[...]
    xc = cb_ref[...]
    for tap in range(CONV_W):
        off = SUBLANES - (CONV_W - 1) + tap
        xc = xc + ext_ref[off:off + tm, :] * cw[tap:tap + 1, :]
    tail = ext_ref[tm:tm + SUBLANES, :]
    ext_ref[:SUBLANES, :] = tail
    conv_out_ref[0] = tail

    xb = xc.astype(BF16)
    r = jax.nn.sigmoid(jnp.dot(xb, wa_ref[...], preferred_element_type=F32) + ba_ref[...])
    gi = jax.nn.sigmoid(jnp.dot(xb, wx_ref[...], preferred_element_type=F32) + bx_ref[...])
    neg_lam = -lam_ref[...]
    softplus = jnp.maximum(neg_lam, 0.0) + jnp.log1p(jnp.exp(-jnp.abs(neg_lam)))
    log_a = (-RG_C) * r * softplus
    a = jnp.exp(log_a)
    b = jnp.sqrt(-jnp.tanh(log_a) * (a * a + 1.0)) * (gi * xc)

    sub = lax.broadcasted_iota(jnp.int32, a.shape, 0) % SUBLANES
    shift = 1
    while shift < SUBLANES:
        keep = sub >= shift
        a_prev = pltpu.roll(a, shift, axis=0)
        b_prev = pltpu.roll(b, shift, axis=0)
        b = jnp.where(keep, a * b_prev + b, b)
        a = jnp.where(keep, a * a_prev, a)
        shift *= 2
    a_ref[...] = a
    b_ref[...] = b

    def group(g, h):
        r0 = pl.multiple_of(g * SUBLANES, SUBLANES)
        hg = b_ref[pl.ds(r0, SUBLANES), :] + a_ref[pl.ds(r0, SUBLANES), :] * h
        hs_ref[pl.ds(r0, SUBLANES), :] = hg
        return hg[SUBLANES - 1:SUBLANES, :]

    h_last = lax.fori_loop(0, tm // SUBLANES, group, h_ref[...])
    h_ref[...] = h_last
    h_out_ref[0] = h_last
    rnn_ref[...] = (hs_ref[...] * jax.nn.gelu(rg_ref[...].astype(F32))).astype(BF16)


def _rnn(rx, rgate, conv0, h0, cw, cb, wa, ba, wx, bx, lam, row0, batch, seq, tm):
    w = rx.shape[1]
    assert row0 % tm == 0 and seq % tm == 0 and tm % SUBLANES == 0
    blk0 = row0 // tm
    nt = seq // tm
    tok = pl.BlockSpec((tm, w), lambda b, j: (blk0 + b * nt + j, 0))
    state = lambda rows: pl.BlockSpec((1, rows, w), lambda b, j: (b, 0, 0))
    return pl.pallas_call(
        functools.partial(_rnn_kernel, tm),
        grid=(batch, nt),
        in_specs=[tok, tok, state(SUBLANES), state(1),
                  _const_spec(cw.shape), _const_spec((1, w)), _const_spec((w, w)), _const_spec((1, w)),
                  _const_spec((w, w)), _const_spec((1, w)), _const_spec((1, w))],
        out_specs=[pl.BlockSpec((tm, w), lambda b, j: (b * nt + j, 0)), state(SUBLANES), state(1)],
        out_shape=[jax.ShapeDtypeStruct((batch * seq, w), BF16),
                   jax.ShapeDtypeStruct((batch, SUBLANES, w), F32),
                   jax.ShapeDtypeStruct((batch, 1, w), F32)],
        scratch_shapes=[pltpu.VMEM((tm + SUBLANES, w), F32), pltpu.VMEM((tm, w), F32), pltpu.VMEM((tm, w), F32),
                        pltpu.VMEM((tm, w), F32), pltpu.VMEM((1, w), F32)],
        compiler_params=_params(2),
        name="rnn",
    )(rx, rgate, conv0, h0, cw, cb, wa, ba, wx, bx, lam)


def _merge_kernel(n_prompt_tiles,
                  xp_ref, xs_ref, attp_ref, atts_ref, rnnp_ref, rnns_ref, ga_ref, gr_ref,
                  watt_ref, wrnn_ref, wout_ref, g2_ref, wr_ref, br_ref,
                  h_ref, hn_ref, ri_ref, wt_ref, cnt_ref, carry_ref):
    i = pl.program_id(0)
    is_p = i < n_prompt_tiles

    @pl.when(i == 0)
    def _():
        carry_ref[...] = jnp.zeros_like(carry_ref)

    x = jnp.where(is_p, xp_ref[...], xs_ref[...])
    att = jnp.where(is_p, attp_ref[...], atts_ref[...])
    rnn = jnp.where(is_p, rnnp_ref[...], rnns_ref[...])
    ya = jnp.dot(att, watt_ref[...], preferred_element_type=F32)
    yr = jnp.dot(rnn, wrnn_ref[...], preferred_element_type=F32)
    mixed = jax.nn.sigmoid(ga_ref[...].astype(F32)) * ya + jax.nn.sigmoid(gr_ref[...].astype(F32)) * yr
    h = x + jnp.dot(mixed.astype(BF16), wout_ref[...], preferred_element_type=F32)
    h_ref[...] = h
    hn = _rmsnorm(h, g2_ref[...])
    hn_ref[...] = hn

    logits = jnp.dot(hn.astype(BF16), wr_ref[...], preferred_element_type=F32) + br_ref[...]
    tm = logits.shape[0]
    lane = lax.broadcasted_iota(jnp.int32, (tm, LANES), 1)
    low = jnp.float32(-3e38)
    big = jnp.int32(1 << 20)
    lg = jnp.where(lane < N_GROUPS, logits, low)
    mg = jnp.max(lg, axis=-1, keepdims=True)
    p_grp = 1.0 / jnp.sum(jnp.exp(lg - mg), axis=-1, keepdims=True)
    grp = jnp.min(jnp.where(lg == mg, lane, big), axis=-1, keepdims=True)
    lo = N_GROUPS + EXPERTS_PER_GROUP * grp
    le = jnp.where((lane >= lo) & (lane < lo + EXPERTS_PER_GROUP), logits, low)
    m1 = jnp.max(le, axis=-1, keepdims=True)
    i1 = jnp.min(jnp.where(le == m1, lane, big), axis=-1, keepdims=True)
    le2 = jnp.where(lane == i1, low, le)
    m2 = jnp.max(le2, axis=-1, keepdims=True)
    i2 = jnp.min(jnp.where(le2 == m2, lane, big), axis=-1, keepdims=True)
    e2 = jnp.exp(m2 - m1)
    inv = 1.0 / (1.0 + e2)
    w1 = p_grp * inv
    w2 = p_grp * (e2 * inv)
    x1 = i1 - N_GROUPS
    x2 = i2 - N_GROUPS

    onehot = jnp.where((lane == x1) | (lane == x2), 1.0, 0.0).astype(BF16)
    rowi = lax.broadcasted_iota(jnp.int32, (tm, tm), 0)
    coli = lax.broadcasted_iota(jnp.int32, (tm, tm), 1)
    tri = jnp.where(coli < rowi, 1.0, 0.0).astype(BF16)
    before = jnp.dot(tri, onehot, preferred_element_type=F32) + carry_ref[...]
    r1 = jnp.sum(jnp.where(lane == x1, before, 0.0), axis=-1, keepdims=True).astype(jnp.int32)
    r2 = jnp.sum(jnp.where(lane == x2, before, 0.0), axis=-1, keepdims=True).astype(jnp.int32)
    carry = carry_ref[...] + jnp.sum(onehot.astype(F32), axis=0, keepdims=True)
    carry_ref[...] = carry
    cnt_ref[...] = carry

    zi = jnp.zeros((tm, LANES), jnp.int32)
    ri_ref[...] = jnp.where(lane == 0, x1, jnp.where(lane == 1, x2, jnp.where(lane == 2, r1, jnp.where(lane == 3, r2, zi))))
    wt_ref[...] = jnp.where(lane == 0, w1, jnp.where(lane == 1, w2, 0.0))


def _merge(xp, xs, att_p, att_s, rnn_p, rnn_s, ga, gr, w_att, w_rnn, w_out, g2, w_r, b_r):
    n_p, d = xp.shape
    n = ga.shape[0]
    npt = n_p // ROW_TILE
    aw = att_p.shape[1]
    rw = rnn_p.shape[1]
    pspec = lambda w: pl.BlockSpec((ROW_TILE, w), lambda i: (jnp.minimum(i, npt - 1), 0))
    sspec = lambda w: pl.BlockSpec((ROW_TILE, w), lambda i: (0, 0))
    row = lambda w: pl.BlockSpec((ROW_TILE, w), lambda i: (i, 0))
    return pl.pallas_call(
        functools.partial(_merge_kernel, npt),
        grid=(n // ROW_TILE,),
        in_specs=[pspec(d), sspec(d), pspec(aw), sspec(aw), pspec(rw), sspec(rw), row(d), row(d),
                  _const_spec(w_att.shape), _const_spec(w_rnn.shape), _const_spec(w_out.shape),
                  _const_spec((1, d)), _const_spec(w_r.shape), _const_spec((1, LANES))],
        out_specs=[row(d), row(d), row(LANES), row(LANES), pl.BlockSpec((1, LANES), lambda i: (0, 0))],
        out_shape=[jax.ShapeDtypeStruct((n, d), F32), jax.ShapeDtypeStruct((n, d), F32),
                   jax.ShapeDtypeStruct((n, LANES), jnp.int32), jax.ShapeDtypeStruct((n, LANES), F32),
                   jax.ShapeDtypeStruct((1, LANES), F32)],
        scratch_shapes=[pltpu.VMEM((1, LANES), F32)],
        compiler_params=_params(1),
        name="merge",
    )(xp, xs, att_p, att_s, rnn_p, rnn_s, ga, gr, w_att, w_rnn, w_out, g2, w_r, b_r)


def _row_copy(src_ref, src_row, dst_ref, dst_row, sem):
    return pltpu.make_async_copy(src_ref.at[pl.ds(src_row, 1)], dst_ref.at[pl.ds(dst_row, 1)], sem)


def _dispatch_kernel(pos_ref, hn_ref, init_ref, xs_ref, sem):
    del init_ref
    tm = hn_ref.shape[0]

    def issue(t, c):
        for k in range(TOP_K):
            _row_copy(hn_ref, t, xs_ref, pos_ref[TOP_K * t + k], sem).start()
        return c

    lax.fori_loop(0, tm, issue, 0)

    def drain(t, c):
        for k in range(TOP_K):
            _row_copy(hn_ref, 0, xs_ref, 0, sem).wait()
        return c

    lax.fori_loop(0, tm, drain, 0)


def _dispatch(pos_flat, hn, rows):
    n, d = hn.shape
    init = jnp.zeros((rows, d), hn.dtype)
    return pl.pallas_call(
        _dispatch_kernel,
        grid=(n // ROW_TILE,),
        in_specs=[pl.BlockSpec((TOP_K * ROW_TILE,), lambda i: (i,), memory_space=pltpu.SMEM),
                  pl.BlockSpec((ROW_TILE, d), lambda i: (i, 0)),
                  pl.BlockSpec(memory_space=pl.ANY)],
        out_specs=pl.BlockSpec(memory_space=pl.ANY),
        out_shape=jax.ShapeDtypeStruct((rows, d), hn.dtype),
        scratch_shapes=[pltpu.SemaphoreType.DMA(())],
        input_output_aliases={2: 0},
        compiler_params=_params(1),
        name="dispatch",
    )(pos_flat, hn, init)


def _experts_kernel(be_ref, nu_ref, x_ref, wg_ref, wu_ref, wd_ref, y_ref):
    i = pl.program_id(0)

    @pl.when(i < nu_ref[0])
    def _():
        x = x_ref[...].astype(BF16)
        g = jnp.dot(x, wg_ref[0], preferred_element_type=F32)
        u = jnp.dot(x, wu_ref[0], preferred_element_type=F32)
        hid = (jax.nn.silu(g) * u).astype(BF16)
        y_ref[...] = jnp.dot(hid, wd_ref[0], preferred_element_type=F32)

    @pl.when(i >= nu_ref[0])
    def _():
        y_ref[...] = jnp.zeros_like(y_ref)


def _experts(block_expert, n_used, xs, wg, wu, wd):
    rows, d = xs.shape
    de = wg.shape[2]
    nb = rows // EXPERT_BLOCK
    grid_spec = pltpu.PrefetchScalarGridSpec(
        num_scalar_prefetch=2,
        grid=(nb,),
        in_specs=[pl.BlockSpec((EXPERT_BLOCK, d), lambda i, be, nu: (jnp.minimum(i, nu[0] - 1), 0)),
                  pl.BlockSpec((1, d, de), lambda i, be, nu: (be[i], 0, 0)),
                  pl.BlockSpec((1, d, de), lambda i, be, nu: (be[i], 0, 0)),
                  pl.BlockSpec((1, de, d), lambda i, be, nu: (be[i], 0, 0))],
        out_specs=pl.BlockSpec((EXPERT_BLOCK, d), lambda i, be, nu: (i, 0)),
    )
    return pl.pallas_call(
        _experts_kernel,
        grid_spec=grid_spec,
        out_shape=jax.ShapeDtypeStruct((rows, d), F32),
        compiler_params=_params(1),
        name="experts",
    )(block_expert, n_used, xs, wg, wu, wd)


def _combine_kernel(pos_ref, h_ref, wt_ref, gf_ref, y_hbm_ref, out_ref, buf_ref, sem):
    tm = h_ref.shape[0]

    def issue(t, c):
        for k in range(TOP_K):
            _row_copy(y_hbm_ref, pos_ref[TOP_K * t + k], buf_ref.at[k], t, sem).start()
        return c

    lax.fori_loop(0, tm, issue, 0)

    def drain(t, c):
        for k in range(TOP_K):
            _row_copy(y_hbm_ref, 0, buf_ref.at[k], 0, sem).wait()
        return c

    lax.fori_loop(0, tm, drain, 0)
    wt = wt_ref[...]
    moe = buf_ref[0] * wt[:, 0:1] + buf_ref[1] * wt[:, 1:2]
    out_ref[...] = _rmsnorm(h_ref[...] + moe, gf_ref[...])


def _combine(pos_flat, h, wt, gf, y_sorted, row0, n_rows):
    d = h.shape[1]
    assert row0 % ROW_TILE == 0 and n_rows % ROW_TILE == 0
    blk0 = row0 // ROW_TILE
    return pl.pallas_call(
        _combine_kernel,
        grid=(n_rows // ROW_TILE,),
        in_specs=[pl.BlockSpec((TOP_K * ROW_TILE,), lambda i: (blk0 + i,), memory_space=pltpu.SMEM),
                  pl.BlockSpec((ROW_TILE, d), lambda i: (blk0 + i, 0)),
                  pl.BlockSpec((ROW_TILE, LANES), lambda i: (blk0 + i, 0)),
                  _const_spec((1, d)),
                  pl.BlockSpec(memory_space=pl.ANY)],
        out_specs=pl.BlockSpec((ROW_TILE, d), lambda i: (i, 0)),
        out_shape=jax.ShapeDtypeStruct((n_rows, d), F32),
        scratch_shapes=[pltpu.VMEM((TOP_K, ROW_TILE, d), F32), pltpu.SemaphoreType.DMA(())],
        compiler_params=_params(1),
        name="combine",
    )(pos_flat, h, wt, gf, y_sorted)


def _rel_bias_table(rel_bias, dist):
    idx = jnp.clip(dist, -MAX_REL, MAX_REL) + MAX_REL
    return rel_bias[:, idx].astype(F32)


def _prompt_bias(rel_bias):
    r = jnp.arange(Q_BLOCK)[:, None]
    j = jnp.arange(KEY_BLOCKS * Q_BLOCK)[None, :]
    bias = _rel_bias_table(rel_bias, r + (KEY_BLOCKS - 1) * Q_BLOCK - j)
    cq = r // CHUNK
    ck = j // CHUNK
    in_band = (ck >= cq) & (ck <= cq + LEFT_CHUNKS)
    return jnp.where(in_band[None], bias, NEG_INF)


def _sample_bias(rel_bias, cache_len, t):
    qi = jnp.arange(t)[:, None]
    kj = jnp.arange(cache_len + t)[None, :]
    bias = _rel_bias_table(rel_bias, qi + cache_len - kj)
    bias = jnp.where((kj >= cache_len - BAND_PAST)[None], bias, NEG_INF)
    return bias.reshape(N_HEADS * t, cache_len + t)


def _block_diag(w):
    nb, bw, _ = w.shape
    eye = jnp.eye(nb, dtype=w.dtype)
    return jnp.einsum("ncd,nm->ncmd", w, eye).reshape(nb * bw, nb * bw)


def kernel(x_prompt, x_sample, cache_k, cache_v, state_conv, state_h, norm1_g, w_in, b_in, rel_bias, w_att_branch, conv_w, conv_b, w_rec_gate, b_rec_gate, w_in_gate, b_in_gate, lru_lambda, w_rnn_branch, w_out, norm2_g, w_router_group, b_router_group, w_router_expert, b_router_expert, w_e_gate, w_e_up, w_e_down, final_norm_g):
    batch, seq, d = x_prompt.shape
    dec_b, dec_t, _ = x_sample.shape
    depth = w_in.shape[0]
    assert depth == 1
    att_w = N_HEADS * HEAD_DIM
    rnn_w = conv_w.shape[2]
    cache_len = cache_k.shape[2]
    n_p = batch * seq
    n_s = dec_b * dec_t
    n = n_p + n_s
    keep = min(BAND_PAST + CHUNK, seq)
    row2 = lambda a: a.reshape(1, -1).astype(F32)

    xp = x_prompt.reshape(n_p, d)
    xs = x_sample.reshape(n_s, d)
    q, k, v, kv32, rx, rgate, ga, gr = _proj(xp, xs, row2(norm1_g[0]), w_in[0].astype(BF16), row2(b_in[0]),
                                             att_w, rnn_w)

    att_p = _attn_prompt(q, k, v, _prompt_bias(rel_bias[0]), batch, seq)
    att_s, new_k_s, new_v_s = _attn_sample(
        q, k, v, kv32, cache_k[0].reshape(dec_b, cache_len, att_w), cache_v[0].reshape(dec_b, cache_len, att_w),
        _sample_bias(rel_bias[0], cache_len, dec_t), n_p, dec_b, dec_t)

    rnn_args = (conv_w[0].astype(F32), row2(conv_b[0]),
                _block_diag(w_rec_gate[0]).astype(BF16), row2(b_rec_gate[0]),
                _block_diag(w_in_gate[0]).astype(BF16), row2(b_in_gate[0]), row2(lru_lambda[0]))
    pad_state = lambda s: jnp.pad(s.astype(F32), ((0, 0), (SUBLANES - (CONV_W - 1), 0), (0, 0)))
    rnn_p, conv_p, h_p = _rnn(rx, rgate, jnp.zeros((batch, SUBLANES, rnn_w), F32), jnp.zeros((batch, 1, rnn_w), F32),
                              *rnn_args, 0, batch, seq, ROW_TILE)
    rnn_s, conv_s, h_s = _rnn(rx, rgate, pad_state(state_conv[0]), state_h[0].astype(F32)[:, None, :],
                              *rnn_args, n_p, dec_b, dec_t, dec_t)

    w_r = jnp.zeros((d, LANES), F32).at[:, :N_GROUPS].set(w_router_group[0])
    w_r = w_r.at[:, N_GROUPS:N_GROUPS + N_EXPERTS].set(w_router_expert[0]).astype(BF16)
    b_r = jnp.zeros((1, LANES), F32).at[0, :N_GROUPS].set(b_router_group[0])
    b_r = b_r.at[0, N_GROUPS:N_GROUPS + N_EXPERTS].set(b_router_expert[0])
    h, hn, ri, wt, cnt = _merge(xp, xs, att_p, att_s, rnn_p, rnn_s, ga, gr,
                                w_att_branch[0].astype(BF16), w_rnn_branch[0].astype(BF16), w_out[0].astype(BF16),
                                row2(norm2_g[0]), w_r, b_r)

    counts = cnt[0, :N_EXPERTS].astype(jnp.int32)
    padded = (counts + EXPERT_BLOCK - 1) // EXPERT_BLOCK * EXPERT_BLOCK
    pad_end = jnp.cumsum(padded)
    pad_start = pad_end - padded
    pos_flat = (pad_start[ri[:, :TOP_K]] + ri[:, TOP_K:2 * TOP_K]).reshape(n * TOP_K)
    n_blocks = -(-(n * TOP_K) // EXPERT_BLOCK) + N_EXPERTS
    rows = n_blocks * EXPERT_BLOCK
    block_start = jnp.arange(n_blocks, dtype=jnp.int32) * EXPERT_BLOCK
    block_expert = jnp.minimum(jnp.searchsorted(pad_end, block_start, side="right"), N_EXPERTS - 1).astype(jnp.int32)
    n_used = (pad_end[-1:] // EXPERT_BLOCK).astype(jnp.int32)

    x_sorted = _dispatch(pos_flat, hn, rows)
    y_sorted = _experts(block_expert, n_used, x_sorted,
                        w_e_gate[0].astype(BF16), w_e_up[0].astype(BF16), w_e_down[0].astype(BF16))
    gf = row2(final_norm_g)
    y_p = _combine(pos_flat, h, wt, gf, y_sorted, 0, n_p)
    y_s = _combine(pos_flat, h, wt, gf, y_sorted, n_p, n_s)

    def kv_state(col0):
        rows_p = kv32[:n_p, col0:col0 + att_w].reshape(batch, seq, N_HEADS, HEAD_DIM)
        return rows_p[:, seq - keep:][None]

    heads = lambda a: a.reshape(1, dec_b, cache_len, N_HEADS, HEAD_DIM)
    return (y_p.reshape(batch, seq, d), y_s.reshape(dec_b, dec_t, d),
            kv_state(0), kv_state(att_w),
            conv_p[:, SUBLANES - (CONV_W - 1):][None], h_p[:, 0][None],
            heads(new_k_s), heads(new_v_s),
            conv_s[:, SUBLANES - (CONV_W - 1):][None], h_s[:, 0][None])
```

```python
import functools

import jax
import jax.numpy as jnp
from jax import lax
from jax.experimental import pallas as pl
from jax.experimental.pallas import tpu as pltpu

CHUNK = 64
LEFT_CHUNKS = 8
BAND_PAST = LEFT_CHUNKS * CHUNK
N_HEADS = 8
HEAD_DIM = 64
MAX_REL = 256
NEG_INF = -1e30
RNN_BLOCKS = 8
CONV_W = 4
RG_C = 8.0
N_GROUPS = 4
EXPERTS_PER_GROUP = 8
N_EXPERTS = N_GROUPS * EXPERTS_PER_GROUP
TOP_K = 2
EPS = 1e-6

LANES = 128
SUBLANES = 8
ROW_TILE = 512
Q_BLOCK = 256
KEY_BLOCKS = 3
KEY_WINDOW = KEY_BLOCKS * Q_BLOCK
BIAS_BASE = 1024
EXPERT_BLOCK = 256
DMA_UNROLL = 8
VMEM_LIMIT = 56 * 1024 * 1024

BF16 = jnp.bfloat16
F32 = jnp.float32


def _params(n_axes, vmem=VMEM_LIMIT):
    return pltpu.CompilerParams(dimension_semantics=("arbitrary",) * n_axes, vmem_limit_bytes=vmem)


def _const_spec(shape):
    zeros = (0,) * len(shape)
    return pl.BlockSpec(shape, lambda *_: zeros, pipeline_mode=pl.Buffered(1))


def _rmsnorm(x, g):
    return x * lax.rsqrt(jnp.mean(x * x, axis=-1, keepdims=True) + EPS) * g


def _proj_kernel(n_prompt_tiles, state_tiles, att_w, rnn_w, d_model,
                 xp_ref, xs_ref, g_ref, w_ref, b_ref,
                 q_ref, k_ref, v_ref, kvt_ref, rx_ref, rg_ref, ga_ref, gr_ref):
    i = pl.program_id(0)
    x = jnp.where(i < n_prompt_tiles, xp_ref[...], xs_ref[...])
    xn = _rmsnorm(x, g_ref[...]).astype(BF16)

    def seg(lo, width):
        return jnp.dot(xn, w_ref[:, lo:lo + width], preferred_element_type=F32) + b_ref[:, lo:lo + width]

    q = seg(0, att_w)
    k = seg(att_w, att_w)
    v = seg(2 * att_w, att_w)
    q_ref[...] = (q * (HEAD_DIM ** -0.5)).astype(BF16)
    k_ref[...] = k.astype(BF16)
    v_ref[...] = v.astype(BF16)

    @pl.when(functools.reduce(jnp.logical_or, [i == t for t in state_tiles]))
    def _():
        kvt_ref[:, :att_w] = k
        kvt_ref[:, att_w:] = v

    o = 3 * att_w
    rx_ref[...] = seg(o, rnn_w)
    rg_ref[...] = seg(o + rnn_w, rnn_w).astype(BF16)
    ga_ref[...] = seg(o + 2 * rnn_w, d_model).astype(BF16)
    gr_ref[...] = seg(o + 2 * rnn_w + d_model, d_model).astype(BF16)


def _proj(xp, xs, g1, w_in, b_in, att_w, rnn_w, state_tiles):
    n_p, d = xp.shape
    n_s = xs.shape[0]
    assert n_p % ROW_TILE == 0 and n_s == ROW_TILE
    n = n_p + n_s
    npt = n_p // ROW_TILE
    pw = w_in.shape[1]
    row = lambda w: pl.BlockSpec((ROW_TILE, w), lambda i: (i, 0))
    outs = [(att_w, BF16), (att_w, BF16), (att_w, BF16), (2 * att_w, F32), (rnn_w, F32), (rnn_w, BF16),
            (d, BF16), (d, BF16)]
    slot = lambda i: sum(jnp.where(i > t, 1, 0) for t in state_tiles)
    out_specs = [row(w) for w, _ in outs]
    out_specs[3] = pl.BlockSpec((ROW_TILE, 2 * att_w), lambda i: (slot(i), 0))
    out_shape = [jax.ShapeDtypeStruct((n, w), dt) for w, dt in outs]
    out_shape[3] = jax.ShapeDtypeStruct((len(state_tiles) * ROW_TILE, 2 * att_w), F32)
    return pl.pallas_call(
        functools.partial(_proj_kernel, npt, state_tiles, att_w, rnn_w, d),
        grid=(n // ROW_TILE,),
        in_specs=[pl.BlockSpec((ROW_TILE, d), lambda i: (jnp.minimum(i, npt - 1), 0)),
                  pl.BlockSpec((ROW_TILE, d), lambda i: (0, 0)),
                  _const_spec((1, d)), _const_spec((d, pw)), _const_spec((1, pw))],
        out_specs=out_specs,
        out_shape=out_shape,
        compiler_params=_params(1),
        name="proj",
    )(xp, xs, g1, w_in, b_in)


def _attn_p_kernel(q_ref, k0_ref, k1_ref, k2_ref, v0_ref, v1_ref, v2_ref, base_ref, o_ref, bias_ref):
    b = pl.program_id(0)
    i = pl.program_id(1)

    @pl.when((b == 0) & (i == 0))
    def _():
        r = lax.broadcasted_iota(jnp.int32, (Q_BLOCK, KEY_WINDOW), 0) // CHUNK
        c = lax.broadcasted_iota(jnp.int32, (Q_BLOCK, KEY_WINDOW), 1) // CHUNK
        in_band = (c >= r) & (c <= r + LEFT_CHUNKS)
        for h in range(N_HEADS):
            rows = jnp.broadcast_to(base_ref[h:h + 1, :], (Q_BLOCK, BIAS_BASE))
            toeplitz = pltpu.roll(rows, BIAS_BASE - (Q_BLOCK - 1), axis=1, stride=1, stride_axis=0)
            bias_ref[h] = jnp.where(in_band, toeplitz[:, :KEY_WINDOW], NEG_INF)

    kwin = jnp.concatenate([k0_ref[...], k1_ref[...], k2_ref[...]], axis=0)
    vwin = jnp.concatenate([v0_ref[...], v1_ref[...], v2_ref[...]], axis=0)
    col = lax.broadcasted_iota(jnp.int32, (1, KEY_WINDOW), 1)
    start_mask = jnp.where(col // Q_BLOCK + i - (KEY_BLOCKS - 1) >= 0, 0.0, NEG_INF).astype(F32)
    lane = lax.broadcasted_iota(jnp.int32, (1, LANES), 1)
    for pair in range(N_HEADS // 2):
        sl = slice(pair * LANES, (pair + 1) * LANES)
        q2 = q_ref[:, sl]
        k2 = kwin[:, sl]
        v2 = vwin[:, sl]
        acc = jnp.zeros((Q_BLOCK, LANES), F32)
        for half in range(2):
            hmask = (lane // HEAD_DIM) == half
            qm = jnp.where(hmask, q2, jnp.zeros_like(q2))
            vm = jnp.where(hmask, v2, jnp.zeros_like(v2))
            s = lax.dot_general(qm, k2, (((1,), (1,)), ((), ())), preferred_element_type=F32)
            s = s + bias_ref[2 * pair + half] + start_mask
            m = jnp.max(s, axis=-1, keepdims=True)
            e = jnp.exp(s - m)
            l = jnp.sum(e, axis=-1, keepdims=True)
            o = jnp.dot(e.astype(BF16), vm, preferred_element_type=F32)
            acc = acc + o / l
        o_ref[:, sl] = acc.astype(BF16)


def _attn_prompt(q, k, v, base, batch, seq):
    w = q.shape[1]
    nqb = seq // Q_BLOCK
    kspec = lambda back: pl.BlockSpec((Q_BLOCK, w), lambda b, i: (b * nqb + jnp.maximum(i - back, 0), 0))
    return pl.pallas_call(
        _attn_p_kernel,
        grid=(batch, nqb),
        in_specs=[pl.BlockSpec((Q_BLOCK, w), lambda b, i: (b * nqb + i, 0)),
                  kspec(2), kspec(1), kspec(0), kspec(2), kspec(1), kspec(0),
                  _const_spec(base.shape)],
        out_specs=pl.BlockSpec((Q_BLOCK, w), lambda b, i: (b * nqb + i, 0)),
        out_shape=jax.ShapeDtypeStruct((batch * seq, w), BF16),
        scratch_shapes=[pltpu.VMEM((N_HEADS, Q_BLOCK, KEY_WINDOW), F32)],
        compiler_params=_params(2),
        name="attn_prompt",
    )(q, k, k, k, v, v, v, base)


def _attn_s_kernel(t, q_ref, k_ref, v_ref, kvt_ref, ck_ref, cv_ref, bias_ref, o_ref, nk_ref, nv_ref):
    cache_len = ck_ref.shape[1]
    w = q_ref.shape[1]
    ck = ck_ref[0]
    cv = cv_ref[0]
    kall = jnp.concatenate([ck.astype(BF16), k_ref[...]], axis=0)
    vall = jnp.concatenate([cv.astype(BF16), v_ref[...]], axis=0)
    q = q_ref[...]
    head_of_lane = lax.broadcasted_iota(jnp.int32, (1, w), 1) // HEAD_DIM
    qs = jnp.concatenate([jnp.where(head_of_lane == h, q, jnp.zeros_like(q)) for h in range(N_HEADS)], axis=0)
    s = lax.dot_general(qs, kall, (((1,), (1,)), ((), ())), preferred_element_type=F32) + bias_ref[...]
    m = jnp.max(s, axis=-1, keepdims=True)
    e = jnp.exp(s - m)
    l = jnp.sum(e, axis=-1, keepdims=True)
    o_all = jnp.dot(e.astype(BF16), vall, preferred_element_type=F32) / l
    o = jnp.zeros((t, w), F32)
    for h in range(N_HEADS):
        o = o + jnp.where(head_of_lane == h, o_all[h * t:(h + 1) * t], 0.0)
    o_ref[...] = o.astype(BF16)
    nk_ref[0, :cache_len - t, :] = ck[t:]
    nk_ref[0, cache_len - t:, :] = kvt_ref[:, :w]
    nv_ref[0, :cache_len - t, :] = cv[t:]
    nv_ref[0, cache_len - t:, :] = kvt_ref[:, w:]


def _attn_sample(q, k, v, kvt, cache_k, cache_v, bias, row0, kvt_row0, dec_b, t):
    w = q.shape[1]
    cache_len = cache_k.shape[1]
    assert row0 % t == 0 and kvt_row0 % t == 0
    tok = lambda width, r0: pl.BlockSpec((t, width), lambda b: (r0 // t + b, 0))
    cache = pl.BlockSpec((1, cache_len, w), lambda b: (b, 0, 0))
    return pl.pallas_call(
        functools.partial(_attn_s_kernel, t),
        grid=(dec_b,),
        in_specs=[tok(w, row0), tok(w, row0), tok(w, row0), tok(2 * w, kvt_row0), cache, cache,
                  _const_spec(bias.shape)],
        out_specs=[pl.BlockSpec((t, w), lambda b: (b, 0)), cache, cache],
        out_shape=[jax.ShapeDtypeStruct((dec_b * t, w), BF16),
                   jax.ShapeDtypeStruct(cache_k.shape, F32), jax.ShapeDtypeStruct(cache_v.shape, F32)],
        compiler_params=_params(1),
        name="attn_sample",
    )(q, k, v, kvt, cache_k, cache_v, bias)


def _rnn_kernel(tm, rx_ref, rg_ref, conv0_ref, h0_ref, cw_ref, cb_ref, wa_ref, ba_ref, wx_ref, bx_ref, lam_ref,
                rnn_ref, conv_out_ref, h_out_ref, ext_ref, a_ref, b_ref, hs_ref, h_ref):
    j = pl.program_id(1)

    @pl.when(j == 0)
    def _():
        ext_ref[:SUBLANES, :] = conv0_ref[0]
        h_ref[...] = h0_ref[0]

    ext_ref[SUBLANES:, :] = rx_ref[...]
    cw = cw_ref[...]
    xc = cb_ref[...]
    for tap in range(CONV_W):
        off = SUBLANES - (CONV_W - 1) + tap
        xc = xc + ext_ref[off:off + tm, :] * cw[tap:tap + 1, :]
    tail = ext_ref[tm:tm + SUBLANES, :]
    ext_ref[:SUBLANES, :] = tail
    conv_out_ref[0] = tail

    xb = xc.astype(BF16)
    r = jax.nn.sigmoid(jnp.dot(xb, wa_ref[...], preferred_element_type=F32) + ba_ref[...])
    gi = jax.nn.sigmoid(jnp.dot(xb, wx_ref[...], preferred_element_type=F32) + bx_ref[...])
    neg_lam = -lam_ref[...]
    softplus = jnp.maximum(neg_lam, 0.0) + jnp.log1p(jnp.exp(-jnp.abs(neg_lam)))
    log_a = (-RG_C) * r * softplus
    a = jnp.exp(log_a)
    b = jnp.sqrt(-jnp.tanh(log_a) * (a * a + 1.0)) * (gi * xc)

    sub = lax.broadcasted_iota(jnp.int32, a.shape, 0) % SUBLANES
    shift = 1
    while shift < SUBLANES:
        keep = sub >= shift
        a_prev = pltpu.roll(a, shift, axis=0)
        b_prev = pltpu.roll(b, shift, axis=0)
        b = jnp.where(keep, a * b_prev + b, b)
        a = jnp.where(keep, a * a_prev, a)
        shift *= 2
    a_ref[...] = a
    b_ref[...] = b

    def group(g, h):
        r0 = pl.multiple_of(g * SUBLANES, SUBLANES)
        hg = b_ref[pl.ds(r0, SUBLANES), :] + a_ref[pl.ds(r0, SUBLANES), :] * h
        hs_ref[pl.ds(r0, SUBLANES), :] = hg
        return hg[SUBLANES - 1:SUBLANES, :]

    h_last = lax.fori_loop(0, tm // SUBLANES, group, h_ref[...])
    h_ref[...] = h_last
    h_out_ref[0] = h_last
    rnn_ref[...] = (hs_ref[...] * jax.nn.gelu(rg_ref[...].astype(F32))).astype(BF16)


def _rnn(rx, rgate, conv0, h0, cw, cb, wa, ba, wx, bx, lam, row0, batch, seq, tm):
    w = rx.shape[1]
    assert row0 % tm == 0 and seq % tm == 0 and tm % SUBLANES == 0
    blk0 = row0 // tm
    nt = seq // tm
    tok = pl.BlockSpec((tm, w), lambda b, j: (blk0 + b * nt + j, 0))
    state = lambda rows: pl.BlockSpec((1, rows, w), lambda b, j: (b, 0, 0))
    return pl.pallas_call(
        functools.partial(_rnn_kernel, tm),
        grid=(batch, nt),
        in_specs=[tok, tok, state(SUBLANES), state(1),
                  _const_spec(cw.shape), _const_spec((1, w)), _const_spec((w, w)), _const_spec((1, w)),
                  _const_spec((w, w)), _const_spec((1, w)), _const_spec((1, w))],
        out_specs=[pl.BlockSpec((tm, w), lambda b, j: (b * nt + j, 0)), state(SUBLANES), state(1)],
        out_shape=[jax.ShapeDtypeStruct((batch * seq, w), BF16),
                   jax.ShapeDtypeStruct((batch, SUBLANES, w), F32),
                   jax.ShapeDtypeStruct((batch, 1, w), F32)],
        scratch_shapes=[pltpu.VMEM((tm + SUBLANES, w), F32), pltpu.VMEM((tm, w), F32), pltpu.VMEM((tm, w), F32),
                        pltpu.VMEM((tm, w), F32), pltpu.VMEM((1, w), F32)],
        compiler_params=_params(2),
        name="rnn",
    )(rx, rgate, conv0, h0, cw, cb, wa, ba, wx, bx, lam)


ROUTE_ROWS = SUBLANES


def _merge_kernel(n_prompt_tiles,
                  xp_ref, xs_ref, attp_ref, atts_ref, rnnp_ref, rnns_ref, ga_ref, gr_ref,
                  watt_ref, wrnn_ref, wout_ref, g2_ref, wr_ref, br_ref,
                  h_ref, hn_ref, route_ref, wt_ref, cnt_ref, carry_ref):
    i = pl.program_id(0)
    is_p = i < n_prompt_tiles

    @pl.when(i == 0)
    def _():
        carry_ref[...] = jnp.zeros_like(carry_ref)

    x = jnp.where(is_p, xp_ref[...], xs_ref[...])
    att = jnp.where(is_p, attp_ref[...], atts_ref[...])
    rnn = jnp.where(is_p, rnnp_ref[...], rnns_ref[...])
    ya = jnp.dot(att, watt_ref[...], preferred_element_type=F32)
    yr = jnp.dot(rnn, wrnn_ref[...], preferred_element_type=F32)
    mixed = jax.nn.sigmoid(ga_ref[...].astype(F32)) * ya + jax.nn.sigmoid(gr_ref[...].astype(F32)) * yr
    h = x + jnp.dot(mixed.astype(BF16), wout_ref[...], preferred_element_type=F32)
    h_ref[...] = h
    hn = _rmsnorm(h, g2_ref[...])
    hn_ref[...] = hn

    logits = jnp.dot(hn.astype(BF16), wr_ref[...], preferred_element_type=F32) + br_ref[...]
    tm = logits.shape[0]
    lane = lax.broadcasted_iota(jnp.int32, (tm, LANES), 1)
    low = jnp.float32(-3e38)
    big = jnp.int32(1 << 20)
    lg = jnp.where(lane < N_GROUPS, logits, low)
    mg = jnp.max(lg, axis=-1, keepdims=True)
    p_grp = 1.0 / jnp.sum(jnp.exp(lg - mg), axis=-1, keepdims=True)
    grp = jnp.min(jnp.where(lg == mg, lane, big), axis=-1, keepdims=True)
    lo = N_GROUPS + EXPERTS_PER_GROUP * grp
    le = jnp.where((lane >= lo) & (lane < lo + EXPERTS_PER_GROUP), logits, low)
    m1 = jnp.max(le, axis=-1, keepdims=True)
    i1 = jnp.min(jnp.where(le == m1, lane, big), axis=-1, keepdims=True)
    le2 = jnp.where(lane == i1, low, le)
    m2 = jnp.max(le2, axis=-1, keepdims=True)
    i2 = jnp.min(jnp.where(le2 == m2, lane, big), axis=-1, keepdims=True)
    e2 = jnp.exp(m2 - m1)
    inv = 1.0 / (1.0 + e2)
    w1 = p_grp * inv
    w2 = p_grp * (e2 * inv)
    x1 = i1 - N_GROUPS
    x2 = i2 - N_GROUPS

    onehot = jnp.where((lane == x1) | (lane == x2), 1.0, 0.0).astype(BF16)
    rowi = lax.broadcasted_iota(jnp.int32, (tm, tm), 0)
    coli = lax.broadcasted_iota(jnp.int32, (tm, tm), 1)
    tri = jnp.where(coli < rowi, 1.0, 0.0).astype(BF16)
    before = jnp.dot(tri, onehot, preferred_element_type=F32) + carry_ref[...]
    r1 = jnp.sum(jnp.where(lane == x1, before, 0.0), axis=-1, keepdims=True)
    r2 = jnp.sum(jnp.where(lane == x2, before, 0.0), axis=-1, keepdims=True)
    carry = carry_ref[...] + jnp.sum(onehot.astype(F32), axis=0, keepdims=True)
    carry_ref[...] = carry
    cnt_ref[...] = carry

    fields = (x1.astype(F32), x2.astype(F32), r1, r2, w1, w2)
    slab = jnp.zeros((tm, LANES), F32)
    for idx, val in enumerate(fields):
        slab = jnp.where(lane == idx, val, slab)
    route_ref[...] = slab.T[:ROUTE_ROWS, :]
    wt_ref[...] = jnp.where(lane == 0, w1, jnp.where(lane == 1, w2, 0.0))


def _merge(xp, xs, att_p, att_s, rnn_p, rnn_s, ga, gr, w_att, w_rnn, w_out, g2, w_r, b_r):
    n_p, d = xp.shape
    n = ga.shape[0]
    npt = n_p // ROW_TILE
    aw = att_p.shape[1]
    rw = rnn_p.shape[1]
    pspec = lambda w: pl.BlockSpec((ROW_TILE, w), lambda i: (jnp.minimum(i, npt - 1), 0))
    sspec = lambda w: pl.BlockSpec((ROW_TILE, w), lambda i: (0, 0))
    row = lambda w: pl.BlockSpec((ROW_TILE, w), lambda i: (i, 0))
    return pl.pallas_call(
        functools.partial(_merge_kernel, npt),
        grid=(n // ROW_TILE,),
        in_specs=[pspec(d), sspec(d), pspec(aw), sspec(aw), pspec(rw), sspec(rw), row(d), row(d),
                  _const_spec(w_att.shape), _const_spec(w_rnn.shape), _const_spec(w_out.shape),
                  _const_spec((1, d)), _const_spec(w_r.shape), _const_spec((1, LANES))],
        out_specs=[row(d), row(d), pl.BlockSpec((ROUTE_ROWS, ROW_TILE), lambda i: (0, i)), row(LANES),
                   pl.BlockSpec((1, LANES), lambda i: (0, 0))],
        out_shape=[jax.ShapeDtypeStruct((n, d), F32), jax.ShapeDtypeStruct((n, d), F32),
                   jax.ShapeDtypeStruct((ROUTE_ROWS, n), F32), jax.ShapeDtypeStruct((n, LANES), F32),
                   jax.ShapeDtypeStruct((1, LANES), F32)],
        scratch_shapes=[pltpu.VMEM((1, LANES), F32)],
        compiler_params=_params(1),
        name="merge",
    )(xp, xs, att_p, att_s, rnn_p, rnn_s, ga, gr, w_att, w_rnn, w_out, g2, w_r, b_r)


def _row_copy(src_ref, src_row, dst_ref, dst_row, sem):
    return pltpu.make_async_copy(src_ref.at[pl.ds(src_row, 1)], dst_ref.at[pl.ds(dst_row, 1)], sem)


def _dispatch_kernel(pad_end_ref, padded_ref, pos0_ref, pos1_ref, hn_ref, xs_ref, zero_ref, sem):
    tm = hn_ref.shape[0]

    @pl.when(pl.program_id(0) == 0)
    def _():
        zero_ref[...] = jnp.zeros_like(zero_ref)

        def zero_block(start):
            start = pl.multiple_of(start, EXPERT_BLOCK)
            return pltpu.make_async_copy(zero_ref, xs_ref.at[pl.ds(start, EXPERT_BLOCK)], sem)

        used_rows = pad_end_ref[N_EXPERTS - 1]
        total_rows = xs_ref.shape[0]
        for wait in (False, True):
            for e in range(N_EXPERTS):
                @pl.when(padded_ref[e] > 0)
                def _():
                    cp = zero_block(pad_end_ref[e] - EXPERT_BLOCK)
                    cp.wait() if wait else cp.start()

                @pl.when(used_rows + e * EXPERT_BLOCK < total_rows)
                def _():
                    cp = zero_block(used_rows + e * EXPERT_BLOCK)
                    cp.wait() if wait else cp.start()

    def issue(g, c):
        for u in range(DMA_UNROLL):
            t = g * DMA_UNROLL + u
            _row_copy(hn_ref, t, xs_ref, pos0_ref[t], sem).start()
            _row_copy(hn_ref, t, xs_ref, pos1_ref[t], sem).start()
        return c

    lax.fori_loop(0, tm // DMA_UNROLL, issue, 0)
    for _ in range(TOP_K):
        pltpu.make_async_copy(hn_ref, xs_ref.at[pl.ds(0, tm)], sem).wait()


def _dispatch(pad_end, padded, pos0, pos1, hn, rows):
    n, d = hn.shape
    grid_spec = pltpu.PrefetchScalarGridSpec(
        num_scalar_prefetch=2,
        grid=(n // ROW_TILE,),
        in_specs=[pl.BlockSpec((ROW_TILE,), lambda i, pe, pd: (i,), memory_space=pltpu.SMEM),
                  pl.BlockSpec((ROW_TILE,), lambda i, pe, pd: (i,), memory_space=pltpu.SMEM),
                  pl.BlockSpec((ROW_TILE, d), lambda i, pe, pd: (i, 0))],
        out_specs=pl.BlockSpec(memory_space=pl.ANY),
        scratch_shapes=[pltpu.VMEM((EXPERT_BLOCK, d), hn.dtype), pltpu.SemaphoreType.DMA(())],
    )
    return pl.pallas_call(
        _dispatch_kernel,
        grid_spec=grid_spec,
        out_shape=jax.ShapeDtypeStruct((rows, d), hn.dtype),
        compiler_params=_params(1),
        name="dispatch",
    )(pad_end, padded, pos0, pos1, hn)


def _experts_kernel(be_ref, nu_ref, x_ref, wg_ref, wu_ref, wd_ref, y_ref, wg_bf, wu_bf, wd_bf):
    i = pl.program_id(0)
    used = i < nu_ref[0]

    @pl.when(used & ((i == 0) | (be_ref[i] != be_ref[jnp.maximum(i - 1, 0)])))
    def _():
        wg_bf[...] = wg_ref[0].astype(BF16)
        wu_bf[...] = wu_ref[0].astype(BF16)
        wd_bf[...] = wd_ref[0].astype(BF16)

    @pl.when(used)
    def _():
        x = x_ref[...].astype(BF16)
        g = jnp.dot(x, wg_bf[...], preferred_element_type=F32)
        u = jnp.dot(x, wu_bf[...], preferred_element_type=F32)
        hid = (jax.nn.silu(g) * u).astype(BF16)
        y_ref[...] = jnp.dot(hid, wd_bf[...], preferred_element_type=F32)

    @pl.when(jnp.logical_not(used))
    def _():
        y_ref[...] = jnp.zeros_like(y_ref)


def _experts(block_expert, n_used, xs, wg, wu, wd):
    rows, d = xs.shape
    de = wg.shape[2]
    nb = rows // EXPERT_BLOCK
    last = lambda i, nu: jnp.minimum(i, nu[0] - 1)
    grid_spec = pltpu.PrefetchScalarGridSpec(
        num_scalar_prefetch=2,
        grid=(nb,),
        in_specs=[pl.BlockSpec((EXPERT_BLOCK, d), lambda i, be, nu: (last(i, nu), 0)),
                  pl.BlockSpec((1, d, de), lambda i, be, nu: (be[last(i, nu)], 0, 0)),
                  pl.BlockSpec((1, d, de), lambda i, be, nu: (be[last(i, nu)], 0, 0)),
                  pl.BlockSpec((1, de, d), lambda i, be, nu: (be[last(i, nu)], 0, 0))],
        out_specs=pl.BlockSpec((EXPERT_BLOCK, d), lambda i, be, nu: (i, 0)),
        scratch_shapes=[pltpu.VMEM((d, de), BF16), pltpu.VMEM((d, de), BF16), pltpu.VMEM((de, d), BF16)],
    )
    return pl.pallas_call(
        _experts_kernel,
        grid_spec=grid_spec,
        out_shape=jax.ShapeDtypeStruct((rows, d), F32),
        compiler_params=_params(1),
        name="experts",
    )(block_expert, n_used, xs, wg, wu, wd)


def _combine_kernel(pos0_ref, pos1_ref, h_ref, wt_ref, gf_ref, y_hbm_ref, out_ref, buf_ref, sem):
    tm = h_ref.shape[0]

    def issue(g, c):
        for u in range(DMA_UNROLL):
            t = g * DMA_UNROLL + u
            _row_copy(y_hbm_ref, pos0_ref[t], buf_ref.at[0], t, sem).start()
            _row_copy(y_hbm_ref, pos1_ref[t], buf_ref.at[1], t, sem).start()
        return c

    lax.fori_loop(0, tm // DMA_UNROLL, issue, 0)
    for k in range(TOP_K):
        pltpu.make_async_copy(y_hbm_ref.at[pl.ds(0, tm)], buf_ref.at[k], sem).wait()
    wt = wt_ref[...]
    moe = buf_ref[0] * wt[:, 0:1] + buf_ref[1] * wt[:, 1:2]
    out_ref[...] = _rmsnorm(h_ref[...] + moe, gf_ref[...])


def _combine(pos0, pos1, h, wt, gf, y_sorted, row0, n_rows):
    d = h.shape[1]
    assert row0 % ROW_TILE == 0 and n_rows % ROW_TILE == 0
    blk0 = row0 // ROW_TILE
    return pl.pallas_call(
        _combine_kernel,
        grid=(n_rows // ROW_TILE,),
        in_specs=[pl.BlockSpec((ROW_TILE,), lambda i: (blk0 + i,), memory_space=pltpu.SMEM),
                  pl.BlockSpec((ROW_TILE,), lambda i: (blk0 + i,), memory_space=pltpu.SMEM),
                  pl.BlockSpec((ROW_TILE, d), lambda i: (blk0 + i, 0)),
                  pl.BlockSpec((ROW_TILE, LANES), lambda i: (blk0 + i, 0)),
                  _const_spec((1, d)),
                  pl.BlockSpec(memory_space=pl.ANY)],
        out_specs=pl.BlockSpec((ROW_TILE, d), lambda i: (i, 0)),
        out_shape=jax.ShapeDtypeStruct((n_rows, d), F32),
        scratch_shapes=[pltpu.VMEM((TOP_K, ROW_TILE, d), F32), pltpu.SemaphoreType.DMA(())],
        compiler_params=_params(1),
        name="combine",
    )(pos0, pos1, h, wt, gf, y_sorted)


def _prompt_bias_base(rel_bias):
    m = jnp.arange(BIAS_BASE)
    dist = (Q_BLOCK - 1) - m + (KEY_BLOCKS - 1) * Q_BLOCK
    idx = jnp.clip(dist, -MAX_REL, MAX_REL) + MAX_REL
    return rel_bias[:, idx].astype(F32)


def _sample_bias(rel_bias, cache_len, t):
    qi = jnp.arange(t)[:, None]
    kj = jnp.arange(cache_len + t)[None, :]
    idx = jnp.clip(qi + cache_len - kj, -MAX_REL, MAX_REL) + MAX_REL
    bias = rel_bias[:, idx].astype(F32)
    bias = jnp.where((kj >= cache_len - BAND_PAST)[None], bias, NEG_INF)
    return bias.reshape(N_HEADS * t, cache_len + t)


def _block_diag(w):
    nb, bw, _ = w.shape
    eye = jnp.eye(nb, dtype=w.dtype)
    return jnp.einsum("ncd,nm->ncmd", w, eye).reshape(nb * bw, nb * bw)


def kernel(x_prompt, x_sample, cache_k, cache_v, state_conv, state_h, norm1_g, w_in, b_in, rel_bias, w_att_branch, conv_w, conv_b, w_rec_gate, b_rec_gate, w_in_gate, b_in_gate, lru_lambda, w_rnn_branch, w_out, norm2_g, w_router_group, b_router_group, w_router_expert, b_router_expert, w_e_gate, w_e_up, w_e_down, final_norm_g):
    batch, seq, d = x_prompt.shape
    dec_b, dec_t, _ = x_sample.shape
    depth = w_in.shape[0]
    assert depth == 1
    att_w = N_HEADS * HEAD_DIM
    rnn_w = conv_w.shape[2]
    cache_len = cache_k.shape[2]
    n_p = batch * seq
    n_s = dec_b * dec_t
    n = n_p + n_s
    keep = min(BAND_PAST + CHUNK, seq)
    row2 = lambda a: a.reshape(1, -1).astype(F32)

    tiles_per_seq = seq // ROW_TILE
    tail_tiles = -(-keep // ROW_TILE)
    state_tiles = tuple(b * tiles_per_seq + tiles_per_seq - tail_tiles + j
                        for b in range(batch) for j in range(tail_tiles)) + (n_p // ROW_TILE,)

    xp = x_prompt.reshape(n_p, d)
    xs = x_sample.reshape(n_s, d)
    q, k, v, kvt, rx, rgate, ga, gr = _proj(xp, xs, row2(norm1_g[0]), w_in[0].astype(BF16), row2(b_in[0]),
                                            att_w, rnn_w, state_tiles)

    att_p = _attn_prompt(q, k, v, _prompt_bias_base(rel_bias[0]), batch, seq)
    att_s, new_k_s, new_v_s = _attn_sample(
        q, k, v, kvt, cache_k[0].reshape(dec_b, cache_len, att_w), cache_v[0].reshape(dec_b, cache_len, att_w),
        _sample_bias(rel_bias[0], cache_len, dec_t), n_p, batch * tail_tiles * ROW_TILE, dec_b, dec_t)

    rnn_args = (conv_w[0].astype(F32), row2(conv_b[0]),
                _block_diag(w_rec_gate[0]).astype(BF16), row2(b_rec_gate[0]),
                _block_diag(w_in_gate[0]).astype(BF16), row2(b_in_gate[0]), row2(lru_lambda[0]))
    pad_state = lambda s: jnp.pad(s.astype(F32), ((0, 0), (SUBLANES - (CONV_W - 1), 0), (0, 0)))
    rnn_p, conv_p, h_p = _rnn(rx, rgate, jnp.zeros((batch, SUBLANES, rnn_w), F32), jnp.zeros((batch, 1, rnn_w), F32),
                              *rnn_args, 0, batch, seq, ROW_TILE)
    rnn_s, conv_s, h_s = _rnn(rx, rgate, pad_state(state_conv[0]), state_h[0].astype(F32)[:, None, :],
                              *rnn_args, n_p, dec_b, dec_t, dec_t)

    w_r = jnp.zeros((d, LANES), F32).at[:, :N_GROUPS].set(w_router_group[0])
    w_r = w_r.at[:, N_GROUPS:N_GROUPS + N_EXPERTS].set(w_router_expert[0]).astype(BF16)
    b_r = jnp.zeros((1, LANES), F32).at[0, :N_GROUPS].set(b_router_group[0])
    b_r = b_r.at[0, N_GROUPS:N_GROUPS + N_EXPERTS].set(b_router_expert[0])
    h, hn, route, wt, cnt = _merge(xp, xs, att_p, att_s, rnn_p, rnn_s, ga, gr,
                                   w_att_branch[0].astype(BF16), w_rnn_branch[0].astype(BF16), w_out[0].astype(BF16),
                                   row2(norm2_g[0]), w_r, b_r)

    counts = cnt[0, :N_EXPERTS].astype(jnp.int32)
    padded = (counts + EXPERT_BLOCK - 1) // EXPERT_BLOCK * EXPERT_BLOCK
    pad_end = jnp.cumsum(padded)
    pad_start = pad_end - padded
    expert = route[0:TOP_K].astype(jnp.int32)
    rank = route[TOP_K:2 * TOP_K].astype(jnp.int32)
    ids = jnp.arange(N_EXPERTS, dtype=jnp.int32)[:, None, None]
    pos = jnp.sum(jnp.where(expert[None] == ids, pad_start[:, None, None], 0), axis=0) + rank
    n_blocks = -(-(n * TOP_K) // EXPERT_BLOCK) + N_EXPERTS
    rows = n_blocks * EXPERT_BLOCK
    block_start = jnp.arange(n_blocks, dtype=jnp.int32) * EXPERT_BLOCK
    block_expert = jnp.minimum(jnp.sum((block_start[:, None] >= pad_end[None, :]).astype(jnp.int32), axis=1),
                               N_EXPERTS - 1)
    n_used = (pad_end[-1:] // EXPERT_BLOCK).astype(jnp.int32)

    x_sorted = _dispatch(pad_end, padded, pos[0], pos[1], hn, rows)
    y_sorted = _experts(block_expert, n_used, x_sorted, w_e_gate[0], w_e_up[0], w_e_down[0])
    gf = row2(final_norm_g)
    y_p = _combine(pos[0], pos[1], h, wt, gf, y_sorted, 0, n_p)
    y_s = _combine(pos[0], pos[1], h, wt, gf, y_sorted, n_p, n_s)

    def kv_state(col0):
        span = tail_tiles * ROW_TILE
        tails = [kvt[b * span + span - keep:(b + 1) * span, col0:col0 + att_w] for b in range(batch)]
        return jnp.stack(tails).reshape(1, batch, keep, N_HEADS, HEAD_DIM)

    heads = lambda a: a.reshape(1, dec_b, cache_len, N_HEADS, HEAD_DIM)
    return (y_p.reshape(batch, seq, d), y_s.reshape(dec_b, dec_t, d),
            kv_state(0), kv_state(att_w),
            conv_p[:, SUBLANES - (CONV_W - 1):][None], h_p[:, 0][None],
            heads(new_k_s), heads(new_v_s),
            conv_s[:, SUBLANES - (CONV_W - 1):][None], h_s[:, 0][None])
```

```python
import functools

import jax
import jax.numpy as jnp
from jax import lax
from jax.experimental import pallas as pl
from jax.experimental.pallas import tpu as pltpu

CHUNK = 64
LEFT_CHUNKS = 8
BAND_PAST = LEFT_CHUNKS * CHUNK
N_HEADS = 8
HEAD_DIM = 64
MAX_REL = 256
NEG_INF = -1e30
RNN_BLOCKS = 8
CONV_W = 4
RG_C = 8.0
N_GROUPS = 4
EXPERTS_PER_GROUP = 8
N_EXPERTS = N_GROUPS * EXPERTS_PER_GROUP
TOP_K = 2
EPS = 1e-6

LANES = 128
SUBLANES = 8
ROW_TILE = 512
Q_BLOCK = 256
KEY_BLOCKS = 3
KEY_WINDOW = KEY_BLOCKS * Q_BLOCK
BIAS_BASE = 1024
EXPERT_BLOCK = 512
DMA_UNROLL = 8
VMEM_LIMIT = 56 * 1024 * 1024

BF16 = jnp.bfloat16
F32 = jnp.float32


def _params(n_axes, vmem=VMEM_LIMIT):
    return pltpu.CompilerParams(dimension_semantics=("arbitrary",) * n_axes, vmem_limit_bytes=vmem)


def _const_spec(shape):
    zeros = (0,) * len(shape)
    return pl.BlockSpec(shape, lambda *_: zeros, pipeline_mode=pl.Buffered(1))


def _rmsnorm(x, g):
    return x * lax.rsqrt(jnp.mean(x * x, axis=-1, keepdims=True) + EPS) * g


def _sigmoid(x):
    return 0.5 * jnp.tanh(0.5 * x) + 0.5


def _proj_kernel(n_prompt_tiles, state_tiles, att_w, rnn_w, d_model,
                 xp_ref, xs_ref, g_ref, w_ref, b_ref,
                 q_ref, k_ref, v_ref, kvt_ref, rx_ref, rg_ref, ga_ref, gr_ref):
    i = pl.program_id(0)
    x = jnp.where(i < n_prompt_tiles, xp_ref[...], xs_ref[...])
    xn = _rmsnorm(x, g_ref[...]).astype(BF16)

    def seg(lo, width):
        return jnp.dot(xn, w_ref[:, lo:lo + width], preferred_element_type=F32) + b_ref[:, lo:lo + width]

    q = seg(0, att_w)
    k = seg(att_w, att_w)
    v = seg(2 * att_w, att_w)
    q_ref[...] = (q * (HEAD_DIM ** -0.5)).astype(BF16)
    k_ref[...] = k.astype(BF16)
    v_ref[...] = v.astype(BF16)

    @pl.when(functools.reduce(jnp.logical_or, [i == t for t in state_tiles]))
    def _():
        kvt_ref[:, :att_w] = k
        kvt_ref[:, att_w:] = v

    o = 3 * att_w
    rx = seg(o, rnn_w)
    for c in range(rnn_w // LANES):
        rx_ref[c] = rx[:, c * LANES:(c + 1) * LANES]
    rg_ref[...] = seg(o + rnn_w, rnn_w).astype(BF16)
    ga_ref[...] = seg(o + 2 * rnn_w, d_model).astype(BF16)
    gr_ref[...] = seg(o + 2 * rnn_w + d_model, d_model).astype(BF16)


def _proj(xp, xs, g1, w_in, b_in, att_w, rnn_w, state_tiles):
    n_p, d = xp.shape
    n_s = xs.shape[0]
    assert n_p % ROW_TILE == 0 and n_s == ROW_TILE
    n = n_p + n_s
    npt = n_p // ROW_TILE
    pw = w_in.shape[1]
    row = lambda w: pl.BlockSpec((ROW_TILE, w), lambda i: (i, 0))
    outs = [(att_w, BF16), (att_w, BF16), (att_w, BF16), (2 * att_w, F32), (rnn_w, F32), (rnn_w, BF16),
            (d, BF16), (d, BF16)]
    slot = lambda i: sum(jnp.where(i > t, 1, 0) for t in state_tiles)
    out_specs = [row(w) for w, _ in outs]
    out_specs[3] = pl.BlockSpec((ROW_TILE, 2 * att_w), lambda i: (slot(i), 0))
    out_specs[4] = pl.BlockSpec((rnn_w // LANES, ROW_TILE, LANES), lambda i: (0, i, 0))
    out_shape = [jax.ShapeDtypeStruct((n, w), dt) for w, dt in outs]
    out_shape[3] = jax.ShapeDtypeStruct((len(state_tiles) * ROW_TILE, 2 * att_w), F32)
    out_shape[4] = jax.ShapeDtypeStruct((rnn_w // LANES, n, LANES), F32)
    return pl.pallas_call(
        functools.partial(_proj_kernel, npt, state_tiles, att_w, rnn_w, d),
        grid=(n // ROW_TILE,),
        in_specs=[pl.BlockSpec((ROW_TILE, d), lambda i: (jnp.minimum(i, npt - 1), 0)),
                  pl.BlockSpec((ROW_TILE, d), lambda i: (0, 0)),
                  _const_spec((1, d)), _const_spec((d, pw)), _const_spec((1, pw))],
        out_specs=out_specs,
        out_shape=out_shape,
        compiler_params=_params(1),
        name="proj",
    )(xp, xs, g1, w_in, b_in)


def _attn_p_kernel(q_ref, k0_ref, k1_ref, k2_ref, v0_ref, v1_ref, v2_ref, base_ref, o_ref, bias_ref):
    b = pl.program_id(0)
    i = pl.program_id(1)

    @pl.when((b == 0) & (i == 0))
    def _():
        r = lax.broadcasted_iota(jnp.int32, (Q_BLOCK, KEY_WINDOW), 0) // CHUNK
        c = lax.broadcasted_iota(jnp.int32, (Q_BLOCK, KEY_WINDOW), 1) // CHUNK
        in_band = (c >= r) & (c <= r + LEFT_CHUNKS)
        for h in range(N_HEADS):
            rows = jnp.broadcast_to(base_ref[h:h + 1, :], (Q_BLOCK, BIAS_BASE))
            toeplitz = pltpu.roll(rows, BIAS_BASE - (Q_BLOCK - 1), axis=1, stride=1, stride_axis=0)
            bias_ref[h] = jnp.where(in_band, toeplitz[:, :KEY_WINDOW], NEG_INF)

    kwin = jnp.concatenate([k0_ref[...], k1_ref[...], k2_ref[...]], axis=0)
    vwin = jnp.concatenate([v0_ref[...], v1_ref[...], v2_ref[...]], axis=0)
    lane = lax.broadcasted_iota(jnp.int32, (1, LANES), 1)

    def heads(at_sequence_start):
        if at_sequence_start:
            col = lax.broadcasted_iota(jnp.int32, (1, KEY_WINDOW), 1)
            start_mask = jnp.where(col // Q_BLOCK + i - (KEY_BLOCKS - 1) >= 0, 0.0, NEG_INF).astype(F32)
        for pair in range(N_HEADS // 2):
            sl = slice(pair * LANES, (pair + 1) * LANES)
            q2 = q_ref[:, sl]
            k2 = kwin[:, sl]
            v2 = vwin[:, sl]
            acc = jnp.zeros((Q_BLOCK, LANES), F32)
            for half in range(2):
                hmask = (lane // HEAD_DIM) == half
                qm = jnp.where(hmask, q2, jnp.zeros_like(q2))
                vm = jnp.where(hmask, v2, jnp.zeros_like(v2))
                s = lax.dot_general(qm, k2, (((1,), (1,)), ((), ())), preferred_element_type=F32)
                s = s + bias_ref[2 * pair + half]
                if at_sequence_start:
                    s = s + start_mask
                m = jnp.max(s, axis=-1, keepdims=True)
                e = jnp.exp(s - m)
                l = jnp.sum(e, axis=-1, keepdims=True)
                o = jnp.dot(e.astype(BF16), vm, preferred_element_type=F32)
                acc = acc + o / l
            o_ref[:, sl] = acc.astype(BF16)

    pl.when(i < KEY_BLOCKS - 1)(lambda: heads(True))
    pl.when(i >= KEY_BLOCKS - 1)(lambda: heads(False))


def _attn_prompt(q, k, v, base, batch, seq):
    w = q.shape[1]
    nqb = seq // Q_BLOCK
    kspec = lambda back: pl.BlockSpec((Q_BLOCK, w), lambda b, i: (b * nqb + jnp.maximum(i - back, 0), 0))
    return pl.pallas_call(
        _attn_p_kernel,
        grid=(batch, nqb),
        in_specs=[pl.BlockSpec((Q_BLOCK, w), lambda b, i: (b * nqb + i, 0)),
                  kspec(2), kspec(1), kspec(0), kspec(2), kspec(1), kspec(0),
                  _const_spec(base.shape)],
        out_specs=pl.BlockSpec((Q_BLOCK, w), lambda b, i: (b * nqb + i, 0)),
        out_shape=jax.ShapeDtypeStruct((batch * seq, w), BF16),
        scratch_shapes=[pltpu.VMEM((N_HEADS, Q_BLOCK, KEY_WINDOW), F32)],
        compiler_params=_params(2),
        name="attn_prompt",
    )(q, k, k, k, v, v, v, base)


def _attn_s_kernel(t, q_ref, k_ref, v_ref, kvt_ref, ck_ref, cv_ref, bias_ref, o_ref, nk_ref, nv_ref):
    cache_len = ck_ref.shape[1]
    w = q_ref.shape[1]
    ck = ck_ref[0]
    cv = cv_ref[0]
    kall = jnp.concatenate([ck.astype(BF16), k_ref[...]], axis=0)
    vall = jnp.concatenate([cv.astype(BF16), v_ref[...]], axis=0)
    q = q_ref[...]
    head_of_lane = lax.broadcasted_iota(jnp.int32, (1, w), 1) // HEAD_DIM
    qs = jnp.concatenate([jnp.where(head_of_lane == h, q, jnp.zeros_like(q)) for h in range(N_HEADS)], axis=0)
    s = lax.dot_general(qs, kall, (((1,), (1,)), ((), ())), preferred_element_type=F32) + bias_ref[...]
    m = jnp.max(s, axis=-1, keepdims=True)
    e = jnp.exp(s - m)
    l = jnp.sum(e, axis=-1, keepdims=True)
    o_all = jnp.dot(e.astype(BF16), vall, preferred_element_type=F32) / l
    o = jnp.zeros((t, w), F32)
    for h in range(N_HEADS):
        o = o + jnp.where(head_of_lane == h, o_all[h * t:(h + 1) * t], 0.0)
    o_ref[...] = o.astype(BF16)
    nk_ref[0, :cache_len - t, :] = ck[t:]
    nk_ref[0, cache_len - t:, :] = kvt_ref[:, :w]
    nv_ref[0, :cache_len - t, :] = cv[t:]
    nv_ref[0, cache_len - t:, :] = kvt_ref[:, w:]


def _attn_sample(q, k, v, kvt, cache_k, cache_v, bias, row0, kvt_row0, dec_b, t):
    w = q.shape[1]
    cache_len = cache_k.shape[1]
    assert row0 % t == 0 and kvt_row0 % t == 0
    tok = lambda width, r0: pl.BlockSpec((t, width), lambda b: (r0 // t + b, 0))
    cache = pl.BlockSpec((1, cache_len, w), lambda b: (b, 0, 0))
    return pl.pallas_call(
        functools.partial(_attn_s_kernel, t),
        grid=(dec_b,),
        in_specs=[tok(w, row0), tok(w, row0), tok(w, row0), tok(2 * w, kvt_row0), cache, cache,
                  _const_spec(bias.shape)],
        out_specs=[pl.BlockSpec((t, w), lambda b: (b, 0)), cache, cache],
        out_shape=[jax.ShapeDtypeStruct((dec_b * t, w), BF16),
                   jax.ShapeDtypeStruct(cache_k.shape, F32), jax.ShapeDtypeStruct(cache_v.shape, F32)],
        compiler_params=_params(1),
        name="attn_sample",
    )(q, k, v, kvt, cache_k, cache_v, bias)


def _rnn_kernel(tm, rx_ref, rg_ref, conv0_ref, h0_ref, cw_ref, cb_ref, wa_ref, ba_ref, wx_ref, bx_ref, lam_ref,
                rnn_ref, conv_out_ref, h_out_ref, ext_ref, a_ref, b_ref, hs_ref, h_ref):
    j = pl.program_id(1)

    @pl.when(j == 0)
    def _():
        ext_ref[:SUBLANES, :] = conv0_ref[0]
        h_ref[...] = h0_ref[0]

    ext_ref[SUBLANES:, :] = jnp.concatenate([rx_ref[c] for c in range(rx_ref.shape[0])], axis=1)
    cw = cw_ref[...]
    xc = cb_ref[...]
    for tap in range(CONV_W):
        off = SUBLANES - (CONV_W - 1) + tap
        xc = xc + ext_ref[off:off + tm, :] * cw[tap:tap + 1, :]
    tail = ext_ref[tm:tm + SUBLANES, :]
    ext_ref[:SUBLANES, :] = tail
    conv_out_ref[0] = tail

    a, b = _rglru_coeffs(xc, wa_ref, ba_ref, wx_ref, bx_ref, lam_ref)

    sub = lax.broadcasted_iota(jnp.int32, a.shape, 0) % SUBLANES
    shift = 1
    while shift < SUBLANES:
        keep = sub >= shift
        a_prev = pltpu.roll(a, shift, axis=0)
        b_prev = pltpu.roll(b, shift, axis=0)
        b = jnp.where(keep, a * b_prev + b, b)
        a = jnp.where(keep, a * a_prev, a)
        shift *= 2
    a_ref[...] = a
    b_ref[...] = b

    def group(g, h):
        r0 = pl.multiple_of(g * SUBLANES, SUBLANES)
        hg = b_ref[pl.ds(r0, SUBLANES), :] + a_ref[pl.ds(r0, SUBLANES), :] * h
        hs_ref[pl.ds(r0, SUBLANES), :] = hg
        return hg[SUBLANES - 1:SUBLANES, :]

    h_last = lax.fori_loop(0, tm // SUBLANES, group, h_ref[...])
    h_ref[...] = h_last
    h_out_ref[0] = h_last
    rnn_ref[...] = (hs_ref[...] * jax.nn.gelu(rg_ref[...].astype(F32))).astype(BF16)


def _rnn(rx, rgate, conv0, h0, cw, cb, wa, ba, wx, bx, lam, row0, batch, seq, tm):
    w = rgate.shape[1]
    assert row0 % tm == 0 and seq % tm == 0 and tm % SUBLANES == 0
    blk0 = row0 // tm
    nt = seq // tm
    tok = pl.BlockSpec((tm, w), lambda b, j: (blk0 + b * nt + j, 0))
    tok_cols = pl.BlockSpec((w // LANES, tm, LANES), lambda b, j: (0, blk0 + b * nt + j, 0))
    state = lambda rows: pl.BlockSpec((1, rows, w), lambda b, j: (b, 0, 0))
    return pl.pallas_call(
        functools.partial(_rnn_kernel, tm),
        grid=(batch, nt),
        in_specs=[tok_cols, tok, state(SUBLANES), state(1),
                  _const_spec(cw.shape), _const_spec((1, w)), _const_spec((w, w)), _const_spec((1, w)),
                  _const_spec((w, w)), _const_spec((1, w)), _const_spec((1, w))],
        out_specs=[pl.BlockSpec((tm, w), lambda b, j: (b * nt + j, 0)), state(SUBLANES), state(1)],
        out_shape=[jax.ShapeDtypeStruct((batch * seq, w), BF16),
                   jax.ShapeDtypeStruct((batch, SUBLANES, w), F32),
                   jax.ShapeDtypeStruct((batch, 1, w), F32)],
        scratch_shapes=[pltpu.VMEM((tm + SUBLANES, w), F32), pltpu.VMEM((tm, w), F32), pltpu.VMEM((tm, w), F32),
                        pltpu.VMEM((tm, w), F32), pltpu.VMEM((1, w), F32)],
        compiler_params=_params(2),
        name="rnn",
    )(rx, rgate, conv0, h0, cw, cb, wa, ba, wx, bx, lam)


def _rglru_coeffs(xc, wa_ref, ba_ref, wx_ref, bx_ref, lam_ref):
    xb = xc.astype(BF16)
    r = _sigmoid(jnp.dot(xb, wa_ref[...], preferred_element_type=F32) + ba_ref[...])
    gi = _sigmoid(jnp.dot(xb, wx_ref[...], preferred_element_type=F32) + bx_ref[...])
    neg_lam = -lam_ref[...]
    softplus = jnp.maximum(neg_lam, 0.0) + jnp.log1p(jnp.exp(-jnp.abs(neg_lam)))
    log_a = (-RG_C) * r * softplus
    a = jnp.exp(log_a)
    b = jnp.sqrt(-jnp.tanh(log_a) * (a * a + 1.0)) * (gi * xc)
    return a, b


def _rnn_prompt_kernel(tm, rx_ref, rg_ref, cw_ref, cb_ref, wa_ref, ba_ref, wx_ref, bx_ref, lam_ref,
                       rnn_ref, conv_out_ref, h_out_ref, tail_ref, hl_ref, p_ref, hs_ref, h_ref):
    seg = tm // SUBLANES
    ncol = rx_ref.shape[0]
    w = ncol * LANES

    @pl.when(pl.program_id(1) == 0)
    def _():
        tail_ref[...] = jnp.zeros_like(tail_ref)
        h_ref[...] = jnp.zeros_like(h_ref)

    def strided_rows(ref, j):
        return jnp.concatenate([ref[c, pl.ds(j, SUBLANES, stride=seg), :] for c in range(ncol)], axis=1)

    slab = [strided_rows(rx_ref, j) for j in range(seg)]
    sub = lax.broadcasted_iota(jnp.int32, (SUBLANES, w), 0)

    def before(k):
        carry = jnp.broadcast_to(tail_ref[SUBLANES - k:SUBLANES - k + 1, :], (SUBLANES, w))
        return jnp.where(sub == 0, carry, pltpu.roll(slab[seg - k], 1, axis=0))

    lead = {k: before(k) for k in range(1, CONV_W)}
    cw = cw_ref[...]
    cb = cb_ref[...]
    xc = []
    for j in range(seg):
        acc = cb + slab[j] * cw[CONV_W - 1:CONV_W, :]
        for k in range(1, CONV_W):
            prev = slab[j - k] if j >= k else lead[k - j]
            acc = acc + prev * cw[CONV_W - 1 - k:CONV_W - k, :]
        xc.append(acc)
    tail = jnp.concatenate([rx_ref[c, tm - SUBLANES:tm, :] for c in range(ncol)], axis=1)
    tail_ref[...] = tail
    conv_out_ref[0] = tail

    a, b = _rglru_coeffs(jnp.concatenate(xc, axis=0), wa_ref, ba_ref, wx_ref, bx_ref, lam_ref)

    h = jnp.zeros((SUBLANES, w), F32)
    p = jnp.ones((SUBLANES, w), F32)
    for j in range(seg):
        rows = slice(j * SUBLANES, (j + 1) * SUBLANES)
        h = a[rows] * h + b[rows]
        p = a[rows] * p
        hl_ref[rows, :] = h
        p_ref[rows, :] = p
    c = h_ref[...]
    enter = []
    for s in range(SUBLANES):
        enter.append(c)
        c = h[s:s + 1, :] + p[s:s + 1, :] * c
    h_ref[...] = c
    h_out_ref[0] = c
    enter = jnp.concatenate(enter, axis=0)
    for j in range(seg):
        rows = slice(j * SUBLANES, (j + 1) * SUBLANES)
        hj = hl_ref[rows, :] + p_ref[rows, :] * enter
        for c in range(ncol):
            hs_ref[c, pl.ds(j, SUBLANES, stride=seg), :] = hj[:, c * LANES:(c + 1) * LANES]
    hseq = jnp.concatenate([hs_ref[c] for c in range(ncol)], axis=1)
    rnn_ref[...] = (hseq * jax.nn.gelu(rg_ref[...].astype(F32))).astype(BF16)


def _rnn_prompt(rx, rgate, cw, cb, wa, ba, wx, bx, lam, batch, seq, tm):
    w = rgate.shape[1]
    assert seq % tm == 0 and tm % (SUBLANES * SUBLANES) == 0 and tm // SUBLANES >= CONV_W
    nt = seq // tm
    tok = pl.BlockSpec((tm, w), lambda b, j: (b * nt + j, 0))
    tok_cols = pl.BlockSpec((w // LANES, tm, LANES), lambda b, j: (0, b * nt + j, 0))
    state = lambda rows: pl.BlockSpec((1, rows, w), lambda b, j: (b, 0, 0))
    return pl.pallas_call(
        functools.partial(_rnn_prompt_kernel, tm),
        grid=(batch, nt),
        in_specs=[tok_cols, tok,
                  _const_spec(cw.shape), _const_spec((1, w)), _const_spec((w, w)), _const_spec((1, w)),
                  _const_spec((w, w)), _const_spec((1, w)), _const_spec((1, w))],
        out_specs=[tok, state(SUBLANES), state(1)],
        out_shape=[jax.ShapeDtypeStruct((batch * seq, w), BF16),
                   jax.ShapeDtypeStruct((batch, SUBLANES, w), F32),
                   jax.ShapeDtypeStruct((batch, 1, w), F32)],
        scratch_shapes=[pltpu.VMEM((SUBLANES, w), F32), pltpu.VMEM((tm, w), F32), pltpu.VMEM((tm, w), F32),
                        pltpu.VMEM((w // LANES, tm, LANES), F32), pltpu.VMEM((1, w), F32)],
        compiler_params=_params(2),
        name="rnn_prompt",
    )(rx, rgate, cw, cb, wa, ba, wx, bx, lam)


ROUTE_ROWS = SUBLANES


def _merge_kernel(n_prompt_tiles,
                  xp_ref, xs_ref, attp_ref, atts_ref, rnnp_ref, rnns_ref, ga_ref, gr_ref,
                  watt_ref, wrnn_ref, wout_ref, g2_ref, wr_ref, br_ref,
                  h_ref, hn_ref, route_ref, wt_ref, cnt_ref, carry_ref):
    i = pl.program_id(0)
    is_p = i < n_prompt_tiles

    @pl.when(i == 0)
    def _():
        carry_ref[...] = jnp.zeros_like(carry_ref)

    att = lax.cond(is_p, lambda: attp_ref[...], lambda: atts_ref[...])
    rnn = lax.cond(is_p, lambda: rnnp_ref[...], lambda: rnns_ref[...])
    ya = jnp.dot(att, watt_ref[...], preferred_element_type=F32)
    yr = jnp.dot(rnn, wrnn_ref[...], preferred_element_type=F32)
    mixed = _sigmoid(ga_ref[...].astype(F32)) * ya + _sigmoid(gr_ref[...].astype(F32)) * yr
    mixed_out = jnp.dot(mixed.astype(BF16), wout_ref[...], preferred_element_type=F32)

    @pl.when(is_p)
    def _():
        h_ref[...] = xp_ref[...] + mixed_out

    @pl.when(jnp.logical_not(is_p))
    def _():
        h_ref[...] = xs_ref[...] + mixed_out

    h = h_ref[...]
    hn = _rmsnorm(h, g2_ref[...])
    hn_ref[...] = hn

    logits = jnp.dot(hn.astype(BF16), wr_ref[...], preferred_element_type=F32) + br_ref[...]
    tm = logits.shape[0]
    lane = lax.broadcasted_iota(jnp.int32, (tm, LANES), 1)
    low = jnp.float32(-3e38)
    big = jnp.int32(1 << 20)
    lg = jnp.where(lane < N_GROUPS, logits, low)
    mg = jnp.max(lg, axis=-1, keepdims=True)
    p_grp = 1.0 / jnp.sum(jnp.exp(lg - mg), axis=-1, keepdims=True)
    grp = jnp.min(jnp.where(lg == mg, lane, big), axis=-1, keepdims=True)
    lo = N_GROUPS + EXPERTS_PER_GROUP * grp
    le = jnp.where((lane >= lo) & (lane < lo + EXPERTS_PER_GROUP), logits, low)
    m1 = jnp.max(le, axis=-1, keepdims=True)
    i1 = jnp.min(jnp.where(le == m1, lane, big), axis=-1, keepdims=True)
    le2 = jnp.where(lane == i1, low, le)
    m2 = jnp.max(le2, axis=-1, keepdims=True)
    i2 = jnp.min(jnp.where(le2 == m2, lane, big), axis=-1, keepdims=True)
    e2 = jnp.exp(m2 - m1)
    inv = 1.0 / (1.0 + e2)
    w1 = p_grp * inv
    w2 = p_grp * (e2 * inv)
    x1 = i1 - N_GROUPS
    x2 = i2 - N_GROUPS

    onehot = jnp.where((lane == x1) | (lane == x2), 1.0, 0.0).astype(BF16)
    rowi = lax.broadcasted_iota(jnp.int32, (tm, tm), 0)
    coli = lax.broadcasted_iota(jnp.int32, (tm, tm), 1)
    tri = jnp.where(coli < rowi, 1.0, 0.0).astype(BF16)
    before = jnp.dot(tri, onehot, preferred_element_type=F32) + carry_ref[...]
    r1 = jnp.sum(jnp.where(lane == x1, before, 0.0), axis=-1, keepdims=True)
    r2 = jnp.sum(jnp.where(lane == x2, before, 0.0), axis=-1, keepdims=True)
    carry = carry_ref[...] + jnp.sum(onehot.astype(F32), axis=0, keepdims=True)
    carry_ref[...] = carry
    cnt_ref[...] = carry

    fields = (x1.astype(F32), x2.astype(F32), r1, r2, w1, w2)
    slab = jnp.zeros((tm, LANES), F32)
    for idx, val in enumerate(fields):
        slab = jnp.where(lane == idx, val, slab)
    route_ref[...] = slab.T[:ROUTE_ROWS, :]
    wt_ref[...] = jnp.where(lane == 0, w1, jnp.where(lane == 1, w2, 0.0))


def _merge(xp, xs, att_p, att_s, rnn_p, rnn_s, ga, gr, w_att, w_rnn, w_out, g2, w_r, b_r):
    n_p, d = xp.shape
    n = ga.shape[0]
    npt = n_p // ROW_TILE
    aw = att_p.shape[1]
    rw = rnn_p.shape[1]
    pspec = lambda w: pl.BlockSpec((ROW_TILE, w), lambda i: (jnp.minimum(i, npt - 1), 0))
    sspec = lambda w: pl.BlockSpec((ROW_TILE, w), lambda i: (0, 0))
    row = lambda w: pl.BlockSpec((ROW_TILE, w), lambda i: (i, 0))
    return pl.pallas_call(
        functools.partial(_merge_kernel, npt),
        grid=(n // ROW_TILE,),
        in_specs=[pspec(d), sspec(d), pspec(aw), sspec(aw), pspec(rw), sspec(rw), row(d), row(d),
                  _const_spec(w_att.shape), _const_spec(w_rnn.shape), _const_spec(w_out.shape),
                  _const_spec((1, d)), _const_spec(w_r.shape), _const_spec((1, LANES))],
        out_specs=[row(d), row(d), pl.BlockSpec((ROUTE_ROWS, ROW_TILE), lambda i: (0, i)), row(LANES),
                   pl.BlockSpec((1, LANES), lambda i: (0, 0))],
        out_shape=[jax.ShapeDtypeStruct((n, d), F32), jax.ShapeDtypeStruct((n, d), F32),
                   jax.ShapeDtypeStruct((ROUTE_ROWS, n), F32), jax.ShapeDtypeStruct((n, LANES), F32),
                   jax.ShapeDtypeStruct((1, LANES), F32)],
        scratch_shapes=[pltpu.VMEM((1, LANES), F32)],
        compiler_params=_params(1),
        name="merge",
    )(xp, xs, att_p, att_s, rnn_p, rnn_s, ga, gr, w_att, w_rnn, w_out, g2, w_r, b_r)


def _row_copy(src_ref, src_row, dst_ref, dst_row, sem):
    return pltpu.make_async_copy(src_ref.at[pl.ds(src_row, 1)], dst_ref.at[pl.ds(dst_row, 1)], sem)


def _dispatch_kernel(pad_end_ref, padded_ref, pos0_ref, pos1_ref, hn_ref, xs_ref, zero_ref, sem):
    tm = hn_ref.shape[0]

    @pl.when(pl.program_id(0) == 0)
    def _():
        zero_ref[...] = jnp.zeros_like(zero_ref)

        def zero_block(start):
            start = pl.multiple_of(start, EXPERT_BLOCK)
            return pltpu.make_async_copy(zero_ref, xs_ref.at[pl.ds(start, EXPERT_BLOCK)], sem)

        used_rows = pad_end_ref[N_EXPERTS - 1]
        total_rows = xs_ref.shape[0]
        for wait in (False, True):
            for e in range(N_EXPERTS):
                @pl.when(padded_ref[e] > 0)
                def _():
                    cp = zero_block(pad_end_ref[e] - EXPERT_BLOCK)
                    cp.wait() if wait else cp.start()

                @pl.when(used_rows + e * EXPERT_BLOCK < total_rows)
                def _():
                    cp = zero_block(used_rows + e * EXPERT_BLOCK)
                    cp.wait() if wait else cp.start()

    def issue(g, c):
        for u in range(DMA_UNROLL):
            t = g * DMA_UNROLL + u
            _row_copy(hn_ref, t, xs_ref, pos0_ref[t], sem).start()
            _row_copy(hn_ref, t, xs_ref, pos1_ref[t], sem).start()
        return c

    lax.fori_loop(0, tm // DMA_UNROLL, issue, 0)
    for _ in range(TOP_K):
        pltpu.make_async_copy(hn_ref, xs_ref.at[pl.ds(0, tm)], sem).wait()


def _dispatch(pad_end, padded, pos0, pos1, hn, rows):
    n, d = hn.shape
    grid_spec = pltpu.PrefetchScalarGridSpec(
        num_scalar_prefetch=2,
        grid=(n // ROW_TILE,),
        in_specs=[pl.BlockSpec((ROW_TILE,), lambda i, pe, pd: (i,), memory_space=pltpu.SMEM),
                  pl.BlockSpec((ROW_TILE,), lambda i, pe, pd: (i,), memory_space=pltpu.SMEM),
                  pl.BlockSpec((ROW_TILE, d), lambda i, pe, pd: (i, 0))],
        out_specs=pl.BlockSpec(memory_space=pl.ANY),
        scratch_shapes=[pltpu.VMEM((EXPERT_BLOCK, d), hn.dtype), pltpu.SemaphoreType.DMA(())],
    )
    return pl.pallas_call(
        _dispatch_kernel,
        grid_spec=grid_spec,
        out_shape=jax.ShapeDtypeStruct((rows, d), hn.dtype),
        compiler_params=_params(1),
        name="dispatch",
    )(pad_end, padded, pos0, pos1, hn)


def _experts_kernel(be_ref, nu_ref, x_ref, wg_ref, wu_ref, wd_ref, y_ref, wg_bf, wu_bf, wd_bf):
    i = pl.program_id(0)
    used = i < nu_ref[0]

    @pl.when(used & ((i == 0) | (be_ref[i] != be_ref[jnp.maximum(i - 1, 0)])))
    def _():
        wg_bf[...] = wg_ref[0].astype(BF16)
        wu_bf[...] = wu_ref[0].astype(BF16)
        wd_bf[...] = wd_ref[0].astype(BF16)

    @pl.when(used)
    def _():
        x = x_ref[...].astype(BF16)
        g = jnp.dot(x, wg_bf[...], preferred_element_type=F32)
        u = jnp.dot(x, wu_bf[...], preferred_element_type=F32)
        hid = (g * _sigmoid(g) * u).astype(BF16)
        y_ref[...] = jnp.dot(hid, wd_bf[...], preferred_element_type=F32)

    @pl.when(jnp.logical_not(used))
    def _():
        y_ref[...] = jnp.zeros_like(y_ref)


def _experts(block_expert, n_used, xs, wg, wu, wd):
    rows, d = xs.shape
    de = wg.shape[2]
    nb = rows // EXPERT_BLOCK
    last = lambda i, nu: jnp.minimum(i, nu[0] - 1)
    grid_spec = pltpu.PrefetchScalarGridSpec(
        num_scalar_prefetch=2,
        grid=(nb,),
        in_specs=[pl.BlockSpec((EXPERT_BLOCK, d), lambda i, be, nu: (last(i, nu), 0)),
                  pl.BlockSpec((1, d, de), lambda i, be, nu: (be[last(i, nu)], 0, 0)),
                  pl.BlockSpec((1, d, de), lambda i, be, nu: (be[last(i, nu)], 0, 0)),
                  pl.BlockSpec((1, de, d), lambda i, be, nu: (be[last(i, nu)], 0, 0))],
        out_specs=pl.BlockSpec((EXPERT_BLOCK, d), lambda i, be, nu: (i, 0)),
        scratch_shapes=[pltpu.VMEM((d, de), BF16), pltpu.VMEM((d, de), BF16), pltpu.VMEM((de, d), BF16)],
    )
    return pl.pallas_call(
        _experts_kernel,
        grid_spec=grid_spec,
        out_shape=jax.ShapeDtypeStruct((rows, d), F32),
        compiler_params=_params(1),
        name="experts",
    )(block_expert, n_used, xs, wg, wu, wd)


def _combine_kernel(pos0_ref, pos1_ref, h_ref, wt_ref, gf_ref, y_hbm_ref, out_ref, buf_ref, sem):
    tm = h_ref.shape[0]

    def issue(g, c):
        for u in range(DMA_UNROLL):
            t = g * DMA_UNROLL + u
            _row_copy(y_hbm_ref, pos0_ref[t], buf_ref.at[0], t, sem).start()
            _row_copy(y_hbm_ref, pos1_ref[t], buf_ref.at[1], t, sem).start()
        return c

    lax.fori_loop(0, tm // DMA_UNROLL, issue, 0)
    for k in range(TOP_K):
        pltpu.make_async_copy(y_hbm_ref.at[pl.ds(0, tm)], buf_ref.at[k], sem).wait()
    wt = wt_ref[...]
    moe = buf_ref[0] * wt[:, 0:1] + buf_ref[1] * wt[:, 1:2]
    out_ref[...] = _rmsnorm(h_ref[...] + moe, gf_ref[...])


def _combine(pos0, pos1, h, wt, gf, y_sorted, row0, n_rows):
    d = h.shape[1]
    assert row0 % ROW_TILE == 0 and n_rows % ROW_TILE == 0
    blk0 = row0 // ROW_TILE
    return pl.pallas_call(
        _combine_kernel,
        grid=(n_rows // ROW_TILE,),
        in_specs=[pl.BlockSpec((ROW_TILE,), lambda i: (blk0 + i,), memory_space=pltpu.SMEM),
                  pl.BlockSpec((ROW_TILE,), lambda i: (blk0 + i,), memory_space=pltpu.SMEM),
                  pl.BlockSpec((ROW_TILE, d), lambda i: (blk0 + i, 0)),
                  pl.BlockSpec((ROW_TILE, LANES), lambda i: (blk0 + i, 0)),
                  _const_spec((1, d)),
                  pl.BlockSpec(memory_space=pl.ANY)],
        out_specs=pl.BlockSpec((ROW_TILE, d), lambda i: (i, 0)),
        out_shape=jax.ShapeDtypeStruct((n_rows, d), F32),
        scratch_shapes=[pltpu.VMEM((TOP_K, ROW_TILE, d), F32), pltpu.SemaphoreType.DMA(())],
        compiler_params=_params(1),
        name="combine",
    )(pos0, pos1, h, wt, gf, y_sorted)


def _prompt_bias_base(rel_bias):
    m = jnp.arange(BIAS_BASE)
    dist = (Q_BLOCK - 1) - m + (KEY_BLOCKS - 1) * Q_BLOCK
    idx = jnp.clip(dist, -MAX_REL, MAX_REL) + MAX_REL
    return rel_bias[:, idx].astype(F32)


def _sample_bias(rel_bias, cache_len, t):
    qi = jnp.arange(t)[:, None]
    kj = jnp.arange(cache_len + t)[None, :]
    idx = jnp.clip(qi + cache_len - kj, -MAX_REL, MAX_REL) + MAX_REL
    bias = rel_bias[:, idx].astype(F32)
    bias = jnp.where((kj >= cache_len - BAND_PAST)[None], bias, NEG_INF)
    return bias.reshape(N_HEADS * t, cache_len + t)


def _block_diag(w):
    nb, bw, _ = w.shape
    eye = jnp.eye(nb, dtype=w.dtype)
    return jnp.einsum("ncd,nm->ncmd", w, eye).reshape(nb * bw, nb * bw)


def kernel(x_prompt, x_sample, cache_k, cache_v, state_conv, state_h, norm1_g, w_in, b_in, rel_bias, w_att_branch, conv_w, conv_b, w_rec_gate, b_rec_gate, w_in_gate, b_in_gate, lru_lambda, w_rnn_branch, w_out, norm2_g, w_router_group, b_router_group, w_router_expert, b_router_expert, w_e_gate, w_e_up, w_e_down, final_norm_g):
    batch, seq, d = x_prompt.shape
    dec_b, dec_t, _ = x_sample.shape
    depth = w_in.shape[0]
    assert depth == 1
    att_w = N_HEADS * HEAD_DIM
    rnn_w = conv_w.shape[2]
    cache_len = cache_k.shape[2]
    n_p = batch * seq
    n_s = dec_b * dec_t
    n = n_p + n_s
    keep = min(BAND_PAST + CHUNK, seq)
    row2 = lambda a: a.reshape(1, -1).astype(F32)

    tiles_per_seq = seq // ROW_TILE
    tail_tiles = -(-keep // ROW_TILE)
    state_tiles = tuple(b * tiles_per_seq + tiles_per_seq - tail_tiles + j
                        for b in range(batch) for j in range(tail_tiles)) + (n_p // ROW_TILE,)

    xp = x_prompt.reshape(n_p, d)
    xs = x_sample.reshape(n_s, d)
    q, k, v, kvt, rx, rgate, ga, gr = _proj(xp, xs, row2(norm1_g[0]), w_in[0].astype(BF16), row2(b_in[0]),
                                            att_w, rnn_w, state_tiles)

    att_p = _attn_prompt(q, k, v, _prompt_bias_base(rel_bias[0]), batch, seq)
    att_s, new_k_s, new_v_s = _attn_sample(
        q, k, v, kvt, cache_k[0].reshape(dec_b, cache_len, att_w), cache_v[0].reshape(dec_b, cache_len, att_w),
        _sample_bias(rel_bias[0], cache_len, dec_t), n_p, batch * tail_tiles * ROW_TILE, dec_b, dec_t)

    rnn_args = (conv_w[0].astype(F32), row2(conv_b[0]),
                _block_diag(w_rec_gate[0]).astype(BF16), row2(b_rec_gate[0]),
                _block_diag(w_in_gate[0]).astype(BF16), row2(b_in_gate[0]), row2(lru_lambda[0]))
    pad_state = lambda s: jnp.pad(s.astype(F32), ((0, 0), (SUBLANES - (CONV_W - 1), 0), (0, 0)))
    rnn_p, conv_p, h_p = _rnn_prompt(rx, rgate, *rnn_args, batch, seq, ROW_TILE)
    rnn_s, conv_s, h_s = _rnn(rx, rgate, pad_state(state_conv[0]), state_h[0].astype(F32)[:, None, :],
                              *rnn_args, n_p, dec_b, dec_t, dec_t)

    w_r = jnp.zeros((d, LANES), F32).at[:, :N_GROUPS].set(w_router_group[0])
    w_r = w_r.at[:, N_GROUPS:N_GROUPS + N_EXPERTS].set(w_router_expert[0]).astype(BF16)
    b_r = jnp.zeros((1, LANES), F32).at[0, :N_GROUPS].set(b_router_group[0])
    b_r = b_r.at[0, N_GROUPS:N_GROUPS + N_EXPERTS].set(b_router_expert[0])
    h, hn, route, wt, cnt = _merge(xp, xs, att_p, att_s, rnn_p, rnn_s, ga, gr,
                                   w_att_branch[0].astype(BF16), w_rnn_branch[0].astype(BF16), w_out[0].astype(BF16),
                                   row2(norm2_g[0]), w_r, b_r)

    counts = cnt[0, :N_EXPERTS].astype(jnp.int32)
    padded = (counts + EXPERT_BLOCK - 1) // EXPERT_BLOCK * EXPERT_BLOCK
    pad_end = jnp.cumsum(padded)
    pad_start = pad_end - padded
    expert = route[0:TOP_K].astype(jnp.int32)
    rank = route[TOP_K:2 * TOP_K].astype(jnp.int32)
    ids = jnp.arange(N_EXPERTS, dtype=jnp.int32)[:, None, None]
    pos = jnp.sum(jnp.where(expert[None] == ids, pad_start[:, None, None], 0), axis=0) + rank
    n_blocks = -(-(n * TOP_K) // EXPERT_BLOCK) + N_EXPERTS
    rows = n_blocks * EXPERT_BLOCK
    block_start = jnp.arange(n_blocks, dtype=jnp.int32) * EXPERT_BLOCK
    block_expert = jnp.minimum(jnp.sum((block_start[:, None] >= pad_end[None, :]).astype(jnp.int32), axis=1),
                               N_EXPERTS - 1)
    n_used = (pad_end[-1:] // EXPERT_BLOCK).astype(jnp.int32)

    x_sorted = _dispatch(pad_end, padded, pos[0], pos[1], hn, rows)
    y_sorted = _experts(block_expert, n_used, x_sorted, w_e_gate[0], w_e_up[0], w_e_down[0])
    gf = row2(final_norm_g)
    y_p = _combine(pos[0], pos[1], h, wt, gf, y_sorted, 0, n_p)
    y_s = _combine(pos[0], pos[1], h, wt, gf, y_sorted, n_p, n_s)

    def kv_state(col0):
        span = tail_tiles * ROW_TILE
        tails = [kvt[b * span + span - keep:(b + 1) * span, col0:col0 + att_w] for b in range(batch)]
        return jnp.stack(tails).reshape(1, batch, keep, N_HEADS, HEAD_DIM)

    heads = lambda a: a.reshape(1, dec_b, cache_len, N_HEADS, HEAD_DIM)
    return (y_p.reshape(batch, seq, d), y_s.reshape(dec_b, dec_t, d),
            kv_state(0), kv_state(att_w),
            conv_p[:, SUBLANES - (CONV_W - 1):][None], h_p[:, 0][None],
            heads(new_k_s), heads(new_v_s),
            conv_s[:, SUBLANES - (CONV_W - 1):][None], h_s[:, 0][None])
```

```python
import functools

import jax
import jax.numpy as jnp
from jax import lax
from jax.experimental import pallas as pl
from jax.experimental.pallas import tpu as pltpu

CHUNK = 64
LEFT_CHUNKS = 8
BAND_PAST = LEFT_CHUNKS * CHUNK
N_HEADS = 8
HEAD_DIM = 64
MAX_REL = 256
NEG_INF = -1e30
RNN_BLOCKS = 8
CONV_W = 4
RG_C = 8.0
N_GROUPS = 4
EXPERTS_PER_GROUP = 8
N_EXPERTS = N_GROUPS * EXPERTS_PER_GROUP
TOP_K = 2
EPS = 1e-6

LANES = 128
SUBLANES = 8
ROW_TILE = 512
Q_BLOCK = 256
KEY_BLOCKS = 3
KEY_WINDOW = KEY_BLOCKS * Q_BLOCK
BIAS_BASE = 1024
EXPERT_BLOCK = 512
VMEM_LIMIT = 56 * 1024 * 1024

BF16 = jnp.bfloat16
F32 = jnp.float32


def _params(n_axes, vmem=VMEM_LIMIT):
    return pltpu.CompilerParams(dimension_semantics=("arbitrary",) * n_axes, vmem_limit_bytes=vmem)


def _const_spec(shape):
    zeros = (0,) * len(shape)
    return pl.BlockSpec(shape, lambda *_: zeros, pipeline_mode=pl.Buffered(1))


def _rmsnorm(x, g):
    return x * lax.rsqrt(jnp.mean(x * x, axis=-1, keepdims=True) + EPS) * g


def _sigmoid(x):
    return 0.5 * jnp.tanh(0.5 * x) + 0.5


def _proj_kernel(n_prompt_tiles, state_tiles, att_w, rnn_w, d_model,
                 xp_ref, xs_ref, g_ref, w_ref, b_ref,
                 q_ref, k_ref, v_ref, kvt_ref, rx_ref, rg_ref, ga_ref, gr_ref):
    i = pl.program_id(0)
    x = jnp.where(i < n_prompt_tiles, xp_ref[...], xs_ref[...])
    xn = _rmsnorm(x, g_ref[...]).astype(BF16)

    def seg(lo, width):
        return jnp.dot(xn, w_ref[:, lo:lo + width], preferred_element_type=F32) + b_ref[:, lo:lo + width]

    q = seg(0, att_w)
    k = seg(att_w, att_w)
    v = seg(2 * att_w, att_w)
    q_ref[...] = (q * (HEAD_DIM ** -0.5)).astype(BF16)
    k_ref[...] = k.astype(BF16)
    v_ref[...] = v.astype(BF16)

    @pl.when(functools.reduce(jnp.logical_or, [i == t for t in state_tiles]))
    def _():
        kvt_ref[:, :att_w] = k
        kvt_ref[:, att_w:] = v

    o = 3 * att_w
    rx = seg(o, rnn_w)
    for c in range(rnn_w // LANES):
        rx_ref[c] = rx[:, c * LANES:(c + 1) * LANES]
    rg_ref[...] = seg(o + rnn_w, rnn_w).astype(BF16)
    ga_ref[...] = seg(o + 2 * rnn_w, d_model).astype(BF16)
    gr_ref[...] = seg(o + 2 * rnn_w + d_model, d_model).astype(BF16)


def _proj(xp, xs, g1, w_in, b_in, att_w, rnn_w, state_tiles):
    n_p, d = xp.shape
    n_s = xs.shape[0]
    assert n_p % ROW_TILE == 0 and n_s == ROW_TILE
    n = n_p + n_s
    npt = n_p // ROW_TILE
    pw = w_in.shape[1]
    row = lambda w: pl.BlockSpec((ROW_TILE, w), lambda i: (i, 0))
    outs = [(att_w, BF16), (att_w, BF16), (att_w, BF16), (2 * att_w, F32), (rnn_w, F32), (rnn_w, BF16),
            (d, BF16), (d, BF16)]
    slot = lambda i: sum(jnp.where(i > t, 1, 0) for t in state_tiles)
    out_specs = [row(w) for w, _ in outs]
    out_specs[3] = pl.BlockSpec((ROW_TILE, 2 * att_w), lambda i: (slot(i), 0))
    out_specs[4] = pl.BlockSpec((rnn_w // LANES, ROW_TILE, LANES), lambda i: (0, i, 0))
    out_shape = [jax.ShapeDtypeStruct((n, w), dt) for w, dt in outs]
    out_shape[3] = jax.ShapeDtypeStruct((len(state_tiles) * ROW_TILE, 2 * att_w), F32)
    out_shape[4] = jax.ShapeDtypeStruct((rnn_w // LANES, n, LANES), F32)
    return pl.pallas_call(
        functools.partial(_proj_kernel, npt, state_tiles, att_w, rnn_w, d),
        grid=(n // ROW_TILE,),
        in_specs=[pl.BlockSpec((ROW_TILE, d), lambda i: (jnp.minimum(i, npt - 1), 0)),
                  pl.BlockSpec((ROW_TILE, d), lambda i: (0, 0)),
                  _const_spec((1, d)), _const_spec((d, pw)), _const_spec((1, pw))],
        out_specs=out_specs,
        out_shape=out_shape,
        compiler_params=_params(1),
        name="proj",
    )(xp, xs, g1, w_in, b_in)


def _attn_p_kernel(q_ref, k0_ref, k1_ref, k2_ref, v0_ref, v1_ref, v2_ref, base_ref, o_ref, bias_ref):
    b = pl.program_id(0)
    i = pl.program_id(1)

    @pl.when((b == 0) & (i == 0))
    def _():
        r = lax.broadcasted_iota(jnp.int32, (Q_BLOCK, KEY_WINDOW), 0) // CHUNK
        c = lax.broadcasted_iota(jnp.int32, (Q_BLOCK, KEY_WINDOW), 1) // CHUNK
        in_band = (c >= r) & (c <= r + LEFT_CHUNKS)
        for h in range(N_HEADS):
            rows = jnp.broadcast_to(base_ref[h:h + 1, :], (Q_BLOCK, BIAS_BASE))
            toeplitz = pltpu.roll(rows, BIAS_BASE - (Q_BLOCK - 1), axis=1, stride=1, stride_axis=0)
            bias_ref[h] = jnp.where(in_band, toeplitz[:, :KEY_WINDOW], NEG_INF)

    kwin = jnp.concatenate([k0_ref[...], k1_ref[...], k2_ref[...]], axis=0)
    vwin = jnp.concatenate([v0_ref[...], v1_ref[...], v2_ref[...]], axis=0)
    lane = lax.broadcasted_iota(jnp.int32, (1, LANES), 1)

    def heads(at_sequence_start):
        if at_sequence_start:
            col = lax.broadcasted_iota(jnp.int32, (1, KEY_WINDOW), 1)
            start_mask = jnp.where(col // Q_BLOCK + i - (KEY_BLOCKS - 1) >= 0, 0.0, NEG_INF).astype(F32)
        for pair in range(N_HEADS // 2):
            sl = slice(pair * LANES, (pair + 1) * LANES)
            q2 = q_ref[:, sl]
            k2 = kwin[:, sl]
            v2 = vwin[:, sl]
            acc = jnp.zeros((Q_BLOCK, LANES), F32)
            for half in range(2):
                hmask = (lane // HEAD_DIM) == half
                qm = jnp.where(hmask, q2, jnp.zeros_like(q2))
                vm = jnp.where(hmask, v2, jnp.zeros_like(v2))
                s = lax.dot_general(qm, k2, (((1,), (1,)), ((), ())), preferred_element_type=F32)
                s = s + bias_ref[2 * pair + half]
                if at_sequence_start:
                    s = s + start_mask
                m = jnp.max(s, axis=-1, keepdims=True)
                e = jnp.exp(s - m)
                l = jnp.sum(e, axis=-1, keepdims=True)
                o = jnp.dot(e.astype(BF16), vm, preferred_element_type=F32)
                acc = acc + o / l
            o_ref[:, sl] = acc.astype(BF16)

    pl.when(i < KEY_BLOCKS - 1)(lambda: heads(True))
    pl.when(i >= KEY_BLOCKS - 1)(lambda: heads(False))


def _attn_prompt(q, k, v, base, batch, seq):
    w = q.shape[1]
    nqb = seq // Q_BLOCK
    kspec = lambda back: pl.BlockSpec((Q_BLOCK, w), lambda b, i: (b * nqb + jnp.maximum(i - back, 0), 0))
    return pl.pallas_call(
        _attn_p_kernel,
        grid=(batch, nqb),
        in_specs=[pl.BlockSpec((Q_BLOCK, w), lambda b, i: (b * nqb + i, 0)),
                  kspec(2), kspec(1), kspec(0), kspec(2), kspec(1), kspec(0),
                  _const_spec(base.shape)],
        out_specs=pl.BlockSpec((Q_BLOCK, w), lambda b, i: (b * nqb + i, 0)),
        out_shape=jax.ShapeDtypeStruct((batch * seq, w), BF16),
        scratch_shapes=[pltpu.VMEM((N_HEADS, Q_BLOCK, KEY_WINDOW), F32)],
        compiler_params=_params(2),
        name="attn_prompt",
    )(q, k, k, k, v, v, v, base)


def _attn_s_kernel(t, q_ref, k_ref, v_ref, kvt_ref, ck_ref, cv_ref, bias_ref, o_ref, nk_ref, nv_ref):
    cache_len = ck_ref.shape[1]
    w = q_ref.shape[1]
    ck = ck_ref[0]
    cv = cv_ref[0]
    kall = jnp.concatenate([ck.astype(BF16), k_ref[...]], axis=0)
    vall = jnp.concatenate([cv.astype(BF16), v_ref[...]], axis=0)
    q = q_ref[...]
    head_of_lane = lax.broadcasted_iota(jnp.int32, (1, w), 1) // HEAD_DIM
    qs = jnp.concatenate([jnp.where(head_of_lane == h, q, jnp.zeros_like(q)) for h in range(N_HEADS)], axis=0)
    s = lax.dot_general(qs, kall, (((1,), (1,)), ((), ())), preferred_element_type=F32) + bias_ref[...]
    m = jnp.max(s, axis=-1, keepdims=True)
    e = jnp.exp(s - m)
    l = jnp.sum(e, axis=-1, keepdims=True)
    o_all = jnp.dot(e.astype(BF16), vall, preferred_element_type=F32) / l
    o = jnp.zeros((t, w), F32)
    for h in range(N_HEADS):
        o = o + jnp.where(head_of_lane == h, o_all[h * t:(h + 1) * t], 0.0)
    o_ref[...] = o.astype(BF16)
    nk_ref[0, :cache_len - t, :] = ck[t:]
    nk_ref[0, cache_len - t:, :] = kvt_ref[:, :w]
    nv_ref[0, :cache_len - t, :] = cv[t:]
    nv_ref[0, cache_len - t:, :] = kvt_ref[:, w:]


def _attn_sample(q, k, v, kvt, cache_k, cache_v, bias, row0, kvt_row0, dec_b, t):
    w = q.shape[1]
    cache_len = cache_k.shape[1]
    assert row0 % t == 0 and kvt_row0 % t == 0
    tok = lambda width, r0: pl.BlockSpec((t, width), lambda b: (r0 // t + b, 0))
    cache = pl.BlockSpec((1, cache_len, w), lambda b: (b, 0, 0))
    return pl.pallas_call(
        functools.partial(_attn_s_kernel, t),
        grid=(dec_b,),
        in_specs=[tok(w, row0), tok(w, row0), tok(w, row0), tok(2 * w, kvt_row0), cache, cache,
                  _const_spec(bias.shape)],
        out_specs=[pl.BlockSpec((t, w), lambda b: (b, 0)), cache, cache],
        out_shape=[jax.ShapeDtypeStruct((dec_b * t, w), BF16),
                   jax.ShapeDtypeStruct(cache_k.shape, F32), jax.ShapeDtypeStruct(cache_v.shape, F32)],
        compiler_params=_params(1),
        name="attn_sample",
    )(q, k, v, kvt, cache_k, cache_v, bias)


def _rnn_kernel(tm, rx_ref, rg_ref, conv0_ref, h0_ref, cw_ref, cb_ref, wa_ref, ba_ref, wx_ref, bx_ref, lam_ref,
                rnn_ref, conv_out_ref, h_out_ref, ext_ref, a_ref, b_ref, hs_ref, h_ref):
    j = pl.program_id(1)

    @pl.when(j == 0)
    def _():
        ext_ref[:SUBLANES, :] = conv0_ref[0]
        h_ref[...] = h0_ref[0]

    ext_ref[SUBLANES:, :] = jnp.concatenate([rx_ref[c] for c in range(rx_ref.shape[0])], axis=1)
    cw = cw_ref[...]
    xc = cb_ref[...]
    for tap in range(CONV_W):
        off = SUBLANES - (CONV_W - 1) + tap
        xc = xc + ext_ref[off:off + tm, :] * cw[tap:tap + 1, :]
    tail = ext_ref[tm:tm + SUBLANES, :]
    ext_ref[:SUBLANES, :] = tail
    conv_out_ref[0] = tail

    a, b = _rglru_coeffs(xc, wa_ref, ba_ref, wx_ref, bx_ref, lam_ref)

    sub = lax.broadcasted_iota(jnp.int32, a.shape, 0) % SUBLANES
    shift = 1
    while shift < SUBLANES:
        keep = sub >= shift
        a_prev = pltpu.roll(a, shift, axis=0)
        b_prev = pltpu.roll(b, shift, axis=0)
        b = jnp.where(keep, a * b_prev + b, b)
        a = jnp.where(keep, a * a_prev, a)
        shift *= 2
    a_ref[...] = a
    b_ref[...] = b

    def group(g, h):
        r0 = pl.multiple_of(g * SUBLANES, SUBLANES)
        hg = b_ref[pl.ds(r0, SUBLANES), :] + a_ref[pl.ds(r0, SUBLANES), :] * h
        hs_ref[pl.ds(r0, SUBLANES), :] = hg
        return hg[SUBLANES - 1:SUBLANES, :]

    h_last = lax.fori_loop(0, tm // SUBLANES, group, h_ref[...])
    h_ref[...] = h_last
    h_out_ref[0] = h_last
    rnn_ref[...] = (hs_ref[...] * jax.nn.gelu(rg_ref[...].astype(F32))).astype(BF16)


def _rnn(rx, rgate, conv0, h0, cw, cb, wa, ba, wx, bx, lam, row0, batch, seq, tm):
    w = rgate.shape[1]
    assert row0 % tm == 0 and seq % tm == 0 and tm % SUBLANES == 0
    blk0 = row0 // tm
    nt = seq // tm
    tok = pl.BlockSpec((tm, w), lambda b, j: (blk0 + b * nt + j, 0))
    tok_cols = pl.BlockSpec((w // LANES, tm, LANES), lambda b, j: (0, blk0 + b * nt + j, 0))
    state = lambda rows: pl.BlockSpec((1, rows, w), lambda b, j: (b, 0, 0))
    return pl.pallas_call(
        functools.partial(_rnn_kernel, tm),
        grid=(batch, nt),
        in_specs=[tok_cols, tok, state(SUBLANES), state(1),
                  _const_spec(cw.shape), _const_spec((1, w)), _const_spec((w, w)), _const_spec((1, w)),
                  _const_spec((w, w)), _const_spec((1, w)), _const_spec((1, w))],
        out_specs=[pl.BlockSpec((tm, w), lambda b, j: (b * nt + j, 0)), state(SUBLANES), state(1)],
        out_shape=[jax.ShapeDtypeStruct((batch * seq, w), BF16),
                   jax.ShapeDtypeStruct((batch, SUBLANES, w), F32),
                   jax.ShapeDtypeStruct((batch, 1, w), F32)],
        scratch_shapes=[pltpu.VMEM((tm + SUBLANES, w), F32), pltpu.VMEM((tm, w), F32), pltpu.VMEM((tm, w), F32),
                        pltpu.VMEM((tm, w), F32), pltpu.VMEM((1, w), F32)],
        compiler_params=_params(2),
        name="rnn",
    )(rx, rgate, conv0, h0, cw, cb, wa, ba, wx, bx, lam)


def _rglru_coeffs(xc, wa_ref, ba_ref, wx_ref, bx_ref, lam_ref):
    xb = xc.astype(BF16)
    r = _sigmoid(jnp.dot(xb, wa_ref[...], preferred_element_type=F32) + ba_ref[...])
    gi = _sigmoid(jnp.dot(xb, wx_ref[...], preferred_element_type=F32) + bx_ref[...])
    neg_lam = -lam_ref[...]
    softplus = jnp.maximum(neg_lam, 0.0) + jnp.log1p(jnp.exp(-jnp.abs(neg_lam)))
    log_a = (-RG_C) * r * softplus
    a = jnp.exp(log_a)
    b = jnp.sqrt(-jnp.tanh(log_a) * (a * a + 1.0)) * (gi * xc)
    return a, b


def _rnn_prompt_kernel(tm, rx_ref, rg_ref, cw_ref, cb_ref, wa_ref, ba_ref, wx_ref, bx_ref, lam_ref,
                       rnn_ref, conv_out_ref, h_out_ref, tail_ref, hl_ref, p_ref, hs_ref, h_ref):
    seg = tm // SUBLANES
    ncol = rx_ref.shape[0]
    w = ncol * LANES

    @pl.when(pl.program_id(1) == 0)
    def _():
        tail_ref[...] = jnp.zeros_like(tail_ref)
        h_ref[...] = jnp.zeros_like(h_ref)

    def strided_rows(ref, j):
        return jnp.concatenate([ref[c, pl.ds(j, SUBLANES, stride=seg), :] for c in range(ncol)], axis=1)

    slab = [strided_rows(rx_ref, j) for j in range(seg)]
    sub = lax.broadcasted_iota(jnp.int32, (SUBLANES, w), 0)

    def before(k):
        carry = jnp.broadcast_to(tail_ref[SUBLANES - k:SUBLANES - k + 1, :], (SUBLANES, w))
        return jnp.where(sub == 0, carry, pltpu.roll(slab[seg - k], 1, axis=0))

    lead = {k: before(k) for k in range(1, CONV_W)}
    cw = cw_ref[...]
    cb = cb_ref[...]
    xc = []
    for j in range(seg):
        acc = cb + slab[j] * cw[CONV_W - 1:CONV_W, :]
        for k in range(1, CONV_W):
            prev = slab[j - k] if j >= k else lead[k - j]
            acc = acc + prev * cw[CONV_W - 1 - k:CONV_W - k, :]
        xc.append(acc)
    tail = jnp.concatenate([rx_ref[c, tm - SUBLANES:tm, :] for c in range(ncol)], axis=1)
    tail_ref[...] = tail
    conv_out_ref[0] = tail

    a, b = _rglru_coeffs(jnp.concatenate(xc, axis=0), wa_ref, ba_ref, wx_ref, bx_ref, lam_ref)

    h = jnp.zeros((SUBLANES, w), F32)
    p = jnp.ones((SUBLANES, w), F32)
    for j in range(seg):
        rows = slice(j * SUBLANES, (j + 1) * SUBLANES)
        h = a[rows] * h + b[rows]
        p = a[rows] * p
        hl_ref[rows, :] = h
        p_ref[rows, :] = p
    c = h_ref[...]
    enter = []
    for s in range(SUBLANES):
        enter.append(c)
        c = h[s:s + 1, :] + p[s:s + 1, :] * c
    h_ref[...] = c
    h_out_ref[0] = c
    enter = jnp.concatenate(enter, axis=0)
    for j in range(seg):
        rows = slice(j * SUBLANES, (j + 1) * SUBLANES)
        hj = hl_ref[rows, :] + p_ref[rows, :] * enter
        for c in range(ncol):
            hs_ref[c, pl.ds(j, SUBLANES, stride=seg), :] = hj[:, c * LANES:(c + 1) * LANES]
    hseq = jnp.concatenate([hs_ref[c] for c in range(ncol)], axis=1)
    rnn_ref[...] = (hseq * jax.nn.gelu(rg_ref[...].astype(F32))).astype(BF16)


def _rnn_prompt(rx, rgate, cw, cb, wa, ba, wx, bx, lam, batch, seq, tm):
    w = rgate.shape[1]
    assert seq % tm == 0 and tm % (SUBLANES * SUBLANES) == 0 and tm // SUBLANES >= CONV_W
    nt = seq // tm
    tok = pl.BlockSpec((tm, w), lambda b, j: (b * nt + j, 0))
    tok_cols = pl.BlockSpec((w // LANES, tm, LANES), lambda b, j: (0, b * nt + j, 0))
    state = lambda rows: pl.BlockSpec((1, rows, w), lambda b, j: (b, 0, 0))
    return pl.pallas_call(
        functools.partial(_rnn_prompt_kernel, tm),
        grid=(batch, nt),
        in_specs=[tok_cols, tok,
                  _const_spec(cw.shape), _const_spec((1, w)), _const_spec((w, w)), _const_spec((1, w)),
                  _const_spec((w, w)), _const_spec((1, w)), _const_spec((1, w))],
        out_specs=[tok, state(SUBLANES), state(1)],
        out_shape=[jax.ShapeDtypeStruct((batch * seq, w), BF16),
                   jax.ShapeDtypeStruct((batch, SUBLANES, w), F32),
                   jax.ShapeDtypeStruct((batch, 1, w), F32)],
        scratch_shapes=[pltpu.VMEM((SUBLANES, w), F32), pltpu.VMEM((tm, w), F32), pltpu.VMEM((tm, w), F32),
                        pltpu.VMEM((w // LANES, tm, LANES), F32), pltpu.VMEM((1, w), F32)],
        compiler_params=_params(2),
        name="rnn_prompt",
    )(rx, rgate, cw, cb, wa, ba, wx, bx, lam)


ROUTE_ROWS = SUBLANES


def _merge_kernel(n_prompt_tiles,
                  xp_ref, xs_ref, attp_ref, atts_ref, rnnp_ref, rnns_ref, ga_ref, gr_ref,
                  watt_ref, wrnn_ref, wout_ref, g2_ref, wr_ref, br_ref,
                  h_ref, hn_ref, route_ref, wt_ref, cnt_ref, carry_ref):
    i = pl.program_id(0)
    is_p = i < n_prompt_tiles

    @pl.when(i == 0)
    def _():
        carry_ref[...] = jnp.zeros_like(carry_ref)

    x = jnp.where(is_p, xp_ref[...], xs_ref[...])
    att = jnp.where(is_p, attp_ref[...], atts_ref[...])
    rnn = jnp.where(is_p, rnnp_ref[...], rnns_ref[...])
    ya = jnp.dot(att, watt_ref[...], preferred_element_type=F32)
    yr = jnp.dot(rnn, wrnn_ref[...], preferred_element_type=F32)
    mixed = _sigmoid(ga_ref[...].astype(F32)) * ya + _sigmoid(gr_ref[...].astype(F32)) * yr
    h = x + jnp.dot(mixed.astype(BF16), wout_ref[...], preferred_element_type=F32)
    h_ref[...] = h
    hn = _rmsnorm(h, g2_ref[...])
    hn_ref[...] = hn

    logits = jnp.dot(hn.astype(BF16), wr_ref[...], preferred_element_type=F32) + br_ref[...]
    tm = logits.shape[0]
    lane = lax.broadcasted_iota(jnp.int32, (tm, LANES), 1)
    low = jnp.float32(-3e38)
    big = jnp.int32(1 << 20)
    lg = jnp.where(lane < N_GROUPS, logits, low)
    mg = jnp.max(lg, axis=-1, keepdims=True)
    p_grp = 1.0 / jnp.sum(jnp.exp(lg - mg), axis=-1, keepdims=True)
    grp = jnp.min(jnp.where(lg == mg, lane, big), axis=-1, keepdims=True)
    lo = N_GROUPS + EXPERTS_PER_GROUP * grp
    le = jnp.where((lane >= lo) & (lane < lo + EXPERTS_PER_GROUP), logits, low)
    m1 = jnp.max(le, axis=-1, keepdims=True)
    i1 = jnp.min(jnp.where(le == m1, lane, big), axis=-1, keepdims=True)
    le2 = jnp.where(lane == i1, low, le)
    m2 = jnp.max(le2, axis=-1, keepdims=True)
    i2 = jnp.min(jnp.where(le2 == m2, lane, big), axis=-1, keepdims=True)
    e2 = jnp.exp(m2 - m1)
    inv = 1.0 / (1.0 + e2)
    w1 = p_grp * inv
    w2 = p_grp * (e2 * inv)
    x1 = i1 - N_GROUPS
    x2 = i2 - N_GROUPS

    onehot = jnp.where((lane == x1) | (lane == x2), 1.0, 0.0).astype(BF16)
    rowi = lax.broadcasted_iota(jnp.int32, (tm, tm), 0)
    coli = lax.broadcasted_iota(jnp.int32, (tm, tm), 1)
    tri = jnp.where(coli < rowi, 1.0, 0.0).astype(BF16)
    before = jnp.dot(tri, onehot, preferred_element_type=F32) + carry_ref[...]
    r1 = jnp.sum(jnp.where(lane == x1, before, 0.0), axis=-1, keepdims=True)
    r2 = jnp.sum(jnp.where(lane == x2, before, 0.0), axis=-1, keepdims=True)
    carry = carry_ref[...] + jnp.sum(onehot.astype(F32), axis=0, keepdims=True)
    carry_ref[...] = carry
    cnt_ref[...] = carry

    fields = (x1.astype(F32), x2.astype(F32), r1, r2, w1, w2)
    slab = jnp.zeros((tm, LANES), F32)
    for idx, val in enumerate(fields):
        slab = jnp.where(lane == idx, val, slab)
    route_ref[...] = slab.T[:ROUTE_ROWS, :]
    wt_ref[...] = jnp.where(lane == 0, w1, jnp.where(lane == 1, w2, 0.0))


def _merge(xp, xs, att_p, att_s, rnn_p, rnn_s, ga, gr, w_att, w_rnn, w_out, g2, w_r, b_r):
    n_p, d = xp.shape
    n = ga.shape[0]
    npt = n_p // ROW_TILE
    aw = att_p.shape[1]
    rw = rnn_p.shape[1]
    pspec = lambda w: pl.BlockSpec((ROW_TILE, w), lambda i: (jnp.minimum(i, npt - 1), 0))
    sspec = lambda w: pl.BlockSpec((ROW_TILE, w), lambda i: (0, 0))
    row = lambda w: pl.BlockSpec((ROW_TILE, w), lambda i: (i, 0))
    return pl.pallas_call(
        functools.partial(_merge_kernel, npt),
        grid=(n // ROW_TILE,),
        in_specs=[pspec(d), sspec(d), pspec(aw), sspec(aw), pspec(rw), sspec(rw), row(d), row(d),
                  _const_spec(w_att.shape), _const_spec(w_rnn.shape), _const_spec(w_out.shape),
                  _const_spec((1, d)), _const_spec(w_r.shape), _const_spec((1, LANES))],
        out_specs=[row(d), row(d), pl.BlockSpec((ROUTE_ROWS, ROW_TILE), lambda i: (0, i)), row(LANES),
                   pl.BlockSpec((1, LANES), lambda i: (0, 0))],
        out_shape=[jax.ShapeDtypeStruct((n, d), F32), jax.ShapeDtypeStruct((n, d), F32),
                   jax.ShapeDtypeStruct((ROUTE_ROWS, n), F32), jax.ShapeDtypeStruct((n, LANES), F32),
                   jax.ShapeDtypeStruct((1, LANES), F32)],
        scratch_shapes=[pltpu.VMEM((1, LANES), F32)],
        compiler_params=_params(1),
        name="merge",
    )(xp, xs, att_p, att_s, rnn_p, rnn_s, ga, gr, w_att, w_rnn, w_out, g2, w_r, b_r)


def _row_copy(src_ref, src_row, dst_ref, dst_row, sem):
    return pltpu.make_async_copy(src_ref.at[pl.ds(src_row, 1)], dst_ref.at[pl.ds(dst_row, 1)], sem)


def _dispatch_kernel(pad_end_ref, padded_ref, pos0_ref, pos1_ref, hn_ref, xs_ref, zero_ref, sem):
    tm = hn_ref.shape[0]

    @pl.when(pl.program_id(0) == 0)
    def _():
        zero_ref[...] = jnp.zeros_like(zero_ref)

        def zero_block(start):
            start = pl.multiple_of(start, EXPERT_BLOCK)
            return pltpu.make_async_copy(zero_ref, xs_ref.at[pl.ds(start, EXPERT_BLOCK)], sem)

        used_rows = pad_end_ref[N_EXPERTS - 1]
        total_rows = xs_ref.shape[0]
        for wait in (False, True):
            for e in range(N_EXPERTS):
                @pl.when(padded_ref[e] > 0)
                def _():
                    cp = zero_block(pad_end_ref[e] - EXPERT_BLOCK)
                    cp.wait() if wait else cp.start()

                @pl.when(used_rows + e * EXPERT_BLOCK < total_rows)
                def _():
                    cp = zero_block(used_rows + e * EXPERT_BLOCK)
                    cp.wait() if wait else cp.start()

    for t in range(tm):
        _row_copy(hn_ref, t, xs_ref, pos0_ref[t], sem).start(priority=0)
        _row_copy(hn_ref, t, xs_ref, pos1_ref[t], sem).start(priority=1)
    for _ in range(TOP_K):
        pltpu.make_async_copy(hn_ref, xs_ref.at[pl.ds(0, tm)], sem).wait()


def _dispatch(pad_end, padded, pos0, pos1, hn, rows):
    n, d = hn.shape
    grid_spec = pltpu.PrefetchScalarGridSpec(
        num_scalar_prefetch=2,
        grid=(n // ROW_TILE,),
        in_specs=[pl.BlockSpec((ROW_TILE,), lambda i, pe, pd: (i,), memory_space=pltpu.SMEM),
                  pl.BlockSpec((ROW_TILE,), lambda i, pe, pd: (i,), memory_space=pltpu.SMEM),
                  pl.BlockSpec((ROW_TILE, d), lambda i, pe, pd: (i, 0))],
        out_specs=pl.BlockSpec(memory_space=pl.ANY),
        scratch_shapes=[pltpu.VMEM((EXPERT_BLOCK, d), hn.dtype), pltpu.SemaphoreType.DMA(())],
    )
    return pl.pallas_call(
        _dispatch_kernel,
        grid_spec=grid_spec,
        out_shape=jax.ShapeDtypeStruct((rows, d), hn.dtype),
        compiler_params=_params(1),
        name="dispatch",
    )(pad_end, padded, pos0, pos1, hn)


def _experts_kernel(be_ref, nu_ref, x_ref, wg_ref, wu_ref, wd_ref, y_ref, wg_bf, wu_bf, wd_bf):
    i = pl.program_id(0)
    used = i < nu_ref[0]

    @pl.when(used & ((i == 0) | (be_ref[i] != be_ref[jnp.maximum(i - 1, 0)])))
    def _():
        wg_bf[...] = wg_ref[0].astype(BF16)
        wu_bf[...] = wu_ref[0].astype(BF16)
        wd_bf[...] = wd_ref[0].astype(BF16)

    @pl.when(used)
    def _():
        x = x_ref[...].astype(BF16)
        g = jnp.dot(x, wg_bf[...], preferred_element_type=F32)
        u = jnp.dot(x, wu_bf[...], preferred_element_type=F32)
        hid = (g * _sigmoid(g) * u).astype(BF16)
        y_ref[...] = jnp.dot(hid, wd_bf[...], preferred_element_type=F32)

    @pl.when(jnp.logical_not(used))
    def _():
        y_ref[...] = jnp.zeros_like(y_ref)


def _experts(block_expert, n_used, xs, wg, wu, wd):
    rows, d = xs.shape
    de = wg.shape[2]
    nb = rows // EXPERT_BLOCK
    last = lambda i, nu: jnp.minimum(i, nu[0] - 1)
    grid_spec = pltpu.PrefetchScalarGridSpec(
        num_scalar_prefetch=2,
        grid=(nb,),
        in_specs=[pl.BlockSpec((EXPERT_BLOCK, d), lambda i, be, nu: (last(i, nu), 0)),
                  pl.BlockSpec((1, d, de), lambda i, be, nu: (be[last(i, nu)], 0, 0)),
                  pl.BlockSpec((1, d, de), lambda i, be, nu: (be[last(i, nu)], 0, 0)),
                  pl.BlockSpec((1, de, d), lambda i, be, nu: (be[last(i, nu)], 0, 0))],
        out_specs=pl.BlockSpec((EXPERT_BLOCK, d), lambda i, be, nu: (i, 0)),
        scratch_shapes=[pltpu.VMEM((d, de), BF16), pltpu.VMEM((d, de), BF16), pltpu.VMEM((de, d), BF16)],
    )
    return pl.pallas_call(
        _experts_kernel,
        grid_spec=grid_spec,
        out_shape=jax.ShapeDtypeStruct((rows, d), F32),
        compiler_params=_params(1),
        name="experts",
    )(block_expert, n_used, xs, wg, wu, wd)


def _combine_kernel(pos0_ref, pos1_ref, h_ref, wt_ref, gf_ref, y_hbm_ref, out_ref, buf_ref, sem):
    tm = h_ref.shape[0]

    for t in range(tm):
        _row_copy(y_hbm_ref, pos0_ref[t], buf_ref.at[0], t, sem).start(priority=0)
        _row_copy(y_hbm_ref, pos1_ref[t], buf_ref.at[1], t, sem).start(priority=1)
    for k in range(TOP_K):
        pltpu.make_async_copy(y_hbm_ref.at[pl.ds(0, tm)], buf_ref.at[k], sem).wait()
    wt = wt_ref[...]
    moe = buf_ref[0] * wt[:, 0:1] + buf_ref[1] * wt[:, 1:2]
    out_ref[...] = _rmsnorm(h_ref[...] + moe, gf_ref[...])


def _combine(pos0, pos1, h, wt, gf, y_sorted, row0, n_rows):
    d = h.shape[1]
    assert row0 % ROW_TILE == 0 and n_rows % ROW_TILE == 0
    blk0 = row0 // ROW_TILE
    return pl.pallas_call(
        _combine_kernel,
        grid=(n_rows // ROW_TILE,),
        in_specs=[pl.BlockSpec((ROW_TILE,), lambda i: (blk0 + i,), memory_space=pltpu.SMEM),
                  pl.BlockSpec((ROW_TILE,), lambda i: (blk0 + i,), memory_space=pltpu.SMEM),
                  pl.BlockSpec((ROW_TILE, d), lambda i: (blk0 + i, 0)),
                  pl.BlockSpec((ROW_TILE, LANES), lambda i: (blk0 + i, 0)),
                  _const_spec((1, d)),
                  pl.BlockSpec(memory_space=pl.ANY)],
        out_specs=pl.BlockSpec((ROW_TILE, d), lambda i: (i, 0)),
        out_shape=jax.ShapeDtypeStruct((n_rows, d), F32),
        scratch_shapes=[pltpu.VMEM((TOP_K, ROW_TILE, d), F32), pltpu.SemaphoreType.DMA(())],
        compiler_params=_params(1),
        name="combine",
    )(pos0, pos1, h, wt, gf, y_sorted)


def _prompt_bias_base(rel_bias):
    m = jnp.arange(BIAS_BASE)
    dist = (Q_BLOCK - 1) - m + (KEY_BLOCKS - 1) * Q_BLOCK
    idx = jnp.clip(dist, -MAX_REL, MAX_REL) + MAX_REL
    return rel_bias[:, idx].astype(F32)


def _sample_bias(rel_bias, cache_len, t):
    qi = jnp.arange(t)[:, None]
    kj = jnp.arange(cache_len + t)[None, :]
    idx = jnp.clip(qi + cache_len - kj, -MAX_REL, MAX_REL) + MAX_REL
    bias = rel_bias[:, idx].astype(F32)
    bias = jnp.where((kj >= cache_len - BAND_PAST)[None], bias, NEG_INF)
    return bias.reshape(N_HEADS * t, cache_len + t)


def _block_diag(w):
    nb, bw, _ = w.shape
    eye = jnp.eye(nb, dtype=w.dtype)
    return jnp.einsum("ncd,nm->ncmd", w, eye).reshape(nb * bw, nb * bw)


def kernel(x_prompt, x_sample, cache_k, cache_v, state_conv, state_h, norm1_g, w_in, b_in, rel_bias, w_att_branch, conv_w, conv_b, w_rec_gate, b_rec_gate, w_in_gate, b_in_gate, lru_lambda, w_rnn_branch, w_out, norm2_g, w_router_group, b_router_group, w_router_expert, b_router_expert, w_e_gate, w_e_up, w_e_down, final_norm_g):
    batch, seq, d = x_prompt.shape
    dec_b, dec_t, _ = x_sample.shape
    depth = w_in.shape[0]
    assert depth == 1
    att_w = N_HEADS * HEAD_DIM
    rnn_w = conv_w.shape[2]
    cache_len = cache_k.shape[2]
    n_p = batch * seq
    n_s = dec_b * dec_t
    n = n_p + n_s
    keep = min(BAND_PAST + CHUNK, seq)
    row2 = lambda a: a.reshape(1, -1).astype(F32)

    tiles_per_seq = seq // ROW_TILE
    tail_tiles = -(-keep // ROW_TILE)
    state_tiles = tuple(b * tiles_per_seq + tiles_per_seq - tail_tiles + j
                        for b in range(batch) for j in range(tail_tiles)) + (n_p // ROW_TILE,)

    xp = x_prompt.reshape(n_p, d)
    xs = x_sample.reshape(n_s, d)
    q, k, v, kvt, rx, rgate, ga, gr = _proj(xp, xs, row2(norm1_g[0]), w_in[0].astype(BF16), row2(b_in[0]),
                                            att_w, rnn_w, state_tiles)

    att_p = _attn_prompt(q, k, v, _prompt_bias_base(rel_bias[0]), batch, seq)
    att_s, new_k_s, new_v_s = _attn_sample(
        q, k, v, kvt, cache_k[0].reshape(dec_b, cache_len, att_w), cache_v[0].reshape(dec_b, cache_len, att_w),
        _sample_bias(rel_bias[0], cache_len, dec_t), n_p, batch * tail_tiles * ROW_TILE, dec_b, dec_t)

    rnn_args = (conv_w[0].astype(F32), row2(conv_b[0]),
                _block_diag(w_rec_gate[0]).astype(BF16), row2(b_rec_gate[0]),
                _block_diag(w_in_gate[0]).astype(BF16), row2(b_in_gate[0]), row2(lru_lambda[0]))
    pad_state = lambda s: jnp.pad(s.astype(F32), ((0, 0), (SUBLANES - (CONV_W - 1), 0), (0, 0)))
    rnn_p, conv_p, h_p = _rnn_prompt(rx, rgate, *rnn_args, batch, seq, ROW_TILE)
    rnn_s, conv_s, h_s = _rnn(rx, rgate, pad_state(state_conv[0]), state_h[0].astype(F32)[:, None, :],
                              *rnn_args, n_p, dec_b, dec_t, dec_t)

    w_r = jnp.zeros((d, LANES), F32).at[:, :N_GROUPS].set(w_router_group[0])
    w_r = w_r.at[:, N_GROUPS:N_GROUPS + N_EXPERTS].set(w_router_expert[0]).astype(BF16)
    b_r = jnp.zeros((1, LANES), F32).at[0, :N_GROUPS].set(b_router_group[0])
    b_r = b_r.at[0, N_GROUPS:N_GROUPS + N_EXPERTS].set(b_router_expert[0])
    h, hn, route, wt, cnt = _merge(xp, xs, att_p, att_s, rnn_p, rnn_s, ga, gr,
                                   w_att_branch[0].astype(BF16), w_rnn_branch[0].astype(BF16), w_out[0].astype(BF16),
                                   row2(norm2_g[0]), w_r, b_r)

    counts = cnt[0, :N_EXPERTS].astype(jnp.int32)
    padded = (counts + EXPERT_BLOCK - 1) // EXPERT_BLOCK * EXPERT_BLOCK
    pad_end = jnp.cumsum(padded)
    pad_start = pad_end - padded
    expert = route[0:TOP_K].astype(jnp.int32)
    rank = route[TOP_K:2 * TOP_K].astype(jnp.int32)
    ids = jnp.arange(N_EXPERTS, dtype=jnp.int32)[:, None, None]
    pos = jnp.sum(jnp.where(expert[None] == ids, pad_start[:, None, None], 0), axis=0) + rank
    n_blocks = -(-(n * TOP_K) // EXPERT_BLOCK) + N_EXPERTS
    rows = n_blocks * EXPERT_BLOCK
    block_start = jnp.arange(n_blocks, dtype=jnp.int32) * EXPERT_BLOCK
    block_expert = jnp.minimum(jnp.sum((block_start[:, None] >= pad_end[None, :]).astype(jnp.int32), axis=1),
                               N_EXPERTS - 1)
    n_used = (pad_end[-1:] // EXPERT_BLOCK).astype(jnp.int32)

    x_sorted = _dispatch(pad_end, padded, pos[0], pos[1], hn, rows)
    y_sorted = _experts(block_expert, n_used, x_sorted, w_e_gate[0], w_e_up[0], w_e_down[0])
    gf = row2(final_norm_g)
    y_p = _combine(pos[0], pos[1], h, wt, gf, y_sorted, 0, n_p)
    y_s = _combine(pos[0], pos[1], h, wt, gf, y_sorted, n_p, n_s)

    def kv_state(col0):
        span = tail_tiles * ROW_TILE
        tails = [kvt[b * span + span - keep:(b + 1) * span, col0:col0 + att_w] for b in range(batch)]
        return jnp.stack(tails).reshape(1, batch, keep, N_HEADS, HEAD_DIM)

    heads = lambda a: a.reshape(1, dec_b, cache_len, N_HEADS, HEAD_DIM)
    return (y_p.reshape(batch, seq, d), y_s.reshape(dec_b, dec_t, d),
            kv_state(0), kv_state(att_w),
            conv_p[:, SUBLANES - (CONV_W - 1):][None], h_p[:, 0][None],
            heads(new_k_s), heads(new_v_s),
            conv_s[:, SUBLANES - (CONV_W - 1):][None], h_s[:, 0][None])
```

```python
import functools

import jax
import jax.numpy as jnp
from jax import lax
from jax.experimental import pallas as pl
from jax.experimental.pallas import tpu as pltpu

CHUNK = 64
LEFT_CHUNKS = 8
BAND_PAST = LEFT_CHUNKS * CHUNK
N_HEADS = 8
HEAD_DIM = 64
MAX_REL = 256
NEG_INF = -1e30
RNN_BLOCKS = 8
CONV_W = 4
RG_C = 8.0
N_GROUPS = 4
EXPERTS_PER_GROUP = 8
N_EXPERTS = N_GROUPS * EXPERTS_PER_GROUP
TOP_K = 2
EPS = 1e-6

LANES = 128
SUBLANES = 8
ROW_TILE = 512
Q_BLOCK = 256
KEY_BLOCKS = 3
KEY_WINDOW = KEY_BLOCKS * Q_BLOCK
BIAS_BASE = 1024
EXPERT_BLOCK = 512
VMEM_LIMIT = 56 * 1024 * 1024

BF16 = jnp.bfloat16
F32 = jnp.float32


def _params(n_axes, vmem=VMEM_LIMIT):
    return pltpu.CompilerParams(dimension_semantics=("arbitrary",) * n_axes, vmem_limit_bytes=vmem)


def _const_spec(shape):
    zeros = (0,) * len(shape)
    return pl.BlockSpec(shape, lambda *_: zeros, pipeline_mode=pl.Buffered(1))


def _rmsnorm(x, g):
    return x * lax.rsqrt(jnp.mean(x * x, axis=-1, keepdims=True) + EPS) * g


def _pack_bf16_halves(x):
    c = x.shape[1] // 2
    lo = lax.bitcast_convert_type(x[:, :c].astype(BF16).astype(F32), jnp.uint32)
    hi = lax.bitcast_convert_type(x[:, c:].astype(BF16).astype(F32), jnp.uint32)
    return (lo >> 16) | (hi & jnp.uint32(0xFFFF0000))


def _unpack_bf16_halves(p):
    lo = lax.bitcast_convert_type(p << 16, F32)
    hi = lax.bitcast_convert_type(p & jnp.uint32(0xFFFF0000), F32)
    return lo, hi


def _sigmoid(x):
    return 0.5 * jnp.tanh(0.5 * x) + 0.5


def _proj_kernel(n_prompt_tiles, state_tiles, att_w, rnn_w, d_model,
                 xp_ref, xs_ref, g_ref, w_ref, b_ref,
                 q_ref, k_ref, v_ref, kvt_ref, rx_ref, rg_ref, ga_ref, gr_ref):
    i = pl.program_id(0)
    x = jnp.where(i < n_prompt_tiles, xp_ref[...], xs_ref[...])
    xn = _rmsnorm(x, g_ref[...]).astype(BF16)

    def seg(lo, width):
        return jnp.dot(xn, w_ref[:, lo:lo + width], preferred_element_type=F32) + b_ref[:, lo:lo + width]

    q = seg(0, att_w)
    k = seg(att_w, att_w)
    v = seg(2 * att_w, att_w)
    q_ref[...] = (q * (HEAD_DIM ** -0.5)).astype(BF16)
    k_ref[...] = k.astype(BF16)
    v_ref[...] = v.astype(BF16)

    @pl.when(functools.reduce(jnp.logical_or, [i == t for t in state_tiles]))
    def _():
        kvt_ref[:, :att_w] = k
        kvt_ref[:, att_w:] = v

    o = 3 * att_w
    rx = seg(o, rnn_w)
    for c in range(rnn_w // LANES):
        rx_ref[c] = rx[:, c * LANES:(c + 1) * LANES]
    rg_ref[...] = seg(o + rnn_w, rnn_w).astype(BF16)
    ga_ref[...] = seg(o + 2 * rnn_w, d_model).astype(BF16)
    gr_ref[...] = seg(o + 2 * rnn_w + d_model, d_model).astype(BF16)


def _proj(xp, xs, g1, w_in, b_in, att_w, rnn_w, state_tiles):
    n_p, d = xp.shape
    n_s = xs.shape[0]
    assert n_p % ROW_TILE == 0 and n_s == ROW_TILE
    n = n_p + n_s
    npt = n_p // ROW_TILE
    pw = w_in.shape[1]
    row = lambda w: pl.BlockSpec((ROW_TILE, w), lambda i: (i, 0))
    outs = [(att_w, BF16), (att_w, BF16), (att_w, BF16), (2 * att_w, F32), (rnn_w, F32), (rnn_w, BF16),
            (d, BF16), (d, BF16)]
    slot = lambda i: sum(jnp.where(i > t, 1, 0) for t in state_tiles)
    out_specs = [row(w) for w, _ in outs]
    out_specs[3] = pl.BlockSpec((ROW_TILE, 2 * att_w), lambda i: (slot(i), 0))
    out_specs[4] = pl.BlockSpec((rnn_w // LANES, ROW_TILE, LANES), lambda i: (0, i, 0))
    out_shape = [jax.ShapeDtypeStruct((n, w), dt) for w, dt in outs]
    out_shape[3] = jax.ShapeDtypeStruct((len(state_tiles) * ROW_TILE, 2 * att_w), F32)
    out_shape[4] = jax.ShapeDtypeStruct((rnn_w // LANES, n, LANES), F32)
    return pl.pallas_call(
        functools.partial(_proj_kernel, npt, state_tiles, att_w, rnn_w, d),
        grid=(n // ROW_TILE,),
        in_specs=[pl.BlockSpec((ROW_TILE, d), lambda i: (jnp.minimum(i, npt - 1), 0)),
                  pl.BlockSpec((ROW_TILE, d), lambda i: (0, 0)),
                  _const_spec((1, d)), _const_spec((d, pw)), _const_spec((1, pw))],
        out_specs=out_specs,
        out_shape=out_shape,
        compiler_params=_params(1),
        name="proj",
    )(xp, xs, g1, w_in, b_in)


def _attn_p_kernel(q_ref, k0_ref, k1_ref, k2_ref, v0_ref, v1_ref, v2_ref, base_ref, o_ref, bias_ref):
    b = pl.program_id(0)
    i = pl.program_id(1)

    @pl.when((b == 0) & (i == 0))
    def _():
        r = lax.broadcasted_iota(jnp.int32, (Q_BLOCK, KEY_WINDOW), 0) // CHUNK
        c = lax.broadcasted_iota(jnp.int32, (Q_BLOCK, KEY_WINDOW), 1) // CHUNK
        in_band = (c >= r) & (c <= r + LEFT_CHUNKS)
        for h in range(N_HEADS):
            rows = jnp.broadcast_to(base_ref[h:h + 1, :], (Q_BLOCK, BIAS_BASE))
            toeplitz = pltpu.roll(rows, BIAS_BASE - (Q_BLOCK - 1), axis=1, stride=1, stride_axis=0)
            bias_ref[h] = jnp.where(in_band, toeplitz[:, :KEY_WINDOW], NEG_INF)

    kwin = jnp.concatenate([k0_ref[...], k1_ref[...], k2_ref[...]], axis=0)
    vwin = jnp.concatenate([v0_ref[...], v1_ref[...], v2_ref[...]], axis=0)
    lane = lax.broadcasted_iota(jnp.int32, (1, LANES), 1)

    def heads(at_sequence_start):
        if at_sequence_start:
            col = lax.broadcasted_iota(jnp.int32, (1, KEY_WINDOW), 1)
            start_mask = jnp.where(col // Q_BLOCK + i - (KEY_BLOCKS - 1) >= 0, 0.0, NEG_INF).astype(F32)
        for pair in range(N_HEADS // 2):
            sl = slice(pair * LANES, (pair + 1) * LANES)
            q2 = q_ref[:, sl]
            k2 = kwin[:, sl]
            v2 = vwin[:, sl]
            acc = jnp.zeros((Q_BLOCK, LANES), F32)
            for half in range(2):
                hmask = (lane // HEAD_DIM) == half
                qm = jnp.where(hmask, q2, jnp.zeros_like(q2))
                vm = jnp.where(hmask, v2, jnp.zeros_like(v2))
                s = lax.dot_general(qm, k2, (((1,), (1,)), ((), ())), preferred_element_type=F32)
                s = s + bias_ref[2 * pair + half]
                if at_sequence_start:
                    s = s + start_mask
                m = jnp.max(s, axis=-1, keepdims=True)
                e = jnp.exp(s - m)
                l = jnp.sum(e, axis=-1, keepdims=True)
                o = jnp.dot(e.astype(BF16), vm, preferred_element_type=F32)
                acc = acc + o / l
            o_ref[:, sl] = acc.astype(BF16)

    pl.when(i < KEY_BLOCKS - 1)(lambda: heads(True))
    pl.when(i >= KEY_BLOCKS - 1)(lambda: heads(False))


def _attn_prompt(q, k, v, base, batch, seq):
    w = q.shape[1]
    nqb = seq // Q_BLOCK
    kspec = lambda back: pl.BlockSpec((Q_BLOCK, w), lambda b, i: (b * nqb + jnp.maximum(i - back, 0), 0))
    return pl.pallas_call(
        _attn_p_kernel,
        grid=(batch, nqb),
        in_specs=[pl.BlockSpec((Q_BLOCK, w), lambda b, i: (b * nqb + i, 0)),
                  kspec(2), kspec(1), kspec(0), kspec(2), kspec(1), kspec(0),
                  _const_spec(base.shape)],
        out_specs=pl.BlockSpec((Q_BLOCK, w), lambda b, i: (b * nqb + i, 0)),
        out_shape=jax.ShapeDtypeStruct((batch * seq, w), BF16),
        scratch_shapes=[pltpu.VMEM((N_HEADS, Q_BLOCK, KEY_WINDOW), F32)],
        compiler_params=_params(2),
        name="attn_prompt",
    )(q, k, k, k, v, v, v, base)


def _attn_s_kernel(t, q_ref, k_ref, v_ref, kvt_ref, ck_ref, cv_ref, bias_ref, o_ref, nk_ref, nv_ref):
    cache_len = ck_ref.shape[1]
    w = q_ref.shape[1]
    ck = ck_ref[0]
    cv = cv_ref[0]
    kall = jnp.concatenate([ck.astype(BF16), k_ref[...]], axis=0)
    vall = jnp.concatenate([cv.astype(BF16), v_ref[...]], axis=0)
    q = q_ref[...]
    head_of_lane = lax.broadcasted_iota(jnp.int32, (1, w), 1) // HEAD_DIM
    qs = jnp.concatenate([jnp.where(head_of_lane == h, q, jnp.zeros_like(q)) for h in range(N_HEADS)], axis=0)
    s = lax.dot_general(qs, kall, (((1,), (1,)), ((), ())), preferred_element_type=F32) + bias_ref[...]
    m = jnp.max(s, axis=-1, keepdims=True)
    e = jnp.exp(s - m)
    l = jnp.sum(e, axis=-1, keepdims=True)
    o_all = jnp.dot(e.astype(BF16), vall, preferred_element_type=F32) / l
    o = jnp.zeros((t, w), F32)
    for h in range(N_HEADS):
        o = o + jnp.where(head_of_lane == h, o_all[h * t:(h + 1) * t], 0.0)
    o_ref[...] = o.astype(BF16)
    nk_ref[0, :cache_len - t, :] = ck[t:]
    nk_ref[0, cache_len - t:, :] = kvt_ref[:, :w]
    nv_ref[0, :cache_len - t, :] = cv[t:]
    nv_ref[0, cache_len - t:, :] = kvt_ref[:, w:]


def _attn_sample(q, k, v, kvt, cache_k, cache_v, bias, row0, kvt_row0, dec_b, t):
    w = q.shape[1]
    cache_len = cache_k.shape[1]
    assert row0 % t == 0 and kvt_row0 % t == 0
    tok = lambda width, r0: pl.BlockSpec((t, width), lambda b: (r0 // t + b, 0))
    cache = pl.BlockSpec((1, cache_len, w), lambda b: (b, 0, 0))
    return pl.pallas_call(
        functools.partial(_attn_s_kernel, t),
        grid=(dec_b,),
        in_specs=[tok(w, row0), tok(w, row0), tok(w, row0), tok(2 * w, kvt_row0), cache, cache,
                  _const_spec(bias.shape)],
        out_specs=[pl.BlockSpec((t, w), lambda b: (b, 0)), cache, cache],
        out_shape=[jax.ShapeDtypeStruct((dec_b * t, w), BF16),
                   jax.ShapeDtypeStruct(cache_k.shape, F32), jax.ShapeDtypeStruct(cache_v.shape, F32)],
        compiler_params=_params(1),
        name="attn_sample",
    )(q, k, v, kvt, cache_k, cache_v, bias)


def _rnn_kernel(tm, rx_ref, rg_ref, conv0_ref, h0_ref, cw_ref, cb_ref, wa_ref, ba_ref, wx_ref, bx_ref, lam_ref,
                rnn_ref, conv_out_ref, h_out_ref, ext_ref, a_ref, b_ref, hs_ref, h_ref):
    j = pl.program_id(1)

    @pl.when(j == 0)
    def _():
        ext_ref[:SUBLANES, :] = conv0_ref[0]
        h_ref[...] = h0_ref[0]

    ext_ref[SUBLANES:, :] = jnp.concatenate([rx_ref[c] for c in range(rx_ref.shape[0])], axis=1)
    cw = cw_ref[...]
    xc = cb_ref[...]
    for tap in range(CONV_W):
        off = SUBLANES - (CONV_W - 1) + tap
        xc = xc + ext_ref[off:off + tm, :] * cw[tap:tap + 1, :]
    tail = ext_ref[tm:tm + SUBLANES, :]
    ext_ref[:SUBLANES, :] = tail
    conv_out_ref[0] = tail

    a, b = _rglru_coeffs(xc, wa_ref, ba_ref, wx_ref, bx_ref, lam_ref)

    sub = lax.broadcasted_iota(jnp.int32, a.shape, 0) % SUBLANES
    shift = 1
    while shift < SUBLANES:
        keep = sub >= shift
        a_prev = pltpu.roll(a, shift, axis=0)
        b_prev = pltpu.roll(b, shift, axis=0)
        b = jnp.where(keep, a * b_prev + b, b)
        a = jnp.where(keep, a * a_prev, a)
        shift *= 2
    a_ref[...] = a
    b_ref[...] = b

    def group(g, h):
        r0 = pl.multiple_of(g * SUBLANES, SUBLANES)
        hg = b_ref[pl.ds(r0, SUBLANES), :] + a_ref[pl.ds(r0, SUBLANES), :] * h
        hs_ref[pl.ds(r0, SUBLANES), :] = hg
        return hg[SUBLANES - 1:SUBLANES, :]

    h_last = lax.fori_loop(0, tm // SUBLANES, group, h_ref[...])
    h_ref[...] = h_last
    h_out_ref[0] = h_last
    rnn_ref[...] = (hs_ref[...] * jax.nn.gelu(rg_ref[...].astype(F32))).astype(BF16)


def _rnn(rx, rgate, conv0, h0, cw, cb, wa, ba, wx, bx, lam, row0, batch, seq, tm):
    w = rgate.shape[1]
    assert row0 % tm == 0 and seq % tm == 0 and tm % SUBLANES == 0
    blk0 = row0 // tm
    nt = seq // tm
    tok = pl.BlockSpec((tm, w), lambda b, j: (blk0 + b * nt + j, 0))
    tok_cols = pl.BlockSpec((w // LANES, tm, LANES), lambda b, j: (0, blk0 + b * nt + j, 0))
    state = lambda rows: pl.BlockSpec((1, rows, w), lambda b, j: (b, 0, 0))
    return pl.pallas_call(
        functools.partial(_rnn_kernel, tm),
        grid=(batch, nt),
        in_specs=[tok_cols, tok, state(SUBLANES), state(1),
                  _const_spec(cw.shape), _const_spec((1, w)), _const_spec((w, w)), _const_spec((1, w)),
                  _const_spec((w, w)), _const_spec((1, w)), _const_spec((1, w))],
        out_specs=[pl.BlockSpec((tm, w), lambda b, j: (b * nt + j, 0)), state(SUBLANES), state(1)],
        out_shape=[jax.ShapeDtypeStruct((batch * seq, w), BF16),
                   jax.ShapeDtypeStruct((batch, SUBLANES, w), F32),
                   jax.ShapeDtypeStruct((batch, 1, w), F32)],
        scratch_shapes=[pltpu.VMEM((tm + SUBLANES, w), F32), pltpu.VMEM((tm, w), F32), pltpu.VMEM((tm, w), F32),
                        pltpu.VMEM((tm, w), F32), pltpu.VMEM((1, w), F32)],
        compiler_params=_params(2),
        name="rnn",
    )(rx, rgate, conv0, h0, cw, cb, wa, ba, wx, bx, lam)


def _rglru_coeffs(xc, wa_ref, ba_ref, wx_ref, bx_ref, lam_ref):
    xb = xc.astype(BF16)
    r = _sigmoid(jnp.dot(xb, wa_ref[...], preferred_element_type=F32) + ba_ref[...])
    gi = _sigmoid(jnp.dot(xb, wx_ref[...], preferred_element_type=F32) + bx_ref[...])
    neg_lam = -lam_ref[...]
    softplus = jnp.maximum(neg_lam, 0.0) + jnp.log1p(jnp.exp(-jnp.abs(neg_lam)))
    log_a = (-RG_C) * r * softplus
    a = jnp.exp(log_a)
    b = jnp.sqrt(-jnp.tanh(log_a) * (a * a + 1.0)) * (gi * xc)
    return a, b


def _rnn_prompt_kernel(tm, rx_ref, rg_ref, cw_ref, cb_ref, wa_ref, ba_ref, wx_ref, bx_ref, lam_ref,
                       rnn_ref, conv_out_ref, h_out_ref, tail_ref, hl_ref, p_ref, hs_ref, h_ref):
    seg = tm // SUBLANES
    ncol = rx_ref.shape[0]
    w = ncol * LANES

    @pl.when(pl.program_id(1) == 0)
    def _():
        tail_ref[...] = jnp.zeros_like(tail_ref)
        h_ref[...] = jnp.zeros_like(h_ref)

    def strided_rows(ref, j):
        return jnp.concatenate([ref[c, pl.ds(j, SUBLANES, stride=seg), :] for c in range(ncol)], axis=1)

    slab = [strided_rows(rx_ref, j) for j in range(seg)]
    sub = lax.broadcasted_iota(jnp.int32, (SUBLANES, w), 0)

    def before(k):
        carry = jnp.broadcast_to(tail_ref[SUBLANES - k:SUBLANES - k + 1, :], (SUBLANES, w))
        return jnp.where(sub == 0, carry, pltpu.roll(slab[seg - k], 1, axis=0))

    lead = {k: before(k) for k in range(1, CONV_W)}
    cw = cw_ref[...]
    cb = cb_ref[...]
    xc = []
    for j in range(seg):
        acc = cb + slab[j] * cw[CONV_W - 1:CONV_W, :]
        for k in range(1, CONV_W):
            prev = slab[j - k] if j >= k else lead[k - j]
            acc = acc + prev * cw[CONV_W - 1 - k:CONV_W - k, :]
        xc.append(acc)
    tail = jnp.concatenate([rx_ref[c, tm - SUBLANES:tm, :] for c in range(ncol)], axis=1)
    tail_ref[...] = tail
    conv_out_ref[0] = tail

    a, b = _rglru_coeffs(jnp.concatenate(xc, axis=0), wa_ref, ba_ref, wx_ref, bx_ref, lam_ref)

    h = jnp.zeros((SUBLANES, w), F32)
    p = jnp.ones((SUBLANES, w), F32)
    for j in range(seg):
        rows = slice(j * SUBLANES, (j + 1) * SUBLANES)
        h = a[rows] * h + b[rows]
        p = a[rows] * p
        hl_ref[rows, :] = h
        p_ref[rows, :] = p
    c = h_ref[...]
    enter = []
    for s in range(SUBLANES):
        enter.append(c)
        c = h[s:s + 1, :] + p[s:s + 1, :] * c
    h_ref[...] = c
    h_out_ref[0] = c
    enter = jnp.concatenate(enter, axis=0)
    for j in range(seg):
        rows = slice(j * SUBLANES, (j + 1) * SUBLANES)
        hj = hl_ref[rows, :] + p_ref[rows, :] * enter
        for c in range(ncol):
            hs_ref[c, pl.ds(j, SUBLANES, stride=seg), :] = hj[:, c * LANES:(c + 1) * LANES]
    hseq = jnp.concatenate([hs_ref[c] for c in range(ncol)], axis=1)
    rnn_ref[...] = (hseq * jax.nn.gelu(rg_ref[...].astype(F32))).astype(BF16)


def _rnn_prompt(rx, rgate, cw, cb, wa, ba, wx, bx, lam, batch, seq, tm):
    w = rgate.shape[1]
    assert seq % tm == 0 and tm % (SUBLANES * SUBLANES) == 0 and tm // SUBLANES >= CONV_W
    nt = seq // tm
    tok = pl.BlockSpec((tm, w), lambda b, j: (b * nt + j, 0))
    tok_cols = pl.BlockSpec((w // LANES, tm, LANES), lambda b, j: (0, b * nt + j, 0))
    state = lambda rows: pl.BlockSpec((1, rows, w), lambda b, j: (b, 0, 0))
    return pl.pallas_call(
        functools.partial(_rnn_prompt_kernel, tm),
        grid=(batch, nt),
        in_specs=[tok_cols, tok,
                  _const_spec(cw.shape), _const_spec((1, w)), _const_spec((w, w)), _const_spec((1, w)),
                  _const_spec((w, w)), _const_spec((1, w)), _const_spec((1, w))],
        out_specs=[tok, state(SUBLANES), state(1)],
        out_shape=[jax.ShapeDtypeStruct((batch * seq, w), BF16),
                   jax.ShapeDtypeStruct((batch, SUBLANES, w), F32),
                   jax.ShapeDtypeStruct((batch, 1, w), F32)],
        scratch_shapes=[pltpu.VMEM((SUBLANES, w), F32), pltpu.VMEM((tm, w), F32), pltpu.VMEM((tm, w), F32),
                        pltpu.VMEM((w // LANES, tm, LANES), F32), pltpu.VMEM((1, w), F32)],
        compiler_params=_params(2),
        name="rnn_prompt",
    )(rx, rgate, cw, cb, wa, ba, wx, bx, lam)


ROUTE_ROWS = SUBLANES


def _merge_kernel(n_prompt_tiles,
                  xp_ref, xs_ref, attp_ref, atts_ref, rnnp_ref, rnns_ref, ga_ref, gr_ref,
                  watt_ref, wrnn_ref, wout_ref, g2_ref, wr_ref, br_ref,
                  h_ref, hn_ref, route_ref, wt_ref, cnt_ref, carry_ref):
    i = pl.program_id(0)
    is_p = i < n_prompt_tiles

    @pl.when(i == 0)
    def _():
        carry_ref[...] = jnp.zeros_like(carry_ref)

    x = jnp.where(is_p, xp_ref[...], xs_ref[...])
    att = jnp.where(is_p, attp_ref[...], atts_ref[...])
    rnn = jnp.where(is_p, rnnp_ref[...], rnns_ref[...])
    ya = jnp.dot(att, watt_ref[...], preferred_element_type=F32)
    yr = jnp.dot(rnn, wrnn_ref[...], preferred_element_type=F32)
    mixed = _sigmoid(ga_ref[...].astype(F32)) * ya + _sigmoid(gr_ref[...].astype(F32)) * yr
    h = x + jnp.dot(mixed.astype(BF16), wout_ref[...], preferred_element_type=F32)
    h_ref[...] = h
    hn = _rmsnorm(h, g2_ref[...])
    hn_ref[...] = _pack_bf16_halves(hn)

    logits = jnp.dot(hn.astype(BF16), wr_ref[...], preferred_element_type=F32) + br_ref[...]
    tm = logits.shape[0]
    lane = lax.broadcasted_iota(jnp.int32, (tm, LANES), 1)
    low = jnp.float32(-3e38)
    big = jnp.int32(1 << 20)
    lg = jnp.where(lane < N_GROUPS, logits, low)
    mg = jnp.max(lg, axis=-1, keepdims=True)
    p_grp = 1.0 / jnp.sum(jnp.exp(lg - mg), axis=-1, keepdims=True)
    grp = jnp.min(jnp.where(lg == mg, lane, big), axis=-1, keepdims=True)
    lo = N_GROUPS + EXPERTS_PER_GROUP * grp
    le = jnp.where((lane >= lo) & (lane < lo + EXPERTS_PER_GROUP), logits, low)
    m1 = jnp.max(le, axis=-1, keepdims=True)
    i1 = jnp.min(jnp.where(le == m1, lane, big), axis=-1, keepdims=True)
    le2 = jnp.where(lane == i1, low, le)
    m2 = jnp.max(le2, axis=-1, keepdims=True)
    i2 = jnp.min(jnp.where(le2 == m2, lane, big), axis=-1, keepdims=True)
    e2 = jnp.exp(m2 - m1)
    inv = 1.0 / (1.0 + e2)
    w1 = p_grp * inv
    w2 = p_grp * (e2 * inv)
    x1 = i1 - N_GROUPS
    x2 = i2 - N_GROUPS

    onehot = jnp.where((lane == x1) | (lane == x2), 1.0, 0.0).astype(BF16)
    rowi = lax.broadcasted_iota(jnp.int32, (tm, tm), 0)
    coli = lax.broadcasted_iota(jnp.int32, (tm, tm), 1)
    tri = jnp.where(coli < rowi, 1.0, 0.0).astype(BF16)
    before = jnp.dot(tri, onehot, preferred_element_type=F32) + carry_ref[...]
    r1 = jnp.sum(jnp.where(lane == x1, before, 0.0), axis=-1, keepdims=True)
    r2 = jnp.sum(jnp.where(lane == x2, before, 0.0), axis=-1, keepdims=True)
    carry = carry_ref[...] + jnp.sum(onehot.astype(F32), axis=0, keepdims=True)
    carry_ref[...] = carry
    cnt_ref[...] = carry

    fields = (x1.astype(F32), x2.astype(F32), r1, r2, w1, w2)
    slab = jnp.zeros((tm, LANES), F32)
    for idx, val in enumerate(fields):
        slab = jnp.where(lane == idx, val, slab)
    route_ref[...] = slab.T[:ROUTE_ROWS, :]
    wt_ref[...] = jnp.where(lane == 0, w1, jnp.where(lane == 1, w2, 0.0))


def _merge(xp, xs, att_p, att_s, rnn_p, rnn_s, ga, gr, w_att, w_rnn, w_out, g2, w_r, b_r):
    n_p, d = xp.shape
    n = ga.shape[0]
    npt = n_p // ROW_TILE
    aw = att_p.shape[1]
    rw = rnn_p.shape[1]
    pspec = lambda w: pl.BlockSpec((ROW_TILE, w), lambda i: (jnp.minimum(i, npt - 1), 0))
    sspec = lambda w: pl.BlockSpec((ROW_TILE, w), lambda i: (0, 0))
    row = lambda w: pl.BlockSpec((ROW_TILE, w), lambda i: (i, 0))
    return pl.pallas_call(
        functools.partial(_merge_kernel, npt),
        grid=(n // ROW_TILE,),
        in_specs=[pspec(d), sspec(d), pspec(aw), sspec(aw), pspec(rw), sspec(rw), row(d), row(d),
                  _const_spec(w_att.shape), _const_spec(w_rnn.shape), _const_spec(w_out.shape),
                  _const_spec((1, d)), _const_spec(w_r.shape), _const_spec((1, LANES))],
        out_specs=[row(d), row(d // 2), pl.BlockSpec((ROUTE_ROWS, ROW_TILE), lambda i: (0, i)), row(LANES),
                   pl.BlockSpec((1, LANES), lambda i: (0, 0))],
        out_shape=[jax.ShapeDtypeStruct((n, d), F32), jax.ShapeDtypeStruct((n, d // 2), jnp.uint32),
                   jax.ShapeDtypeStruct((ROUTE_ROWS, n), F32), jax.ShapeDtypeStruct((n, LANES), F32),
                   jax.ShapeDtypeStruct((1, LANES), F32)],
        scratch_shapes=[pltpu.VMEM((1, LANES), F32)],
        compiler_params=_params(1),
        name="merge",
    )(xp, xs, att_p, att_s, rnn_p, rnn_s, ga, gr, w_att, w_rnn, w_out, g2, w_r, b_r)


def _row_copy(src_ref, src_row, dst_ref, dst_row, sem):
    return pltpu.make_async_copy(src_ref.at[pl.ds(src_row, 1)], dst_ref.at[pl.ds(dst_row, 1)], sem)


def _dispatch_kernel(pad_end_ref, padded_ref, pos0_ref, pos1_ref, hn_ref, xs_ref, zero_ref, sem):
    tm = hn_ref.shape[0]

    @pl.when(pl.program_id(0) == 0)
    def _():
        zero_ref[...] = jnp.zeros_like(zero_ref)

        def zero_block(start):
            start = pl.multiple_of(start, EXPERT_BLOCK)
            return pltpu.make_async_copy(zero_ref, xs_ref.at[pl.ds(start, EXPERT_BLOCK)], sem)

        used_rows = pad_end_ref[N_EXPERTS - 1]
        total_rows = xs_ref.shape[0]
        for wait in (False, True):
            for e in range(N_EXPERTS):
                @pl.when(padded_ref[e] > 0)
                def _():
                    cp = zero_block(pad_end_ref[e] - EXPERT_BLOCK)
                    cp.wait() if wait else cp.start()

                @pl.when(used_rows + e * EXPERT_BLOCK < total_rows)
                def _():
                    cp = zero_block(used_rows + e * EXPERT_BLOCK)
                    cp.wait() if wait else cp.start()

    for t in range(tm):
        _row_copy(hn_ref, t, xs_ref, pos0_ref[t], sem).start(priority=0)
        _row_copy(hn_ref, t, xs_ref, pos1_ref[t], sem).start(priority=1)
    for _ in range(TOP_K):
        pltpu.make_async_copy(hn_ref, xs_ref.at[pl.ds(0, tm)], sem).wait()


def _dispatch(pad_end, padded, pos0, pos1, hn, rows):
    n, d = hn.shape
    grid_spec = pltpu.PrefetchScalarGridSpec(
        num_scalar_prefetch=2,
        grid=(n // ROW_TILE,),
        in_specs=[pl.BlockSpec((ROW_TILE,), lambda i, pe, pd: (i,), memory_space=pltpu.SMEM),
                  pl.BlockSpec((ROW_TILE,), lambda i, pe, pd: (i,), memory_space=pltpu.SMEM),
                  pl.BlockSpec((ROW_TILE, d), lambda i, pe, pd: (i, 0))],
        out_specs=pl.BlockSpec(memory_space=pl.ANY),
        scratch_shapes=[pltpu.VMEM((EXPERT_BLOCK, d), hn.dtype), pltpu.SemaphoreType.DMA(())],
    )
    return pl.pallas_call(
        _dispatch_kernel,
        grid_spec=grid_spec,
        out_shape=jax.ShapeDtypeStruct((rows, d), hn.dtype),
        compiler_params=_params(1),
        name="dispatch",
    )(pad_end, padded, pos0, pos1, hn)


def _experts_kernel(be_ref, nu_ref, x_ref, wg_ref, wu_ref, wd_ref, y_ref, wg_bf, wu_bf, wd_bf):
    i = pl.program_id(0)
    used = i < nu_ref[0]

    @pl.when(used & ((i == 0) | (be_ref[i] != be_ref[jnp.maximum(i - 1, 0)])))
    def _():
        wg_bf[...] = wg_ref[0].astype(BF16)
        wu_bf[...] = wu_ref[0].astype(BF16)
        wd_bf[...] = wd_ref[0].astype(BF16)

    @pl.when(used)
    def _():
        half = wg_bf.shape[0] // 2
        x_lo, x_hi = (part.astype(BF16) for part in _unpack_bf16_halves(x_ref[...]))

        def up(w_bf):
            return (jnp.dot(x_lo, w_bf[:half, :], preferred_element_type=F32)
                    + jnp.dot(x_hi, w_bf[half:, :], preferred_element_type=F32))

        g = up(wg_bf)
        u = up(wu_bf)
        hid = (g * _sigmoid(g) * u).astype(BF16)
        y_ref[...] = _pack_bf16_halves(jnp.dot(hid, wd_bf[...], preferred_element_type=F32))

    @pl.when(jnp.logical_not(used))
    def _():
        y_ref[...] = jnp.zeros_like(y_ref)


def _experts(block_expert, n_used, xs, wg, wu, wd):
    rows, dp = xs.shape
    d = wg.shape[1]
    assert dp * 2 == d
    de = wg.shape[2]
    nb = rows // EXPERT_BLOCK
    last = lambda i, nu: jnp.minimum(i, nu[0] - 1)
    grid_spec = pltpu.PrefetchScalarGridSpec(
        num_scalar_prefetch=2,
        grid=(nb,),
        in_specs=[pl.BlockSpec((EXPERT_BLOCK, dp), lambda i, be, nu: (last(i, nu), 0)),
                  pl.BlockSpec((1, d, de), lambda i, be, nu: (be[last(i, nu)], 0, 0)),
                  pl.BlockSpec((1, d, de), lambda i, be, nu: (be[last(i, nu)], 0, 0)),
                  pl.BlockSpec((1, de, d), lambda i, be, nu: (be[last(i, nu)], 0, 0))],
        out_specs=pl.BlockSpec((EXPERT_BLOCK, dp), lambda i, be, nu: (i, 0)),
        scratch_shapes=[pltpu.VMEM((d, de), BF16), pltpu.VMEM((d, de), BF16), pltpu.VMEM((de, d), BF16)],
    )
    return pl.pallas_call(
        _experts_kernel,
        grid_spec=grid_spec,
        out_shape=jax.ShapeDtypeStruct((rows, dp), jnp.uint32),
        compiler_params=_params(1),
        name="experts",
    )(block_expert, n_used, xs, wg, wu, wd)


def _combine_kernel(pos0_ref, pos1_ref, h_ref, wt_ref, gf_ref, y_hbm_ref, out_ref, buf_ref, sem):
    tm = h_ref.shape[0]

    for t in range(tm):
        _row_copy(y_hbm_ref, pos0_ref[t], buf_ref.at[0], t, sem).start(priority=0)
        _row_copy(y_hbm_ref, pos1_ref[t], buf_ref.at[1], t, sem).start(priority=1)
    for k in range(TOP_K):
        pltpu.make_async_copy(y_hbm_ref.at[pl.ds(0, tm)], buf_ref.at[k], sem).wait()
    wt = wt_ref[...]
    lo0, hi0 = _unpack_bf16_halves(buf_ref[0])
    lo1, hi1 = _unpack_bf16_halves(buf_ref[1])
    w0 = wt[:, 0:1]
    w1 = wt[:, 1:2]
    moe = jnp.concatenate([lo0 * w0 + lo1 * w1, hi0 * w0 + hi1 * w1], axis=1)
    out_ref[...] = _rmsnorm(h_ref[...] + moe, gf_ref[...])


def _combine(pos0, pos1, h, wt, gf, y_sorted, row0, n_rows):
    d = h.shape[1]
    dp = y_sorted.shape[1]
    assert row0 % ROW_TILE == 0 and n_rows % ROW_TILE == 0
    blk0 = row0 // ROW_TILE
    return pl.pallas_call(
        _combine_kernel,
        grid=(n_rows // ROW_TILE,),
        in_specs=[pl.BlockSpec((ROW_TILE,), lambda i: (blk0 + i,), memory_space=pltpu.SMEM),
                  pl.BlockSpec((ROW_TILE,), lambda i: (blk0 + i,), memory_space=pltpu.SMEM),
                  pl.BlockSpec((ROW_TILE, d), lambda i: (blk0 + i, 0)),
                  pl.BlockSpec((ROW_TILE, LANES), lambda i: (blk0 + i, 0)),
                  _const_spec((1, d)),
                  pl.BlockSpec(memory_space=pl.ANY)],
        out_specs=pl.BlockSpec((ROW_TILE, d), lambda i: (i, 0)),
        out_shape=jax.ShapeDtypeStruct((n_rows, d), F32),
        scratch_shapes=[pltpu.VMEM((TOP_K, ROW_TILE, dp), jnp.uint32), pltpu.SemaphoreType.DMA(())],
        compiler_params=_params(1),
        name="combine",
    )(pos0, pos1, h, wt, gf, y_sorted)


def _prompt_bias_base(rel_bias):
    m = jnp.arange(BIAS_BASE)
    dist = (Q_BLOCK - 1) - m + (KEY_BLOCKS - 1) * Q_BLOCK
    idx = jnp.clip(dist, -MAX_REL, MAX_REL) + MAX_REL
    return rel_bias[:, idx].astype(F32)


def _sample_bias(rel_bias, cache_len, t):
    qi = jnp.arange(t)[:, None]
    kj = jnp.arange(cache_len + t)[None, :]
    idx = jnp.clip(qi + cache_len - kj, -MAX_REL, MAX_REL) + MAX_REL
    bias = rel_bias[:, idx].astype(F32)
    bias = jnp.where((kj >= cache_len - BAND_PAST)[None], bias, NEG_INF)
    return bias.reshape(N_HEADS * t, cache_len + t)


def _block_diag(w):
    nb, bw, _ = w.shape
    eye = jnp.eye(nb, dtype=w.dtype)
    return jnp.einsum("ncd,nm->ncmd", w, eye).reshape(nb * bw, nb * bw)


def kernel(x_prompt, x_sample, cache_k, cache_v, state_conv, state_h, norm1_g, w_in, b_in, rel_bias, w_att_branch, conv_w, conv_b, w_rec_gate, b_rec_gate, w_in_gate, b_in_gate, lru_lambda, w_rnn_branch, w_out, norm2_g, w_router_group, b_router_group, w_router_expert, b_router_expert, w_e_gate, w_e_up, w_e_down, final_norm_g):
    batch, seq, d = x_prompt.shape
    dec_b, dec_t, _ = x_sample.shape
    depth = w_in.shape[0]
    assert depth == 1
    att_w = N_HEADS * HEAD_DIM
    rnn_w = conv_w.shape[2]
    cache_len = cache_k.shape[2]
    n_p = batch * seq
    n_s = dec_b * dec_t
    n = n_p + n_s
    keep = min(BAND_PAST + CHUNK, seq)
    row2 = lambda a: a.reshape(1, -1).astype(F32)

    tiles_per_seq = seq // ROW_TILE
    tail_tiles = -(-keep // ROW_TILE)
    state_tiles = tuple(b * tiles_per_seq + tiles_per_seq - tail_tiles + j
                        for b in range(batch) for j in range(tail_tiles)) + (n_p // ROW_TILE,)

    xp = x_prompt.reshape(n_p, d)
    xs = x_sample.reshape(n_s, d)
    q, k, v, kvt, rx, rgate, ga, gr = _proj(xp, xs, row2(norm1_g[0]), w_in[0].astype(BF16), row2(b_in[0]),
                                            att_w, rnn_w, state_tiles)

    att_p = _attn_prompt(q, k, v, _prompt_bias_base(rel_bias[0]), batch, seq)
    att_s, new_k_s, new_v_s = _attn_sample(
        q, k, v, kvt, cache_k[0].reshape(dec_b, cache_len, att_w), cache_v[0].reshape(dec_b, cache_len, att_w),
        _sample_bias(rel_bias[0], cache_len, dec_t), n_p, batch * tail_tiles * ROW_TILE, dec_b, dec_t)

    rnn_args = (conv_w[0].astype(F32), row2(conv_b[0]),
                _block_diag(w_rec_gate[0]).astype(BF16), row2(b_rec_gate[0]),
                _block_diag(w_in_gate[0]).astype(BF16), row2(b_in_gate[0]), row2(lru_lambda[0]))
    pad_state = lambda s: jnp.pad(s.astype(F32), ((0, 0), (SUBLANES - (CONV_W - 1), 0), (0, 0)))
    rnn_p, conv_p, h_p = _rnn_prompt(rx, rgate, *rnn_args, batch, seq, ROW_TILE)
    rnn_s, conv_s, h_s = _rnn(rx, rgate, pad_state(state_conv[0]), state_h[0].astype(F32)[:, None, :],
                              *rnn_args, n_p, dec_b, dec_t, dec_t)

    w_r = jnp.zeros((d, LANES), F32).at[:, :N_GROUPS].set(w_router_group[0])
    w_r = w_r.at[:, N_GROUPS:N_GROUPS + N_EXPERTS].set(w_router_expert[0]).astype(BF16)
    b_r = jnp.zeros((1, LANES), F32).at[0, :N_GROUPS].set(b_router_group[0])
    b_r = b_r.at[0, N_GROUPS:N_GROUPS + N_EXPERTS].set(b_router_expert[0])
    h, hn, route, wt, cnt = _merge(xp, xs, att_p, att_s, rnn_p, rnn_s, ga, gr,
                                   w_att_branch[0].astype(BF16), w_rnn_branch[0].astype(BF16), w_out[0].astype(BF16),
                                   row2(norm2_g[0]), w_r, b_r)

    counts = cnt[0, :N_EXPERTS].astype(jnp.int32)
    padded = (counts + EXPERT_BLOCK - 1) // EXPERT_BLOCK * EXPERT_BLOCK
    pad_end = jnp.cumsum(padded)
    pad_start = pad_end - padded
    expert = route[0:TOP_K].astype(jnp.int32)
    rank = route[TOP_K:2 * TOP_K].astype(jnp.int32)
    ids = jnp.arange(N_EXPERTS, dtype=jnp.int32)[:, None, None]
    pos = jnp.sum(jnp.where(expert[None] == ids, pad_start[:, None, None], 0), axis=0) + rank
    n_blocks = -(-(n * TOP_K) // EXPERT_BLOCK) + N_EXPERTS
    rows = n_blocks * EXPERT_BLOCK
    block_start = jnp.arange(n_blocks, dtype=jnp.int32) * EXPERT_BLOCK
    block_expert = jnp.minimum(jnp.sum((block_start[:, None] >= pad_end[None, :]).astype(jnp.int32), axis=1),
                               N_EXPERTS - 1)
    n_used = (pad_end[-1:] // EXPERT_BLOCK).astype(jnp.int32)

    x_sorted = _dispatch(pad_end, padded, pos[0], pos[1], hn, rows)
    y_sorted = _experts(block_expert, n_used, x_sorted, w_e_gate[0], w_e_up[0], w_e_down[0])
    gf = row2(final_norm_g)
    y_p = _combine(pos[0], pos[1], h, wt, gf, y_sorted, 0, n_p)
    y_s = _combine(pos[0], pos[1], h, wt, gf, y_sorted, n_p, n_s)

    def kv_state(col0):
        span = tail_tiles * ROW_TILE
        tails = [kvt[b * span + span - keep:(b + 1) * span, col0:col0 + att_w] for b in range(batch)]
        return jnp.stack(tails).reshape(1, batch, keep, N_HEADS, HEAD_DIM)

    heads = lambda a: a.reshape(1, dec_b, cache_len, N_HEADS, HEAD_DIM)
    return (y_p.reshape(batch, seq, d), y_s.reshape(dec_b, dec_t, d),
            kv_state(0), kv_state(att_w),
            conv_p[:, SUBLANES - (CONV_W - 1):][None], h_p[:, 0][None],
            heads(new_k_s), heads(new_v_s),
            conv_s[:, SUBLANES - (CONV_W - 1):][None], h_s[:, 0][None])
```

```python
import functools

import jax
import jax.numpy as jnp
from jax import lax
from jax.experimental import pallas as pl
from jax.experimental.pallas import tpu as pltpu

CHUNK = 64
LEFT_CHUNKS = 8
BAND_PAST = LEFT_CHUNKS * CHUNK
N_HEADS = 8
HEAD_DIM = 64
MAX_REL = 256
NEG_INF = -1e30
RNN_BLOCKS = 8
CONV_W = 4
RG_C = 8.0
N_GROUPS = 4
EXPERTS_PER_GROUP = 8
N_EXPERTS = N_GROUPS * EXPERTS_PER_GROUP
TOP_K = 2
EPS = 1e-6

LANES = 128
SUBLANES = 8
ROW_TILE = 512
Q_BLOCK = 256
KEY_BLOCKS = 3
KEY_WINDOW = KEY_BLOCKS * Q_BLOCK
BIAS_BASE = 1024
EXPERT_BLOCK = 512
VMEM_LIMIT = 56 * 1024 * 1024

BF16 = jnp.bfloat16
F32 = jnp.float32


def _params(n_axes, vmem=VMEM_LIMIT):
    return pltpu.CompilerParams(dimension_semantics=("arbitrary",) * n_axes, vmem_limit_bytes=vmem)


def _const_spec(shape):
    zeros = (0,) * len(shape)
    return pl.BlockSpec(shape, lambda *_: zeros, pipeline_mode=pl.Buffered(1))


def _rmsnorm(x, g):
    return x * lax.rsqrt(jnp.mean(x * x, axis=-1, keepdims=True) + EPS) * g


def _pack_bf16_halves(x):
    c = x.shape[1] // 2
    lo = lax.bitcast_convert_type(x[:, :c].astype(BF16).astype(F32), jnp.uint32)
    hi = lax.bitcast_convert_type(x[:, c:].astype(BF16).astype(F32), jnp.uint32)
    return (lo >> 16) | (hi & jnp.uint32(0xFFFF0000))


def _unpack_bf16_halves(p):
    lo = lax.bitcast_convert_type(p << 16, F32)
    hi = lax.bitcast_convert_type(p & jnp.uint32(0xFFFF0000), F32)
    return lo, hi


def _sigmoid(x):
    return 0.5 * jnp.tanh(0.5 * x) + 0.5


def _proj_kernel(n_prompt_tiles, tiles_per_seq, state_tiles, att_w, rnn_w, d_model,
                 xp_ref, xs_ref, g_ref, w_ref, b_ref, cw_ref, cb_ref, wa_ref, ba_ref, wx_ref, bx_ref, lam_ref,
                 q_ref, k_ref, v_ref, kvt_ref, ga_ref, gr_ref, rnn_ref, conv_out_ref, h_out_ref, rxs_ref, rgs_ref,
                 rx_ref, tail_ref, hl_ref, p_ref, hs_ref, h_ref):
    i = pl.program_id(0)
    tm = xp_ref.shape[0]
    x = jnp.where(i < n_prompt_tiles, xp_ref[...], xs_ref[...])
    xn = _rmsnorm(x, g_ref[...]).astype(BF16)

    def seg(lo, width):
        return jnp.dot(xn, w_ref[:, lo:lo + width], preferred_element_type=F32) + b_ref[:, lo:lo + width]

    q = seg(0, att_w)
    k = seg(att_w, att_w)
    v = seg(2 * att_w, att_w)
    q_ref[...] = (q * (HEAD_DIM ** -0.5)).astype(BF16)
    k_ref[...] = k.astype(BF16)
    v_ref[...] = v.astype(BF16)

    @pl.when(functools.reduce(jnp.logical_or, [i == t for t in state_tiles]))
    def _():
        kvt_ref[:, :att_w] = k
        kvt_ref[:, att_w:] = v

    o = 3 * att_w
    ncol = rnn_w // LANES
    rx = seg(o, rnn_w)
    for c in range(ncol):
        rx_ref[c] = rx[:, c * LANES:(c + 1) * LANES]
    rg = seg(o + rnn_w, rnn_w)

    @pl.when(i % tiles_per_seq == 0)
    def _():
        tail_ref[...] = jnp.zeros_like(tail_ref)
        h_ref[...] = jnp.zeros_like(h_ref)

    hseq, tail, h_last = _rglru_prompt_tile(tm, rx_ref, cw_ref, cb_ref, wa_ref, ba_ref, wx_ref, bx_ref, lam_ref,
                                            tail_ref, hl_ref, p_ref, hs_ref, h_ref)
    rnn_ref[...] = (hseq * jax.nn.gelu(rg)).astype(BF16)
    ga_ref[...] = seg(o + 2 * rnn_w, d_model).astype(BF16)
    gr_ref[...] = seg(o + 2 * rnn_w + d_model, d_model).astype(BF16)

    @pl.when(i < n_prompt_tiles)
    def _():
        conv_out_ref[0] = tail
        h_out_ref[0] = h_last

    @pl.when(i == n_prompt_tiles)
    def _():
        for c in range(ncol):
            rxs_ref[c] = rx_ref[c]
        rgs_ref[...] = rg.astype(BF16)


def _proj(xp, xs, g1, w_in, b_in, rnn_args, att_w, rnn_w, state_tiles, batch):
    n_p, d = xp.shape
    n_s = xs.shape[0]
    assert n_p % ROW_TILE == 0 and n_s == ROW_TILE
    assert ROW_TILE % (SUBLANES * SUBLANES) == 0 and ROW_TILE // SUBLANES >= CONV_W
    n = n_p + n_s
    npt = n_p // ROW_TILE
    tps = npt // batch
    pw = w_in.shape[1]
    ncol = rnn_w // LANES
    row = lambda w: pl.BlockSpec((ROW_TILE, w), lambda i: (i, 0))
    rows = lambda w, dt: jax.ShapeDtypeStruct((n, w), dt)
    slot = lambda i: sum(jnp.where(i > t, 1, 0) for t in state_tiles)
    state = lambda r: pl.BlockSpec((1, r, rnn_w), lambda i: (jnp.minimum(i // tps, batch - 1), 0, 0))
    out_specs = [row(att_w), row(att_w), row(att_w),
                 pl.BlockSpec((ROW_TILE, 2 * att_w), lambda i: (slot(i), 0)),
                 row(d), row(d), row(rnn_w), state(SUBLANES), state(1),
                 pl.BlockSpec((ncol, ROW_TILE, LANES), lambda i: (0, 0, 0)),
                 pl.BlockSpec((ROW_TILE, rnn_w), lambda i: (0, 0))]
    out_shape = [rows(att_w, BF16), rows(att_w, BF16), rows(att_w, BF16),
                 jax.ShapeDtypeStruct((len(state_tiles) * ROW_TILE, 2 * att_w), F32),
                 rows(d, BF16), rows(d, BF16), rows(rnn_w, BF16),
                 jax.ShapeDtypeStruct((batch, SUBLANES, rnn_w), F32), jax.ShapeDtypeStruct((batch, 1, rnn_w), F32),
                 jax.ShapeDtypeStruct((ncol, n_s, LANES), F32), jax.ShapeDtypeStruct((n_s, rnn_w), BF16)]
    tile = lambda: pltpu.VMEM((ROW_TILE, rnn_w), F32)
    cols = lambda: pltpu.VMEM((ncol, ROW_TILE, LANES), F32)
    return pl.pallas_call(
        functools.partial(_proj_kernel, npt, tps, state_tiles, att_w, rnn_w, d),
        grid=(n // ROW_TILE,),
        in_specs=[pl.BlockSpec((ROW_TILE, d), lambda i: (jnp.minimum(i, npt - 1), 0)),
                  pl.BlockSpec((ROW_TILE, d), lambda i: (0, 0)),
                  _const_spec((1, d)), _const_spec((d, pw)), _const_spec((1, pw))]
                 + [_const_spec(a.shape) for a in rnn_args],
        out_specs=out_specs,
        out_shape=out_shape,
        scratch_shapes=[cols(), pltpu.VMEM((SUBLANES, rnn_w), F32), tile(), tile(), cols(),
                        pltpu.VMEM((1, rnn_w), F32)],
        compiler_params=_params(1),
        name="proj_rglru",
    )(xp, xs, g1, w_in, b_in, *rnn_args)


def _attn_p_kernel(q_ref, k0_ref, k1_ref, k2_ref, v0_ref, v1_ref, v2_ref, base_ref, o_ref, bias_ref):
    b = pl.program_id(0)
    i = pl.program_id(1)

    @pl.when((b == 0) & (i == 0))
    def _():
        r = lax.broadcasted_iota(jnp.int32, (Q_BLOCK, KEY_WINDOW), 0) // CHUNK
        c = lax.broadcasted_iota(jnp.int32, (Q_BLOCK, KEY_WINDOW), 1) // CHUNK
        in_band = (c >= r) & (c <= r + LEFT_CHUNKS)
        for h in range(N_HEADS):
            rows = jnp.broadcast_to(base_ref[h:h + 1, :], (Q_BLOCK, BIAS_BASE))
            toeplitz = pltpu.roll(rows, BIAS_BASE - (Q_BLOCK - 1), axis=1, stride=1, stride_axis=0)
            bias_ref[h] = jnp.where(in_band, toeplitz[:, :KEY_WINDOW], NEG_INF)

    kwin = jnp.concatenate([k0_ref[...], k1_ref[...], k2_ref[...]], axis=0)
    vwin = jnp.concatenate([v0_ref[...], v1_ref[...], v2_ref[...]], axis=0)
    lane = lax.broadcasted_iota(jnp.int32, (1, LANES), 1)

    def heads(at_sequence_start):
        if at_sequence_start:
            col = lax.broadcasted_iota(jnp.int32, (1, KEY_WINDOW), 1)
            start_mask = jnp.where(col // Q_BLOCK + i - (KEY_BLOCKS - 1) >= 0, 0.0, NEG_INF).astype(F32)
        for pair in range(N_HEADS // 2):
            sl = slice(pair * LANES, (pair + 1) * LANES)
            q2 = q_ref[:, sl]
            k2 = kwin[:, sl]
            v2 = vwin[:, sl]
            acc = jnp.zeros((Q_BLOCK, LANES), F32)
            for half in range(2):
                hmask = (lane // HEAD_DIM) == half
                qm = jnp.where(hmask, q2, jnp.zeros_like(q2))
                vm = jnp.where(hmask, v2, jnp.zeros_like(v2))
                s = lax.dot_general(qm, k2, (((1,), (1,)), ((), ())), preferred_element_type=F32)
                s = s + bias_ref[2 * pair + half]
                if at_sequence_start:
                    s = s + start_mask
                m = jnp.max(s, axis=-1, keepdims=True)
                e = jnp.exp(s - m)
                l = jnp.sum(e, axis=-1, keepdims=True)
                o = jnp.dot(e.astype(BF16), vm, preferred_element_type=F32)
                acc = acc + o / l
            o_ref[:, sl] = acc.astype(BF16)

    pl.when(i < KEY_BLOCKS - 1)(lambda: heads(True))
    pl.when(i >= KEY_BLOCKS - 1)(lambda: heads(False))


def _attn_prompt(q, k, v, base, batch, seq):
    w = q.shape[1]
    nqb = seq // Q_BLOCK
    kspec = lambda back: pl.BlockSpec((Q_BLOCK, w), lambda b, i: (b * nqb + jnp.maximum(i - back, 0), 0))
    return pl.pallas_call(
        _attn_p_kernel,
        grid=(batch, nqb),
        in_specs=[pl.BlockSpec((Q_BLOCK, w), lambda b, i: (b * nqb + i, 0)),
                  kspec(2), kspec(1), kspec(0), kspec(2), kspec(1), kspec(0),
                  _const_spec(base.shape)],
        out_specs=pl.BlockSpec((Q_BLOCK, w), lambda b, i: (b * nqb + i, 0)),
        out_shape=jax.ShapeDtypeStruct((batch * seq, w), BF16),
        scratch_shapes=[pltpu.VMEM((N_HEADS, Q_BLOCK, KEY_WINDOW), F32)],
        compiler_params=_params(2),
        name="attn_prompt",
    )(q, k, k, k, v, v, v, base)


def _attn_s_kernel(t, q_ref, k_ref, v_ref, kvt_ref, ck_ref, cv_ref, bias_ref, o_ref, nk_ref, nv_ref):
    cache_len = ck_ref.shape[1]
    w = q_ref.shape[1]
    ck = ck_ref[0]
    cv = cv_ref[0]
    kall = jnp.concatenate([ck.astype(BF16), k_ref[...]], axis=0)
    vall = jnp.concatenate([cv.astype(BF16), v_ref[...]], axis=0)
    q = q_ref[...]
    head_of_lane = lax.broadcasted_iota(jnp.int32, (1, w), 1) // HEAD_DIM
    qs = jnp.concatenate([jnp.where(head_of_lane == h, q, jnp.zeros_like(q)) for h in range(N_HEADS)], axis=0)
    s = lax.dot_general(qs, kall, (((1,), (1,)), ((), ())), preferred_element_type=F32) + bias_ref[...]
    m = jnp.max(s, axis=-1, keepdims=True)
    e = jnp.exp(s - m)
    l = jnp.sum(e, axis=-1, keepdims=True)
    o_all = jnp.dot(e.astype(BF16), vall, preferred_element_type=F32) / l
    o = jnp.zeros((t, w), F32)
    for h in range(N_HEADS):
        o = o + jnp.where(head_of_lane == h, o_all[h * t:(h + 1) * t], 0.0)
    o_ref[...] = o.astype(BF16)
    nk_ref[0, :cache_len - t, :] = ck[t:]
    nk_ref[0, cache_len - t:, :] = kvt_ref[:, :w]
    nv_ref[0, :cache_len - t, :] = cv[t:]
    nv_ref[0, cache_len - t:, :] = kvt_ref[:, w:]


def _attn_sample(q, k, v, kvt, cache_k, cache_v, bias, row0, kvt_row0, dec_b, t):
    w = q.shape[1]
    cache_len = cache_k.shape[1]
    assert row0 % t == 0 and kvt_row0 % t == 0
    tok = lambda width, r0: pl.BlockSpec((t, width), lambda b: (r0 // t + b, 0))
    cache = pl.BlockSpec((1, cache_len, w), lambda b: (b, 0, 0))
    return pl.pallas_call(
        functools.partial(_attn_s_kernel, t),
        grid=(dec_b,),
        in_specs=[tok(w, row0), tok(w, row0), tok(w, row0), tok(2 * w, kvt_row0), cache, cache,
                  _const_spec(bias.shape)],
        out_specs=[pl.BlockSpec((t, w), lambda b: (b, 0)), cache, cache],
        out_shape=[jax.ShapeDtypeStruct((dec_b * t, w), BF16),
                   jax.ShapeDtypeStruct(cache_k.shape, F32), jax.ShapeDtypeStruct(cache_v.shape, F32)],
        compiler_params=_params(1),
        name="attn_sample",
    )(q, k, v, kvt, cache_k, cache_v, bias)


def _rnn_kernel(tm, rx_ref, rg_ref, conv0_ref, h0_ref, cw_ref, cb_ref, wa_ref, ba_ref, wx_ref, bx_ref, lam_ref,
                rnn_ref, conv_out_ref, h_out_ref, ext_ref, a_ref, b_ref, hs_ref, h_ref):
    j = pl.program_id(1)

    @pl.when(j == 0)
    def _():
        ext_ref[:SUBLANES, :] = conv0_ref[0]
        h_ref[...] = h0_ref[0]

    ext_ref[SUBLANES:, :] = jnp.concatenate([rx_ref[c] for c in range(rx_ref.shape[0])], axis=1)
    cw = cw_ref[...]
    xc = cb_ref[...]
    for tap in range(CONV_W):
        off = SUBLANES - (CONV_W - 1) + tap
        xc = xc + ext_ref[off:off + tm, :] * cw[tap:tap + 1, :]
    tail = ext_ref[tm:tm + SUBLANES, :]
    ext_ref[:SUBLANES, :] = tail
    conv_out_ref[0] = tail

    a, b = _rglru_coeffs(xc, wa_ref, ba_ref, wx_ref, bx_ref, lam_ref)

    sub = lax.broadcasted_iota(jnp.int32, a.shape, 0) % SUBLANES
    shift = 1
    while shift < SUBLANES:
        keep = sub >= shift
        a_prev = pltpu.roll(a, shift, axis=0)
        b_prev = pltpu.roll(b, shift, axis=0)
        b = jnp.where(keep, a * b_prev + b, b)
        a = jnp.where(keep, a * a_prev, a)
        shift *= 2
    a_ref[...] = a
    b_ref[...] = b

    def group(g, h):
        r0 = pl.multiple_of(g * SUBLANES, SUBLANES)
        hg = b_ref[pl.ds(r0, SUBLANES), :] + a_ref[pl.ds(r0, SUBLANES), :] * h
        hs_ref[pl.ds(r0, SUBLANES), :] = hg
        return hg[SUBLANES - 1:SUBLANES, :]

    h_last = lax.fori_loop(0, tm // SUBLANES, group, h_ref[...])
    h_ref[...] = h_last
    h_out_ref[0] = h_last
    rnn_ref[...] = (hs_ref[...] * jax.nn.gelu(rg_ref[...].astype(F32))).astype(BF16)


def _rnn(rx, rgate, conv0, h0, cw, cb, wa, ba, wx, bx, lam, row0, batch, seq, tm):
    w = rgate.shape[1]
    assert row0 % tm == 0 and seq % tm == 0 and tm % SUBLANES == 0
    blk0 = row0 // tm
    nt = seq // tm
    tok = pl.BlockSpec((tm, w), lambda b, j: (blk0 + b * nt + j, 0))
    tok_cols = pl.BlockSpec((w // LANES, tm, LANES), lambda b, j: (0, blk0 + b * nt + j, 0))
    state = lambda rows: pl.BlockSpec((1, rows, w), lambda b, j: (b, 0, 0))
    return pl.pallas_call(
        functools.partial(_rnn_kernel, tm),
        grid=(batch, nt),
        in_specs=[tok_cols, tok, state(SUBLANES), state(1),
                  _const_spec(cw.shape), _const_spec((1, w)), _const_spec((w, w)), _const_spec((1, w)),
                  _const_spec((w, w)), _const_spec((1, w)), _const_spec((1, w))],
        out_specs=[pl.BlockSpec((tm, w), lambda b, j: (b * nt + j, 0)), state(SUBLANES), state(1)],
        out_shape=[jax.ShapeDtypeStruct((batch * seq, w), BF16),
                   jax.ShapeDtypeStruct((batch, SUBLANES, w), F32),
                   jax.ShapeDtypeStruct((batch, 1, w), F32)],
        scratch_shapes=[pltpu.VMEM((tm + SUBLANES, w), F32), pltpu.VMEM((tm, w), F32), pltpu.VMEM((tm, w), F32),
                        pltpu.VMEM((tm, w), F32), pltpu.VMEM((1, w), F32)],
        compiler_params=_params(2),
        name="rnn",
    )(rx, rgate, conv0, h0, cw, cb, wa, ba, wx, bx, lam)


def _rglru_coeffs(xc, wa_ref, ba_ref, wx_ref, bx_ref, lam_ref):
    xb = xc.astype(BF16)
    r = _sigmoid(jnp.dot(xb, wa_ref[...], preferred_element_type=F32) + ba_ref[...])
    gi = _sigmoid(jnp.dot(xb, wx_ref[...], preferred_element_type=F32) + bx_ref[...])
    neg_lam = -lam_ref[...]
    softplus = jnp.maximum(neg_lam, 0.0) + jnp.log1p(jnp.exp(-jnp.abs(neg_lam)))
    log_a = (-RG_C) * r * softplus
    a = jnp.exp(log_a)
    b = jnp.sqrt(-jnp.tanh(log_a) * (a * a + 1.0)) * (gi * xc)
    return a, b


def _rglru_prompt_tile(tm, rx_ref, cw_ref, cb_ref, wa_ref, ba_ref, wx_ref, bx_ref, lam_ref,
                       tail_ref, hl_ref, p_ref, hs_ref, h_ref):
    seg = tm // SUBLANES
    ncol = rx_ref.shape[0]
    w = ncol * LANES

    def strided_rows(ref, j):
        return jnp.concatenate([ref[c, pl.ds(j, SUBLANES, stride=seg), :] for c in range(ncol)], axis=1)

    slab = [strided_rows(rx_ref, j) for j in range(seg)]
    sub = lax.broadcasted_iota(jnp.int32, (SUBLANES, w), 0)

    def before(k):
        carry = jnp.broadcast_to(tail_ref[SUBLANES - k:SUBLANES - k + 1, :], (SUBLANES, w))
        return jnp.where(sub == 0, carry, pltpu.roll(slab[seg - k], 1, axis=0))

    lead = {k: before(k) for k in range(1, CONV_W)}
    cw = cw_ref[...]
    cb = cb_ref[...]
    xc = []
    for j in range(seg):
        acc = cb + slab[j] * cw[CONV_W - 1:CONV_W, :]
        for k in range(1, CONV_W):
            prev = slab[j - k] if j >= k else lead[k - j]
            acc = acc + prev * cw[CONV_W - 1 - k:CONV_W - k, :]
        xc.append(acc)
    tail = jnp.concatenate([rx_ref[c, tm - SUBLANES:tm, :] for c in range(ncol)], axis=1)
    tail_ref[...] = tail

    a, b = _rglru_coeffs(jnp.concatenate(xc, axis=0), wa_ref, ba_ref, wx_ref, bx_ref, lam_ref)

    h = jnp.zeros((SUBLANES, w), F32)
    p = jnp.ones((SUBLANES, w), F32)
    for j in range(seg):
        rows = slice(j * SUBLANES, (j + 1) * SUBLANES)
        h = a[rows] * h + b[rows]
        p = a[rows] * p
        hl_ref[rows, :] = h
        p_ref[rows, :] = p
    c = h_ref[...]
    enter = []
    for s in range(SUBLANES):
        enter.append(c)
        c = h[s:s + 1, :] + p[s:s + 1, :] * c
    last = c
    h_ref[...] = last
    enter = jnp.concatenate(enter, axis=0)
    for j in range(seg):
        rows = slice(j * SUBLANES, (j + 1) * SUBLANES)
        hj = hl_ref[rows, :] + p_ref[rows, :] * enter
        for col in range(ncol):
            hs_ref[col, pl.ds(j, SUBLANES, stride=seg), :] = hj[:, col * LANES:(col + 1) * LANES]
    hseq = jnp.concatenate([hs_ref[col] for col in range(ncol)], axis=1)
    return hseq, tail, last


ROUTE_ROWS = SUBLANES


def _merge_kernel(n_prompt_tiles,
                  xp_ref, xs_ref, attp_ref, atts_ref, rnnp_ref, rnns_ref, ga_ref, gr_ref,
                  watt_ref, wrnn_ref, wout_ref, g2_ref, wr_ref, br_ref,
                  h_ref, hn_ref, route_ref, wt_ref, cnt_ref, carry_ref):
    i = pl.program_id(0)
    is_p = i < n_prompt_tiles

    @pl.when(i == 0)
    def _():
        carry_ref[...] = jnp.zeros_like(carry_ref)

    x = jnp.where(is_p, xp_ref[...], xs_ref[...])
    att = jnp.where(is_p, attp_ref[...], atts_ref[...])
    rnn = jnp.where(is_p, rnnp_ref[...], rnns_ref[...])
    ya = jnp.dot(att, watt_ref[...], preferred_element_type=F32)
    yr = jnp.dot(rnn, wrnn_ref[...], preferred_element_type=F32)
    mixed = _sigmoid(ga_ref[...].astype(F32)) * ya + _sigmoid(gr_ref[...].astype(F32)) * yr
    h = x + jnp.dot(mixed.astype(BF16), wout_ref[...], preferred_element_type=F32)
    h_ref[...] = h
    hn = _rmsnorm(h, g2_ref[...])
    hn_ref[...] = _pack_bf16_halves(hn)

    logits = jnp.dot(hn.astype(BF16), wr_ref[...], preferred_element_type=F32) + br_ref[...]
    tm = logits.shape[0]
    lane = lax.broadcasted_iota(jnp.int32, (tm, LANES), 1)
    low = jnp.float32(-3e38)
    big = jnp.int32(1 << 20)
    lg = jnp.where(lane < N_GROUPS, logits, low)
    mg = jnp.max(lg, axis=-1, keepdims=True)
    p_grp = 1.0 / jnp.sum(jnp.exp(lg - mg), axis=-1, keepdims=True)
    grp = jnp.min(jnp.where(lg == mg, lane, big), axis=-1, keepdims=True)
    lo = N_GROUPS + EXPERTS_PER_GROUP * grp
    le = jnp.where((lane >= lo) & (lane < lo + EXPERTS_PER_GROUP), logits, low)
    m1 = jnp.max(le, axis=-1, keepdims=True)
    i1 = jnp.min(jnp.where(le == m1, lane, big), axis=-1, keepdims=True)
    le2 = jnp.where(lane == i1, low, le)
    m2 = jnp.max(le2, axis=-1, keepdims=True)
    i2 = jnp.min(jnp.where(le2 == m2, lane, big), axis=-1, keepdims=True)
    e2 = jnp.exp(m2 - m1)
    inv = 1.0 / (1.0 + e2)
    w1 = p_grp * inv
    w2 = p_grp * (e2 * inv)
    x1 = i1 - N_GROUPS
    x2 = i2 - N_GROUPS

    onehot = jnp.where((lane == x1) | (lane == x2), 1.0, 0.0).astype(BF16)
    rowi = lax.broadcasted_iota(jnp.int32, (tm, tm), 0)
    coli = lax.broadcasted_iota(jnp.int32, (tm, tm), 1)
    tri = jnp.where(coli < rowi, 1.0, 0.0).astype(BF16)
    before = jnp.dot(tri, onehot, preferred_element_type=F32) + carry_ref[...]
    r1 = jnp.sum(jnp.where(lane == x1, before, 0.0), axis=-1, keepdims=True)
    r2 = jnp.sum(jnp.where(lane == x2, before, 0.0), axis=-1, keepdims=True)
    carry = carry_ref[...] + jnp.sum(onehot.astype(F32), axis=0, keepdims=True)
    carry_ref[...] = carry
    cnt_ref[...] = carry

    fields = (x1.astype(F32), x2.astype(F32), r1, r2, w1, w2)
    slab = jnp.zeros((tm, LANES), F32)
    for idx, val in enumerate(fields):
        slab = jnp.where(lane == idx, val, slab)
    route_ref[...] = slab.T[:ROUTE_ROWS, :]
    wt_ref[...] = jnp.where(lane == 0, w1, jnp.where(lane == 1, w2, 0.0))


def _merge(xp, xs, att_p, att_s, rnn_p, rnn_s, ga, gr, w_att, w_rnn, w_out, g2, w_r, b_r):
    n_p, d = xp.shape
    n = ga.shape[0]
    npt = n_p // ROW_TILE
    aw = att_p.shape[1]
    rw = rnn_p.shape[1]
    pspec = lambda w: pl.BlockSpec((ROW_TILE, w), lambda i: (jnp.minimum(i, npt - 1), 0))
    sspec = lambda w: pl.BlockSpec((ROW_TILE, w), lambda i: (0, 0))
    row = lambda w: pl.BlockSpec((ROW_TILE, w), lambda i: (i, 0))
    return pl.pallas_call(
        functools.partial(_merge_kernel, npt),
        grid=(n // ROW_TILE,),
        in_specs=[pspec(d), sspec(d), pspec(aw), sspec(aw), pspec(rw), sspec(rw), row(d), row(d),
                  _const_spec(w_att.shape), _const_spec(w_rnn.shape), _const_spec(w_out.shape),
                  _const_spec((1, d)), _const_spec(w_r.shape), _const_spec((1, LANES))],
        out_specs=[row(d), row(d // 2), pl.BlockSpec((ROUTE_ROWS, ROW_TILE), lambda i: (0, i)), row(LANES),
                   pl.BlockSpec((1, LANES), lambda i: (0, 0))],
        out_shape=[jax.ShapeDtypeStruct((n, d), F32), jax.ShapeDtypeStruct((n, d // 2), jnp.uint32),
                   jax.ShapeDtypeStruct((ROUTE_ROWS, n), F32), jax.ShapeDtypeStruct((n, LANES), F32),
                   jax.ShapeDtypeStruct((1, LANES), F32)],
        scratch_shapes=[pltpu.VMEM((1, LANES), F32)],
        compiler_params=_params(1),
        name="merge",
    )(xp, xs, att_p, att_s, rnn_p, rnn_s, ga, gr, w_att, w_rnn, w_out, g2, w_r, b_r)


def _row_copy(src_ref, src_row, dst_ref, dst_row, sem):
    return pltpu.make_async_copy(src_ref.at[pl.ds(src_row, 1)], dst_ref.at[pl.ds(dst_row, 1)], sem)


def _dispatch_kernel(pad_end_ref, padded_ref, pos0_ref, pos1_ref, hn_ref, xs_ref, zero_ref, sem):
    tm = hn_ref.shape[0]

    @pl.when(pl.program_id(0) == 0)
    def _():
        zero_ref[...] = jnp.zeros_like(zero_ref)

        def zero_block(start):
            start = pl.multiple_of(start, EXPERT_BLOCK)
            return pltpu.make_async_copy(zero_ref, xs_ref.at[pl.ds(start, EXPERT_BLOCK)], sem)

        used_rows = pad_end_ref[N_EXPERTS - 1]
        total_rows = xs_ref.shape[0]
        for wait in (False, True):
            for e in range(N_EXPERTS):
                @pl.when(padded_ref[e] > 0)
                def _():
                    cp = zero_block(pad_end_ref[e] - EXPERT_BLOCK)
                    cp.wait() if wait else cp.start()

                @pl.when(used_rows + e * EXPERT_BLOCK < total_rows)
                def _():
                    cp = zero_block(used_rows + e * EXPERT_BLOCK)
                    cp.wait() if wait else cp.start()

    for t in range(tm):
        _row_copy(hn_ref, t, xs_ref, pos0_ref[t], sem).start(priority=0)
        _row_copy(hn_ref, t, xs_ref, pos1_ref[t], sem).start(priority=1)
    for _ in range(TOP_K):
        pltpu.make_async_copy(hn_ref, xs_ref.at[pl.ds(0, tm)], sem).wait()


def _dispatch(pad_end, padded, pos0, pos1, hn, rows):
    n, d = hn.shape
    grid_spec = pltpu.PrefetchScalarGridSpec(
        num_scalar_prefetch=2,
        grid=(n // ROW_TILE,),
        in_specs=[pl.BlockSpec((ROW_TILE,), lambda i, pe, pd: (i,), memory_space=pltpu.SMEM),
                  pl.BlockSpec((ROW_TILE,), lambda i, pe, pd: (i,), memory_space=pltpu.SMEM),
                  pl.BlockSpec((ROW_TILE, d), lambda i, pe, pd: (i, 0))],
        out_specs=pl.BlockSpec(memory_space=pl.ANY),
        scratch_shapes=[pltpu.VMEM((EXPERT_BLOCK, d), hn.dtype), pltpu.SemaphoreType.DMA(())],
    )
    return pl.pallas_call(
        _dispatch_kernel,
        grid_spec=grid_spec,
        out_shape=jax.ShapeDtypeStruct((rows, d), hn.dtype),
        compiler_params=_params(1),
        name="dispatch",
    )(pad_end, padded, pos0, pos1, hn)


def _experts_kernel(be_ref, nu_ref, x_ref, wg_ref, wu_ref, wd_ref, y_ref, wg_bf, wu_bf, wd_bf):
    i = pl.program_id(0)
    used = i < nu_ref[0]

    @pl.when(used & ((i == 0) | (be_ref[i] != be_ref[jnp.maximum(i - 1, 0)])))
    def _():
        wg_bf[...] = wg_ref[0].astype(BF16)
        wu_bf[...] = wu_ref[0].astype(BF16)
        wd_bf[...] = wd_ref[0].astype(BF16)

    @pl.when(used)
    def _():
        half = wg_bf.shape[0] // 2
        x_lo, x_hi = (part.astype(BF16) for part in _unpack_bf16_halves(x_ref[...]))

        def up(w_bf):
            return (jnp.dot(x_lo, w_bf[:half, :], preferred_element_type=F32)
                    + jnp.dot(x_hi, w_bf[half:, :], preferred_element_type=F32))

        g = up(wg_bf)
        u = up(wu_bf)
        hid = (g * _sigmoid(g) * u).astype(BF16)
        y_ref[...] = _pack_bf16_halves(jnp.dot(hid, wd_bf[...], preferred_element_type=F32))

    @pl.when(jnp.logical_not(used))
    def _():
        y_ref[...] = jnp.zeros_like(y_ref)


def _experts(block_expert, n_used, xs, wg, wu, wd):
    rows, dp = xs.shape
    d = wg.shape[1]
    assert dp * 2 == d
    de = wg.shape[2]
    nb = rows // EXPERT_BLOCK
    last = lambda i, nu: jnp.minimum(i, nu[0] - 1)
    grid_spec = pltpu.PrefetchScalarGridSpec(
        num_scalar_prefetch=2,
        grid=(nb,),
        in_specs=[pl.BlockSpec((EXPERT_BLOCK, dp), lambda i, be, nu: (last(i, nu), 0)),
                  pl.BlockSpec((1, d, de), lambda i, be, nu: (be[last(i, nu)], 0, 0)),
                  pl.BlockSpec((1, d, de), lambda i, be, nu: (be[last(i, nu)], 0, 0)),
                  pl.BlockSpec((1, de, d), lambda i, be, nu: (be[last(i, nu)], 0, 0))],
        out_specs=pl.BlockSpec((EXPERT_BLOCK, dp), lambda i, be, nu: (i, 0)),
        scratch_shapes=[pltpu.VMEM((d, de), BF16), pltpu.VMEM((d, de), BF16), pltpu.VMEM((de, d), BF16)],
    )
    return pl.pallas_call(
        _experts_kernel,
        grid_spec=grid_spec,
        out_shape=jax.ShapeDtypeStruct((rows, dp), jnp.uint32),
        compiler_params=_params(1),
        name="experts",
    )(block_expert, n_used, xs, wg, wu, wd)


def _combine_kernel(pos0_ref, pos1_ref, h_ref, wt_ref, gf_ref, y_hbm_ref, out_ref, buf_ref, sem):
    tm = h_ref.shape[0]

    for t in range(tm):
        _row_copy(y_hbm_ref, pos0_ref[t], buf_ref.at[0], t, sem).start(priority=0)
        _row_copy(y_hbm_ref, pos1_ref[t], buf_ref.at[1], t, sem).start(priority=1)
    for k in range(TOP_K):
        pltpu.make_async_copy(y_hbm_ref.at[pl.ds(0, tm)], buf_ref.at[k], sem).wait()
    wt = wt_ref[...]
    lo0, hi0 = _unpack_bf16_halves(buf_ref[0])
    lo1, hi1 = _unpack_bf16_halves(buf_ref[1])
    w0 = wt[:, 0:1]
    w1 = wt[:, 1:2]
    moe = jnp.concatenate([lo0 * w0 + lo1 * w1, hi0 * w0 + hi1 * w1], axis=1)
    out_ref[...] = _rmsnorm(h_ref[...] + moe, gf_ref[...])


def _combine(pos0, pos1, h, wt, gf, y_sorted, row0, n_rows):
    d = h.shape[1]
    dp = y_sorted.shape[1]
    assert row0 % ROW_TILE == 0 and n_rows % ROW_TILE == 0
    blk0 = row0 // ROW_TILE
    return pl.pallas_call(
        _combine_kernel,
        grid=(n_rows // ROW_TILE,),
        in_specs=[pl.BlockSpec((ROW_TILE,), lambda i: (blk0 + i,), memory_space=pltpu.SMEM),
                  pl.BlockSpec((ROW_TILE,), lambda i: (blk0 + i,), memory_space=pltpu.SMEM),
                  pl.BlockSpec((ROW_TILE, d), lambda i: (blk0 + i, 0)),
                  pl.BlockSpec((ROW_TILE, LANES), lambda i: (blk0 + i, 0)),
                  _const_spec((1, d)),
                  pl.BlockSpec(memory_space=pl.ANY)],
        out_specs=pl.BlockSpec((ROW_TILE, d), lambda i: (i, 0)),
        out_shape=jax.ShapeDtypeStruct((n_rows, d), F32),
        scratch_shapes=[pltpu.VMEM((TOP_K, ROW_TILE, dp), jnp.uint32), pltpu.SemaphoreType.DMA(())],
        compiler_params=_params(1),
        name="combine",
    )(pos0, pos1, h, wt, gf, y_sorted)


def _prompt_bias_base(rel_bias):
    m = jnp.arange(BIAS_BASE)
    dist = (Q_BLOCK - 1) - m + (KEY_BLOCKS - 1) * Q_BLOCK
    idx = jnp.clip(dist, -MAX_REL, MAX_REL) + MAX_REL
    return rel_bias[:, idx].astype(F32)


def _sample_bias(rel_bias, cache_len, t):
    qi = jnp.arange(t)[:, None]
    kj = jnp.arange(cache_len + t)[None, :]
    idx = jnp.clip(qi + cache_len - kj, -MAX_REL, MAX_REL) + MAX_REL
    bias = rel_bias[:, idx].astype(F32)
    bias = jnp.where((kj >= cache_len - BAND_PAST)[None], bias, NEG_INF)
    return bias.reshape(N_HEADS * t, cache_len + t)


def _block_diag(w):
    nb, bw, _ = w.shape
    eye = jnp.eye(nb, dtype=w.dtype)
    return jnp.einsum("ncd,nm->ncmd", w, eye).reshape(nb * bw, nb * bw)


def kernel(x_prompt, x_sample, cache_k, cache_v, state_conv, state_h, norm1_g, w_in, b_in, rel_bias, w_att_branch, conv_w, conv_b, w_rec_gate, b_rec_gate, w_in_gate, b_in_gate, lru_lambda, w_rnn_branch, w_out, norm2_g, w_router_group, b_router_group, w_router_expert, b_router_expert, w_e_gate, w_e_up, w_e_down, final_norm_g):
    batch, seq, d = x_prompt.shape
    dec_b, dec_t, _ = x_sample.shape
    depth = w_in.shape[0]
    assert depth == 1
    att_w = N_HEADS * HEAD_DIM
    rnn_w = conv_w.shape[2]
    cache_len = cache_k.shape[2]
    n_p = batch * seq
    n_s = dec_b * dec_t
    n = n_p + n_s
    keep = min(BAND_PAST + CHUNK, seq)
    row2 = lambda a: a.reshape(1, -1).astype(F32)

    tiles_per_seq = seq // ROW_TILE
    tail_tiles = -(-keep // ROW_TILE)
    state_tiles = tuple(b * tiles_per_seq + tiles_per_seq - tail_tiles + j
                        for b in range(batch) for j in range(tail_tiles)) + (n_p // ROW_TILE,)

    xp = x_prompt.reshape(n_p, d)
    xs = x_sample.reshape(n_s, d)
    rnn_args = (conv_w[0].astype(F32), row2(conv_b[0]),
                _block_diag(w_rec_gate[0]).astype(BF16), row2(b_rec_gate[0]),
                _block_diag(w_in_gate[0]).astype(BF16), row2(b_in_gate[0]), row2(lru_lambda[0]))
    q, k, v, kvt, ga, gr, rnn_p, conv_p, h_p, rx_s, rgate_s = _proj(
        xp, xs, row2(norm1_g[0]), w_in[0].astype(BF16), row2(b_in[0]), rnn_args, att_w, rnn_w, state_tiles, batch)

    att_p = _attn_prompt(q, k, v, _prompt_bias_base(rel_bias[0]), batch, seq)
    att_s, new_k_s, new_v_s = _attn_sample(
        q, k, v, kvt, cache_k[0].reshape(dec_b, cache_len, att_w), cache_v[0].reshape(dec_b, cache_len, att_w),
        _sample_bias(rel_bias[0], cache_len, dec_t), n_p, batch * tail_tiles * ROW_TILE, dec_b, dec_t)

    pad_state = lambda s: jnp.pad(s.astype(F32), ((0, 0), (SUBLANES - (CONV_W - 1), 0), (0, 0)))
    rnn_s, conv_s, h_s = _rnn(rx_s, rgate_s, pad_state(state_conv[0]), state_h[0].astype(F32)[:, None, :],
                              *rnn_args, 0, dec_b, dec_t, dec_t)

    w_r = jnp.zeros((d, LANES), F32).at[:, :N_GROUPS].set(w_router_group[0])
    w_r = w_r.at[:, N_GROUPS:N_GROUPS + N_EXPERTS].set(w_router_expert[0]).astype(BF16)
    b_r = jnp.zeros((1, LANES), F32).at[0, :N_GROUPS].set(b_router_group[0])
    b_r = b_r.at[0, N_GROUPS:N_GROUPS + N_EXPERTS].set(b_router_expert[0])
    h, hn, route, wt, cnt = _merge(xp, xs, att_p, att_s, rnn_p, rnn_s, ga, gr,
                                   w_att_branch[0].astype(BF16), w_rnn_branch[0].astype(BF16), w_out[0].astype(BF16),
                                   row2(norm2_g[0]), w_r, b_r)

    counts = cnt[0, :N_EXPERTS].astype(jnp.int32)
    padded = (counts + EXPERT_BLOCK - 1) // EXPERT_BLOCK * EXPERT_BLOCK
    pad_end = jnp.cumsum(padded)
    pad_start = pad_end - padded
    expert = route[0:TOP_K].astype(jnp.int32)
    rank = route[TOP_K:2 * TOP_K].astype(jnp.int32)
    ids = jnp.arange(N_EXPERTS, dtype=jnp.int32)[:, None, None]
    pos = jnp.sum(jnp.where(expert[None] == ids, pad_start[:, None, None], 0), axis=0) + rank
    n_blocks = -(-(n * TOP_K) // EXPERT_BLOCK) + N_EXPERTS
    rows = n_blocks * EXPERT_BLOCK
    block_start = jnp.arange(n_blocks, dtype=jnp.int32) * EXPERT_BLOCK
    block_expert = jnp.minimum(jnp.sum((block_start[:, None] >= pad_end[None, :]).astype(jnp.int32), axis=1),
                               N_EXPERTS - 1)
    n_used = (pad_end[-1:] // EXPERT_BLOCK).astype(jnp.int32)

    x_sorted = _dispatch(pad_end, padded, pos[0], pos[1], hn, rows)
    y_sorted = _experts(block_expert, n_used, x_sorted, w_e_gate[0], w_e_up[0], w_e_down[0])
    gf = row2(final_norm_g)
    y_p = _combine(pos[0], pos[1], h, wt, gf, y_sorted, 0, n_p)
    y_s = _combine(pos[0], pos[1], h, wt, gf, y_sorted, n_p, n_s)

    def kv_state(col0):
        span = tail_tiles * ROW_TILE
        tails = [kvt[b * span + span - keep:(b + 1) * span, col0:col0 + att_w] for b in range(batch)]
        return jnp.stack(tails).reshape(1, batch, keep, N_HEADS, HEAD_DIM)

    heads = lambda a: a.reshape(1, dec_b, cache_len, N_HEADS, HEAD_DIM)
    return (y_p.reshape(batch, seq, d), y_s.reshape(dec_b, dec_t, d),
            kv_state(0), kv_state(att_w),
            conv_p[:, SUBLANES - (CONV_W - 1):][None], h_p[:, 0][None],
            heads(new_k_s), heads(new_v_s),
            conv_s[:, SUBLANES - (CONV_W - 1):][None], h_s[:, 0][None])
```

```python
import functools

import jax
import jax.numpy as jnp
from jax import lax
from jax.experimental import pallas as pl
from jax.experimental.pallas import tpu as pltpu

CHUNK = 64
LEFT_CHUNKS = 8
BAND_PAST = LEFT_CHUNKS * CHUNK
N_HEADS = 8
HEAD_DIM = 64
MAX_REL = 256
NEG_INF = -1e30
RNN_BLOCKS = 8
CONV_W = 4
RG_C = 8.0
N_GROUPS = 4
EXPERTS_PER_GROUP = 8
N_EXPERTS = N_GROUPS * EXPERTS_PER_GROUP
TOP_K = 2
EPS = 1e-6

LANES = 128
SUBLANES = 8
ROW_TILE = 512
Q_BLOCK = 256
KEY_BLOCKS = 3
KEY_WINDOW = KEY_BLOCKS * Q_BLOCK
BIAS_BASE = 1024
EXPERT_BLOCK = 512
VMEM_LIMIT = 56 * 1024 * 1024

BF16 = jnp.bfloat16
F32 = jnp.float32


def _params(n_axes, vmem=VMEM_LIMIT):
    return pltpu.CompilerParams(dimension_semantics=("arbitrary",) * n_axes, vmem_limit_bytes=vmem)


def _const_spec(shape):
    zeros = (0,) * len(shape)
    return pl.BlockSpec(shape, lambda *_: zeros, pipeline_mode=pl.Buffered(1))


def _rmsnorm(x, g):
    return x * lax.rsqrt(jnp.mean(x * x, axis=-1, keepdims=True) + EPS) * g


def _pack_bf16_halves(x):
    c = x.shape[1] // 2
    lo = lax.bitcast_convert_type(x[:, :c].astype(BF16).astype(F32), jnp.uint32)
    hi = lax.bitcast_convert_type(x[:, c:].astype(BF16).astype(F32), jnp.uint32)
    return (lo >> 16) | (hi & jnp.uint32(0xFFFF0000))


def _unpack_bf16_halves(p):
    lo = lax.bitcast_convert_type(p << 16, F32)
    hi = lax.bitcast_convert_type(p & jnp.uint32(0xFFFF0000), F32)
    return lo, hi


def _sigmoid(x):
    return 0.5 * jnp.tanh(0.5 * x) + 0.5


def _proj_kernel(n_prompt_tiles, tiles_per_seq, state_tiles, att_w, rnn_w, d_model,
                 xp_ref, xs_ref, g_ref, w_ref, b_ref, cw_ref, cb_ref, wa_ref, ba_ref, wx_ref, bx_ref, lam_ref,
                 q_ref, k_ref, v_ref, kvt_ref, ga_ref, gr_ref, rnn_ref, conv_out_ref, h_out_ref, rxs_ref, rgs_ref,
                 rx_ref, tail_ref, hl_ref, p_ref, hs_ref, h_ref):
    i = pl.program_id(0)
    tm = xp_ref.shape[0]
    x = jnp.where(i < n_prompt_tiles, xp_ref[...], xs_ref[...])
    xn = _rmsnorm(x, g_ref[...]).astype(BF16)

    def seg(lo, width):
        return jnp.dot(xn, w_ref[:, lo:lo + width], preferred_element_type=F32) + b_ref[:, lo:lo + width]

    q = seg(0, att_w)
    k = seg(att_w, att_w)
    v = seg(2 * att_w, att_w)
    q_ref[...] = (q * (HEAD_DIM ** -0.5)).astype(BF16)
    k_ref[...] = k.astype(BF16)
    v_ref[...] = v.astype(BF16)

    @pl.when(functools.reduce(jnp.logical_or, [i == t for t in state_tiles]))
    def _():
        kvt_ref[:, :att_w] = k
        kvt_ref[:, att_w:] = v

    o = 3 * att_w
    ncol = rnn_w // LANES
    rx = seg(o, rnn_w)
    for c in range(ncol):
        rx_ref[c] = rx[:, c * LANES:(c + 1) * LANES]
    rg = seg(o + rnn_w, rnn_w)

    @pl.when(i % tiles_per_seq == 0)
    def _():
        tail_ref[...] = jnp.zeros_like(tail_ref)
        h_ref[...] = jnp.zeros_like(h_ref)

    hseq, tail, h_last = _rglru_prompt_tile(tm, rx_ref, cw_ref, cb_ref, wa_ref, ba_ref, wx_ref, bx_ref, lam_ref,
                                            tail_ref, hl_ref, p_ref, hs_ref, h_ref)
    rnn_ref[...] = (hseq * jax.nn.gelu(rg)).astype(BF16)
    ga_ref[...] = seg(o + 2 * rnn_w, d_model).astype(BF16)
    gr_ref[...] = seg(o + 2 * rnn_w + d_model, d_model).astype(BF16)

    @pl.when(i < n_prompt_tiles)
    def _():
        conv_out_ref[0] = tail
        h_out_ref[0] = h_last

    @pl.when(i == n_prompt_tiles)
    def _():
        for c in range(ncol):
            rxs_ref[c] = rx_ref[c]
        rgs_ref[...] = rg.astype(BF16)


def _proj(xp, xs, g1, w_in, b_in, rnn_args, att_w, rnn_w, state_tiles, batch):
    n_p, d = xp.shape
    n_s = xs.shape[0]
    assert n_p % ROW_TILE == 0 and n_s == ROW_TILE
    assert ROW_TILE % (SUBLANES * SUBLANES) == 0 and ROW_TILE // SUBLANES >= CONV_W
    n = n_p + n_s
    npt = n_p // ROW_TILE
    tps = npt // batch
    pw = w_in.shape[1]
    ncol = rnn_w // LANES
    row = lambda w: pl.BlockSpec((ROW_TILE, w), lambda i: (i, 0))
    rows = lambda w, dt: jax.ShapeDtypeStruct((n, w), dt)
    slot = lambda i: sum(jnp.where(i > t, 1, 0) for t in state_tiles)
    state = lambda r: pl.BlockSpec((1, r, rnn_w), lambda i: (jnp.minimum(i // tps, batch - 1), 0, 0))
    out_specs = [row(att_w), row(att_w), row(att_w),
                 pl.BlockSpec((ROW_TILE, 2 * att_w), lambda i: (slot(i), 0)),
                 row(d), row(d), row(rnn_w), state(SUBLANES), state(1),
                 pl.BlockSpec((ncol, ROW_TILE, LANES), lambda i: (0, 0, 0)),
                 pl.BlockSpec((ROW_TILE, rnn_w), lambda i: (0, 0))]
    out_shape = [rows(att_w, BF16), rows(att_w, BF16), rows(att_w, BF16),
                 jax.ShapeDtypeStruct((len(state_tiles) * ROW_TILE, 2 * att_w), F32),
                 rows(d, BF16), rows(d, BF16), rows(rnn_w, BF16),
                 jax.ShapeDtypeStruct((batch, SUBLANES, rnn_w), F32), jax.ShapeDtypeStruct((batch, 1, rnn_w), F32),
                 jax.ShapeDtypeStruct((ncol, n_s, LANES), F32), jax.ShapeDtypeStruct((n_s, rnn_w), BF16)]
    tile = lambda: pltpu.VMEM((ROW_TILE, rnn_w), F32)
    cols = lambda: pltpu.VMEM((ncol, ROW_TILE, LANES), F32)
    return pl.pallas_call(
        functools.partial(_proj_kernel, npt, tps, state_tiles, att_w, rnn_w, d),
        grid=(n // ROW_TILE,),
        in_specs=[pl.BlockSpec((ROW_TILE, d), lambda i: (jnp.minimum(i, npt - 1), 0)),
                  pl.BlockSpec((ROW_TILE, d), lambda i: (0, 0)),
                  _const_spec((1, d)), _const_spec((d, pw)), _const_spec((1, pw))]
                 + [_const_spec(a.shape) for a in rnn_args],
        out_specs=out_specs,
        out_shape=out_shape,
        scratch_shapes=[cols(), pltpu.VMEM((SUBLANES, rnn_w), F32), tile(), tile(), cols(),
                        pltpu.VMEM((1, rnn_w), F32)],
        compiler_params=_params(1),
        name="proj_rglru",
    )(xp, xs, g1, w_in, b_in, *rnn_args)


def _attn_p_kernel(q_ref, k0_ref, k1_ref, k2_ref, v0_ref, v1_ref, v2_ref, base_ref, o_ref, bias_ref):
    b = pl.program_id(0)
    i = pl.program_id(1)

    @pl.when((b == 0) & (i == 0))
    def _():
        r = lax.broadcasted_iota(jnp.int32, (Q_BLOCK, KEY_WINDOW), 0) // CHUNK
        c = lax.broadcasted_iota(jnp.int32, (Q_BLOCK, KEY_WINDOW), 1) // CHUNK
        in_band = (c >= r) & (c <= r + LEFT_CHUNKS)
        for h in range(N_HEADS):
            rows = jnp.broadcast_to(base_ref[h:h + 1, :], (Q_BLOCK, BIAS_BASE))
            toeplitz = pltpu.roll(rows, BIAS_BASE - (Q_BLOCK - 1), axis=1, stride=1, stride_axis=0)
            bias_ref[h] = jnp.where(in_band, toeplitz[:, :KEY_WINDOW], NEG_INF)

    kwin = jnp.concatenate([k0_ref[...], k1_ref[...], k2_ref[...]], axis=0)
    vwin = jnp.concatenate([v0_ref[...], v1_ref[...], v2_ref[...]], axis=0)
    lane = lax.broadcasted_iota(jnp.int32, (1, LANES), 1)

    def heads(at_sequence_start):
        if at_sequence_start:
            col = lax.broadcasted_iota(jnp.int32, (1, KEY_WINDOW), 1)
            start_mask = jnp.where(col // Q_BLOCK + i - (KEY_BLOCKS - 1) >= 0, 0.0, NEG_INF).astype(F32)
        for pair in range(N_HEADS // 2):
            sl = slice(pair * LANES, (pair + 1) * LANES)
            q2 = q_ref[:, sl]
            k2 = kwin[:, sl]
            v2 = vwin[:, sl]
            acc = jnp.zeros((Q_BLOCK, LANES), F32)
            for half in range(2):
                hmask = (lane // HEAD_DIM) == half
                qm = jnp.where(hmask, q2, jnp.zeros_like(q2))
                vm = jnp.where(hmask, v2, jnp.zeros_like(v2))
                s = lax.dot_general(qm, k2, (((1,), (1,)), ((), ())), preferred_element_type=F32)
                s = s + bias_ref[2 * pair + half]
                if at_sequence_start:
                    s = s + start_mask
                m = jnp.max(s, axis=-1, keepdims=True)
                e = jnp.exp(s - m)
                l = jnp.sum(e, axis=-1, keepdims=True)
                o = jnp.dot(e.astype(BF16), vm, preferred_element_type=F32)
                acc = acc + o / l
            o_ref[:, sl] = acc.astype(BF16)

    pl.when(i < KEY_BLOCKS - 1)(lambda: heads(True))
    pl.when(i >= KEY_BLOCKS - 1)(lambda: heads(False))


def _attn_prompt(q, k, v, base, batch, seq):
    w = q.shape[1]
    nqb = seq // Q_BLOCK
    kspec = lambda back: pl.BlockSpec((Q_BLOCK, w), lambda b, i: (b * nqb + jnp.maximum(i - back, 0), 0))
    return pl.pallas_call(
        _attn_p_kernel,
        grid=(batch, nqb),
        in_specs=[pl.BlockSpec((Q_BLOCK, w), lambda b, i: (b * nqb + i, 0)),
                  kspec(2), kspec(1), kspec(0), kspec(2), kspec(1), kspec(0),
                  _const_spec(base.shape)],
        out_specs=pl.BlockSpec((Q_BLOCK, w), lambda b, i: (b * nqb + i, 0)),
        out_shape=jax.ShapeDtypeStruct((batch * seq, w), BF16),
        scratch_shapes=[pltpu.VMEM((N_HEADS, Q_BLOCK, KEY_WINDOW), F32)],
        compiler_params=_params(2),
        name="attn_prompt",
    )(q, k, k, k, v, v, v, base)


def _attn_s_kernel(t, q_ref, k_ref, v_ref, kvt_ref, ck_ref, cv_ref, base_ref, o_ref, nk_ref, nv_ref, bias_ref):
    cache_len = ck_ref.shape[1]
    w = q_ref.shape[1]

    @pl.when(pl.program_id(0) == 0)
    def _():
        col = lax.broadcasted_iota(jnp.int32, (t, cache_len + t), 1)
        for h in range(N_HEADS):
            rows = jnp.broadcast_to(base_ref[h:h + 1, :], (t, BIAS_BASE))
            toeplitz = pltpu.roll(rows, BIAS_BASE - (t - 1), axis=1, stride=1, stride_axis=0)
            bias_ref[h * t:(h + 1) * t, :] = jnp.where(col >= cache_len - BAND_PAST, toeplitz[:, :cache_len + t],
                                                       NEG_INF)

    ck = ck_ref[0]
    cv = cv_ref[0]
    kall = jnp.concatenate([ck.astype(BF16), k_ref[...]], axis=0)
    vall = jnp.concatenate([cv.astype(BF16), v_ref[...]], axis=0)
    q = q_ref[...]
    head_of_lane = lax.broadcasted_iota(jnp.int32, (1, w), 1) // HEAD_DIM
    qs = jnp.concatenate([jnp.where(head_of_lane == h, q, jnp.zeros_like(q)) for h in range(N_HEADS)], axis=0)
    s = lax.dot_general(qs, kall, (((1,), (1,)), ((), ())), preferred_element_type=F32) + bias_ref[...]
    m = jnp.max(s, axis=-1, keepdims=True)
    e = jnp.exp(s - m)
    l = jnp.sum(e, axis=-1, keepdims=True)
    o_all = jnp.dot(e.astype(BF16), vall, preferred_element_type=F32) / l
    o = jnp.zeros((t, w), F32)
    for h in range(N_HEADS):
        o = o + jnp.where(head_of_lane == h, o_all[h * t:(h + 1) * t], 0.0)
    o_ref[...] = o.astype(BF16)
    nk_ref[0, :cache_len - t, :] = ck[t:]
    nk_ref[0, cache_len - t:, :] = kvt_ref[:, :w]
    nv_ref[0, :cache_len - t, :] = cv[t:]
    nv_ref[0, cache_len - t:, :] = kvt_ref[:, w:]


def _attn_sample(q, k, v, kvt, cache_k, cache_v, base, row0, kvt_row0, dec_b, t):
    w = q.shape[1]
    cache_len = cache_k.shape[1]
    assert row0 % t == 0 and kvt_row0 % t == 0 and cache_len + 2 * t - 1 <= BIAS_BASE
    tok = lambda width, r0: pl.BlockSpec((t, width), lambda b: (r0 // t + b, 0))
    cache = pl.BlockSpec((1, cache_len, w), lambda b: (b, 0, 0))
    return pl.pallas_call(
        functools.partial(_attn_s_kernel, t),
        grid=(dec_b,),
        in_specs=[tok(w, row0), tok(w, row0), tok(w, row0), tok(2 * w, kvt_row0), cache, cache,
                  _const_spec(base.shape)],
        out_specs=[pl.BlockSpec((t, w), lambda b: (b, 0)), cache, cache],
        out_shape=[jax.ShapeDtypeStruct((dec_b * t, w), BF16),
                   jax.ShapeDtypeStruct(cache_k.shape, F32), jax.ShapeDtypeStruct(cache_v.shape, F32)],
        scratch_shapes=[pltpu.VMEM((N_HEADS * t, cache_len + t), F32)],
        compiler_params=_params(1),
        name="attn_sample",
    )(q, k, v, kvt, cache_k, cache_v, base)


def _rnn_kernel(tm, rx_ref, rg_ref, conv0_ref, h0_ref, cw_ref, cb_ref, wa_ref, ba_ref, wx_ref, bx_ref, lam_ref,
                rnn_ref, conv_out_ref, h_out_ref, ext_ref, a_ref, b_ref, hs_ref, h_ref):
    j = pl.program_id(1)

    @pl.when(j == 0)
    def _():
        ext_ref[:SUBLANES, :] = conv0_ref[0]
        h_ref[...] = h0_ref[0]

    ext_ref[SUBLANES:, :] = jnp.concatenate([rx_ref[c] for c in range(rx_ref.shape[0])], axis=1)
    cw = cw_ref[...]
    xc = cb_ref[...]
    for tap in range(CONV_W):
        off = SUBLANES - (CONV_W - 1) + tap
        xc = xc + ext_ref[off:off + tm, :] * cw[tap:tap + 1, :]
    tail = ext_ref[tm:tm + SUBLANES, :]
    ext_ref[:SUBLANES, :] = tail
    conv_out_ref[0] = tail

    a, b = _rglru_coeffs(xc, wa_ref, ba_ref, wx_ref, bx_ref, lam_ref)

    sub = lax.broadcasted_iota(jnp.int32, a.shape, 0) % SUBLANES
    shift = 1
    while shift < SUBLANES:
        keep = sub >= shift
        a_prev = pltpu.roll(a, shift, axis=0)
        b_prev = pltpu.roll(b, shift, axis=0)
        b = jnp.where(keep, a * b_prev + b, b)
        a = jnp.where(keep, a * a_prev, a)
        shift *= 2
    a_ref[...] = a
    b_ref[...] = b

    def group(g, h):
        r0 = pl.multiple_of(g * SUBLANES, SUBLANES)
        hg = b_ref[pl.ds(r0, SUBLANES), :] + a_ref[pl.ds(r0, SUBLANES), :] * h
        hs_ref[pl.ds(r0, SUBLANES), :] = hg
        return hg[SUBLANES - 1:SUBLANES, :]

    h_last = lax.fori_loop(0, tm // SUBLANES, group, h_ref[...])
    h_ref[...] = h_last
    h_out_ref[0] = h_last
    rnn_ref[...] = (hs_ref[...] * jax.nn.gelu(rg_ref[...].astype(F32))).astype(BF16)


def _rnn(rx, rgate, conv0, h0, cw, cb, wa, ba, wx, bx, lam, row0, batch, seq, tm):
    w = rgate.shape[1]
    assert row0 % tm == 0 and seq % tm == 0 and tm % SUBLANES == 0
    blk0 = row0 // tm
    nt = seq // tm
    tok = pl.BlockSpec((tm, w), lambda b, j: (blk0 + b * nt + j, 0))
    tok_cols = pl.BlockSpec((w // LANES, tm, LANES), lambda b, j: (0, blk0 + b * nt + j, 0))
    state = lambda rows: pl.BlockSpec((1, rows, w), lambda b, j: (b, 0, 0))
    return pl.pallas_call(
        functools.partial(_rnn_kernel, tm),
        grid=(batch, nt),
        in_specs=[tok_cols, tok, state(SUBLANES), state(1),
                  _const_spec(cw.shape), _const_spec((1, w)), _const_spec((w, w)), _const_spec((1, w)),
                  _const_spec((w, w)), _const_spec((1, w)), _const_spec((1, w))],
        out_specs=[pl.BlockSpec((tm, w), lambda b, j: (b * nt + j, 0)), state(SUBLANES), state(1)],
        out_shape=[jax.ShapeDtypeStruct((batch * seq, w), BF16),
                   jax.ShapeDtypeStruct((batch, SUBLANES, w), F32),
                   jax.ShapeDtypeStruct((batch, 1, w), F32)],
        scratch_shapes=[pltpu.VMEM((tm + SUBLANES, w), F32), pltpu.VMEM((tm, w), F32), pltpu.VMEM((tm, w), F32),
                        pltpu.VMEM((tm, w), F32), pltpu.VMEM((1, w), F32)],
        compiler_params=_params(2),
        name="rnn",
    )(rx, rgate, conv0, h0, cw, cb, wa, ba, wx, bx, lam)


def _rglru_coeffs(xc, wa_ref, ba_ref, wx_ref, bx_ref, lam_ref):
    xb = xc.astype(BF16)
    r = _sigmoid(jnp.dot(xb, wa_ref[...], preferred_element_type=F32) + ba_ref[...])
    gi = _sigmoid(jnp.dot(xb, wx_ref[...], preferred_element_type=F32) + bx_ref[...])
    neg_lam = -lam_ref[...]
    softplus = jnp.maximum(neg_lam, 0.0) + jnp.log1p(jnp.exp(-jnp.abs(neg_lam)))
    log_a = (-RG_C) * r * softplus
    a = jnp.exp(log_a)
    b = jnp.sqrt(-jnp.tanh(log_a) * (a * a + 1.0)) * (gi * xc)
    return a, b


def _rglru_prompt_tile(tm, rx_ref, cw_ref, cb_ref, wa_ref, ba_ref, wx_ref, bx_ref, lam_ref,
                       tail_ref, hl_ref, p_ref, hs_ref, h_ref):
    seg = tm // SUBLANES
    ncol = rx_ref.shape[0]
    w = ncol * LANES

    def strided_rows(ref, j):
        return jnp.concatenate([ref[c, pl.ds(j, SUBLANES, stride=seg), :] for c in range(ncol)], axis=1)

    slab = [strided_rows(rx_ref, j) for j in range(seg)]
    sub = lax.broadcasted_iota(jnp.int32, (SUBLANES, w), 0)

    def before(k):
        carry = jnp.broadcast_to(tail_ref[SUBLANES - k:SUBLANES - k + 1, :], (SUBLANES, w))
        return jnp.where(sub == 0, carry, pltpu.roll(slab[seg - k], 1, axis=0))

    lead = {k: before(k) for k in range(1, CONV_W)}
    cw = cw_ref[...]
    cb = cb_ref[...]
    xc = []
    for j in range(seg):
        acc = cb + slab[j] * cw[CONV_W - 1:CONV_W, :]
        for k in range(1, CONV_W):
            prev = slab[j - k] if j >= k else lead[k - j]
            acc = acc + prev * cw[CONV_W - 1 - k:CONV_W - k, :]
        xc.append(acc)
    tail = jnp.concatenate([rx_ref[c, tm - SUBLANES:tm, :] for c in range(ncol)], axis=1)
    tail_ref[...] = tail

    a, b = _rglru_coeffs(jnp.concatenate(xc, axis=0), wa_ref, ba_ref, wx_ref, bx_ref, lam_ref)

    h = jnp.zeros((SUBLANES, w), F32)
    p = jnp.ones((SUBLANES, w), F32)
    for j in range(seg):
        rows = slice(j * SUBLANES, (j + 1) * SUBLANES)
        h = a[rows] * h + b[rows]
        p = a[rows] * p
        hl_ref[rows, :] = h
        p_ref[rows, :] = p
    c = h_ref[...]
    enter = []
    for s in range(SUBLANES):
        enter.append(c)
        c = h[s:s + 1, :] + p[s:s + 1, :] * c
    last = c
    h_ref[...] = last
    enter = jnp.concatenate(enter, axis=0)
    for j in range(seg):
        rows = slice(j * SUBLANES, (j + 1) * SUBLANES)
        hj = hl_ref[rows, :] + p_ref[rows, :] * enter
        for col in range(ncol):
            hs_ref[col, pl.ds(j, SUBLANES, stride=seg), :] = hj[:, col * LANES:(col + 1) * LANES]
    hseq = jnp.concatenate([hs_ref[col] for col in range(ncol)], axis=1)
    return hseq, tail, last


ROUTE_ROWS = SUBLANES
MERGE_SPLIT = 1


def _merge_kernel(n_prompt_tiles,
                  xp_ref, xs_ref, attp_ref, atts_ref, rnnp_ref, rnns_ref, ga_ref, gr_ref,
                  watt_ref, wrnn_ref, wout_ref, g2_ref, wr_ref, br_ref,
                  h_ref, hn_ref, route_ref, wt_ref, cnt_ref, carry_ref):
    i = pl.program_id(0)
    is_p = i < n_prompt_tiles

    @pl.when(i == 0)
    def _():
        carry_ref[...] = jnp.zeros_like(carry_ref)

    carry = carry_ref[...]
    for g in range(MERGE_SPLIT):
        carry = _merge_rows(is_p, g * (xp_ref.shape[0] // MERGE_SPLIT), xp_ref.shape[0] // MERGE_SPLIT, carry,
                            xp_ref, xs_ref, attp_ref, atts_ref, rnnp_ref, rnns_ref, ga_ref, gr_ref,
                            watt_ref, wrnn_ref, wout_ref, g2_ref, wr_ref, br_ref,
                            h_ref, hn_ref, route_ref, wt_ref)
    carry_ref[...] = carry
    cnt_ref[...] = carry


def _merge_rows(is_p, r0, tm, carry,
                xp_ref, xs_ref, attp_ref, atts_ref, rnnp_ref, rnns_ref, ga_ref, gr_ref,
                watt_ref, wrnn_ref, wout_ref, g2_ref, wr_ref, br_ref,
                h_ref, hn_ref, route_ref, wt_ref):
    rows = slice(r0, r0 + tm)
    x = jnp.where(is_p, xp_ref[rows, :], xs_ref[rows, :])
    att = jnp.where(is_p, attp_ref[rows, :], atts_ref[rows, :])
    rnn = jnp.where(is_p, rnnp_ref[rows, :], rnns_ref[rows, :])
    ya = jnp.dot(att, watt_ref[...], preferred_element_type=F32)
    yr = jnp.dot(rnn, wrnn_ref[...], preferred_element_type=F32)
    mixed = _sigmoid(ga_ref[rows, :].astype(F32)) * ya + _sigmoid(gr_ref[rows, :].astype(F32)) * yr
    h = x + jnp.dot(mixed.astype(BF16), wout_ref[...], preferred_element_type=F32)
    h_ref[rows, :] = h
    hn = _rmsnorm(h, g2_ref[...])
    hn_ref[rows, :] = _pack_bf16_halves(hn)

    logits = jnp.dot(hn.astype(BF16), wr_ref[...], preferred_element_type=F32) + br_ref[...]
    lane = lax.broadcasted_iota(jnp.int32, (tm, LANES), 1)
    low = jnp.float32(-3e38)
    big = jnp.int32(1 << 20)
    lg = jnp.where(lane < N_GROUPS, logits, low)
    mg = jnp.max(lg, axis=-1, keepdims=True)
    p_grp = 1.0 / jnp.sum(jnp.exp(lg - mg), axis=-1, keepdims=True)
    grp = jnp.min(jnp.where(lg == mg, lane, big), axis=-1, keepdims=True)
    lo = N_GROUPS + EXPERTS_PER_GROUP * grp
    le = jnp.where((lane >= lo) & (lane < lo + EXPERTS_PER_GROUP), logits, low)
    m1 = jnp.max(le, axis=-1, keepdims=True)
    i1 = jnp.min(jnp.where(le == m1, lane, big), axis=-1, keepdims=True)
    le2 = jnp.where(lane == i1, low, le)
    m2 = jnp.max(le2, axis=-1, keepdims=True)
    i2 = jnp.min(jnp.where(le2 == m2, lane, big), axis=-1, keepdims=True)
    e2 = jnp.exp(m2 - m1)
    inv = 1.0 / (1.0 + e2)
    w1 = p_grp * inv
    w2 = p_grp * (e2 * inv)
    x1 = i1 - N_GROUPS
    x2 = i2 - N_GROUPS

    onehot = jnp.where((lane == x1) | (lane == x2), 1.0, 0.0).astype(BF16)
    rowi = lax.broadcasted_iota(jnp.int32, (tm, tm), 0)
    coli = lax.broadcasted_iota(jnp.int32, (tm, tm), 1)
    tri = jnp.where(coli < rowi, 1.0, 0.0).astype(BF16)
    before = jnp.dot(tri, onehot, preferred_element_type=F32) + carry
    r1 = jnp.sum(jnp.where(lane == x1, before, 0.0), axis=-1, keepdims=True)
    r2 = jnp.sum(jnp.where(lane == x2, before, 0.0), axis=-1, keepdims=True)

    fields = (x1.astype(F32), x2.astype(F32), r1, r2, w1, w2)
    slab = jnp.zeros((tm, LANES), F32)
    for idx, val in enumerate(fields):
        slab = jnp.where(lane == idx, val, slab)
    route_ref[:, rows] = slab.T[:ROUTE_ROWS, :]
    wt_ref[rows, :] = jnp.where(lane == 0, w1, jnp.where(lane == 1, w2, 0.0))
    return carry + jnp.sum(onehot.astype(F32), axis=0, keepdims=True)


def _merge(xp, xs, att_p, att_s, rnn_p, rnn_s, ga, gr, w_att, w_rnn, w_out, g2, w_r, b_r):
    n_p, d = xp.shape
    n = ga.shape[0]
    npt = n_p // ROW_TILE
    aw = att_p.shape[1]
    rw = rnn_p.shape[1]
    pspec = lambda w: pl.BlockSpec((ROW_TILE, w), lambda i: (jnp.minimum(i, npt - 1), 0))
    sspec = lambda w: pl.BlockSpec((ROW_TILE, w), lambda i: (0, 0))
    row = lambda w: pl.BlockSpec((ROW_TILE, w), lambda i: (i, 0))
    return pl.pallas_call(
        functools.partial(_merge_kernel, npt),
        grid=(n // ROW_TILE,),
        in_specs=[pspec(d), sspec(d), pspec(aw), sspec(aw), pspec(rw), sspec(rw), row(d), row(d),
                  _const_spec(w_att.shape), _const_spec(w_rnn.shape), _const_spec(w_out.shape),
                  _const_spec((1, d)), _const_spec(w_r.shape), _const_spec((1, LANES))],
        out_specs=[row(d), row(d // 2), pl.BlockSpec((ROUTE_ROWS, ROW_TILE), lambda i: (0, i)), row(LANES),
                   pl.BlockSpec((1, LANES), lambda i: (0, 0))],
        out_shape=[jax.ShapeDtypeStruct((n, d), F32), jax.ShapeDtypeStruct((n, d // 2), jnp.uint32),
                   jax.ShapeDtypeStruct((ROUTE_ROWS, n), F32), jax.ShapeDtypeStruct((n, LANES), F32),
                   jax.ShapeDtypeStruct((1, LANES), F32)],
        scratch_shapes=[pltpu.VMEM((1, LANES), F32)],
        compiler_params=_params(1),
        name="merge",
    )(xp, xs, att_p, att_s, rnn_p, rnn_s, ga, gr, w_att, w_rnn, w_out, g2, w_r, b_r)


def _row_copy(src_ref, src_row, dst_ref, dst_row, sem):
    return pltpu.make_async_copy(src_ref.at[pl.ds(src_row, 1)], dst_ref.at[pl.ds(dst_row, 1)], sem)


DISPATCH_SLOTS = 3


def _dispatch_kernel(n_tiles, pad_end_ref, padded_ref, pos0_ref, pos1_ref, hn_hbm_ref, xs_ref,
                     zero_ref, tile_ref, zero_sem, load_sem, scat_sem):
    i = pl.program_id(0)
    tm = tile_ref.shape[1]

    def load(tile, slot):
        rows = pl.ds(pl.multiple_of(tile * tm, tm), tm)
        return pltpu.make_async_copy(hn_hbm_ref.at[rows], tile_ref.at[slot], load_sem.at[slot])

    def scatter(slot):
        for t in range(tm):
            _row_copy(tile_ref.at[slot], t, xs_ref, pos0_ref[t], scat_sem.at[slot]).start(priority=0)
            _row_copy(tile_ref.at[slot], t, xs_ref, pos1_ref[t], scat_sem.at[slot]).start(priority=1)

    def drain_scatter(slot):
        for _ in range(TOP_K):
            pltpu.make_async_copy(tile_ref.at[slot], xs_ref.at[pl.ds(0, tm)], scat_sem.at[slot]).wait()

    @pl.when(i == 0)
    def _():
        load(0, 0).start()
        zero_ref[...] = jnp.zeros_like(zero_ref)

        def zero_block(start):
            start = pl.multiple_of(start, EXPERT_BLOCK)
            return pltpu.make_async_copy(zero_ref, xs_ref.at[pl.ds(start, EXPERT_BLOCK)], zero_sem)

        used_rows = pad_end_ref[N_EXPERTS - 1]
        total_rows = xs_ref.shape[0]
        for wait in (False, True):
            for e in range(N_EXPERTS):
                @pl.when(padded_ref[e] > 0)
                def _():
                    cp = zero_block(pad_end_ref[e] - EXPERT_BLOCK)
                    cp.wait() if wait else cp.start()

                @pl.when(used_rows + e * EXPERT_BLOCK < total_rows)
                def _():
                    cp = zero_block(used_rows + e * EXPERT_BLOCK)
                    cp.wait() if wait else cp.start()

    for slot in range(DISPATCH_SLOTS):
        @pl.when(((i + 1) % DISPATCH_SLOTS == slot) & (i + 1 < n_tiles))
        def _():
            pl.when(i >= DISPATCH_SLOTS - 1)(functools.partial(drain_scatter, slot))
            load(i + 1, slot).start()

    for slot in range(DISPATCH_SLOTS):
        @pl.when(i % DISPATCH_SLOTS == slot)
        def _():
            load(i, slot).wait()
            scatter(slot)

    @pl.when(i == n_tiles - 1)
    def _():
        for tile in range(max(0, n_tiles - DISPATCH_SLOTS), n_tiles):
            drain_scatter(tile % DISPATCH_SLOTS)


def _dispatch(pad_end, padded, pos0, pos1, hn, rows):
    n, d = hn.shape
    n_tiles = n // ROW_TILE
    grid_spec = pltpu.PrefetchScalarGridSpec(
        num_scalar_prefetch=2,
        grid=(n_tiles,),
        in_specs=[pl.BlockSpec((ROW_TILE,), lambda i, pe, pd: (i,), memory_space=pltpu.SMEM),
                  pl.BlockSpec((ROW_TILE,), lambda i, pe, pd: (i,), memory_space=pltpu.SMEM),
                  pl.BlockSpec(memory_space=pl.ANY)],
        out_specs=pl.BlockSpec(memory_space=pl.ANY),
        scratch_shapes=[pltpu.VMEM((EXPERT_BLOCK, d), hn.dtype), pltpu.VMEM((DISPATCH_SLOTS, ROW_TILE, d), hn.dtype),
                        pltpu.SemaphoreType.DMA(()), pltpu.SemaphoreType.DMA((DISPATCH_SLOTS,)),
                        pltpu.SemaphoreType.DMA((DISPATCH_SLOTS,))],
    )
    return pl.pallas_call(
        functools.partial(_dispatch_kernel, n_tiles),
        grid_spec=grid_spec,
        out_shape=jax.ShapeDtypeStruct((rows, d), hn.dtype),
        compiler_params=_params(1),
        name="dispatch",
    )(pad_end, padded, pos0, pos1, hn)


def _experts_kernel(be_ref, nu_ref, x_ref, wg_ref, wu_ref, wd_ref, y_ref, wg_bf, wu_bf, wd_bf):
    i = pl.program_id(0)
    used = i < nu_ref[0]

    @pl.when(used & ((i == 0) | (be_ref[i] != be_ref[jnp.maximum(i - 1, 0)])))
    def _():
        wg_bf[...] = wg_ref[0].astype(BF16)
        wu_bf[...] = wu_ref[0].astype(BF16)
        wd_bf[...] = wd_ref[0].astype(BF16)

    @pl.when(used)
    def _():
        half = wg_bf.shape[0] // 2
        x_lo, x_hi = (part.astype(BF16) for part in _unpack_bf16_halves(x_ref[...]))

        def up(w_bf):
            return (jnp.dot(x_lo, w_bf[:half, :], preferred_element_type=F32)
                    + jnp.dot(x_hi, w_bf[half:, :], preferred_element_type=F32))

        g = up(wg_bf)
        u = up(wu_bf)
        hid = (g * _sigmoid(g) * u).astype(BF16)
        y_ref[...] = _pack_bf16_halves(jnp.dot(hid, wd_bf[...], preferred_element_type=F32))

    @pl.when(jnp.logical_not(used))
    def _():
        y_ref[...] = jnp.zeros_like(y_ref)


def _experts(block_expert, n_used, xs, wg, wu, wd):
    rows, dp = xs.shape
    d = wg.shape[1]
    assert dp * 2 == d
    de = wg.shape[2]
    nb = rows // EXPERT_BLOCK
    last = lambda i, nu: jnp.minimum(i, nu[0] - 1)
    grid_spec = pltpu.PrefetchScalarGridSpec(
        num_scalar_prefetch=2,
        grid=(nb,),
        in_specs=[pl.BlockSpec((EXPERT_BLOCK, dp), lambda i, be, nu: (last(i, nu), 0)),
                  pl.BlockSpec((1, d, de), lambda i, be, nu: (be[last(i, nu)], 0, 0)),
                  pl.BlockSpec((1, d, de), lambda i, be, nu: (be[last(i, nu)], 0, 0)),
                  pl.BlockSpec((1, de, d), lambda i, be, nu: (be[last(i, nu)], 0, 0))],
        out_specs=pl.BlockSpec((EXPERT_BLOCK, dp), lambda i, be, nu: (i, 0)),
        scratch_shapes=[pltpu.VMEM((d, de), BF16), pltpu.VMEM((d, de), BF16), pltpu.VMEM((de, d), BF16)],
    )
    return pl.pallas_call(
        _experts_kernel,
        grid_spec=grid_spec,
        out_shape=jax.ShapeDtypeStruct((rows, dp), jnp.uint32),
        compiler_params=_params(1),
        name="experts",
    )(block_expert, n_used, xs, wg, wu, wd)


def _combine_kernel(n_tiles, pos0_ref, pos1_ref, next0_ref, next1_ref, h_ref, wt_ref, gf_ref, y_hbm_ref, out_ref,
                    buf_ref, sem):
    i = pl.program_id(0)
    tm = h_ref.shape[0]

    def request(p0_ref, p1_ref, slot):
        for t in range(tm):
            _row_copy(y_hbm_ref, p0_ref[t], buf_ref.at[slot, 0], t, sem.at[slot]).start(priority=0)
            _row_copy(y_hbm_ref, p1_ref[t], buf_ref.at[slot, 1], t, sem.at[slot]).start(priority=1)

    def consume(slot):
        for k in range(TOP_K):
            pltpu.make_async_copy(y_hbm_ref.at[pl.ds(0, tm)], buf_ref.at[slot, k], sem.at[slot]).wait()
        wt = wt_ref[...]
        lo0, hi0 = _unpack_bf16_halves(buf_ref[slot, 0])
        lo1, hi1 = _unpack_bf16_halves(buf_ref[slot, 1])
        w0 = wt[:, 0:1]
        w1 = wt[:, 1:2]
        moe = jnp.concatenate([lo0 * w0 + lo1 * w1, hi0 * w0 + hi1 * w1], axis=1)
        out_ref[...] = _rmsnorm(h_ref[...] + moe, gf_ref[...])

    pl.when(i == 0)(lambda: request(pos0_ref, pos1_ref, 0))
    for slot in range(2):
        pl.when((i + 1 < n_tiles) & ((i + 1) % 2 == slot))(functools.partial(request, next0_ref, next1_ref, slot))
    for slot in range(2):
        pl.when(i % 2 == slot)(functools.partial(consume, slot))


def _combine(pos0, pos1, h, wt, gf, y_sorted, row0, n_rows):
    d = h.shape[1]
    dp = y_sorted.shape[1]
    assert row0 % ROW_TILE == 0 and n_rows % ROW_TILE == 0
    blk0 = row0 // ROW_TILE
    n_tiles = n_rows // ROW_TILE
    last = blk0 + n_tiles - 1
    return pl.pallas_call(
        functools.partial(_combine_kernel, n_tiles),
        grid=(n_tiles,),
        in_specs=[pl.BlockSpec((ROW_TILE,), lambda i: (blk0 + i,), memory_space=pltpu.SMEM),
                  pl.BlockSpec((ROW_TILE,), lambda i: (blk0 + i,), memory_space=pltpu.SMEM),
                  pl.BlockSpec((ROW_TILE,), lambda i: (jnp.minimum(blk0 + i + 1, last),), memory_space=pltpu.SMEM),
                  pl.BlockSpec((ROW_TILE,), lambda i: (jnp.minimum(blk0 + i + 1, last),), memory_space=pltpu.SMEM),
                  pl.BlockSpec((ROW_TILE, d), lambda i: (blk0 + i, 0)),
                  pl.BlockSpec((ROW_TILE, LANES), lambda i: (blk0 + i, 0)),
                  _const_spec((1, d)),
                  pl.BlockSpec(memory_space=pl.ANY)],
        out_specs=pl.BlockSpec((ROW_TILE, d), lambda i: (i, 0)),
        out_shape=jax.ShapeDtypeStruct((n_rows, d), F32),
        scratch_shapes=[pltpu.VMEM((2, TOP_K, ROW_TILE, dp), jnp.uint32), pltpu.SemaphoreType.DMA((2,))],
        compiler_params=_params(1),
        name="combine",
    )(pos0, pos1, pos0, pos1, h, wt, gf, y_sorted)


def _bias_base(rel_bias, n_queries, key_offset):
    m = jnp.arange(BIAS_BASE)
    dist = (n_queries - 1) - m + key_offset
    idx = jnp.clip(dist, -MAX_REL, MAX_REL) + MAX_REL
    return rel_bias[:, idx].astype(F32)


def _block_diag(w):
    nb, bw, _ = w.shape
    eye = jnp.eye(nb, dtype=w.dtype)
    return jnp.einsum("ncd,nm->ncmd", w, eye).reshape(nb * bw, nb * bw)


def kernel(x_prompt, x_sample, cache_k, cache_v, state_conv, state_h, norm1_g, w_in, b_in, rel_bias, w_att_branch, conv_w, conv_b, w_rec_gate, b_rec_gate, w_in_gate, b_in_gate, lru_lambda, w_rnn_branch, w_out, norm2_g, w_router_group, b_router_group, w_router_expert, b_router_expert, w_e_gate, w_e_up, w_e_down, final_norm_g):
    batch, seq, d = x_prompt.shape
    dec_b, dec_t, _ = x_sample.shape
    depth = w_in.shape[0]
    assert depth == 1
    att_w = N_HEADS * HEAD_DIM
    rnn_w = conv_w.shape[2]
    cache_len = cache_k.shape[2]
    n_p = batch * seq
    n_s = dec_b * dec_t
    n = n_p + n_s
    keep = min(BAND_PAST + CHUNK, seq)
    row2 = lambda a: a.reshape(1, -1).astype(F32)

    tiles_per_seq = seq // ROW_TILE
    tail_tiles = -(-keep // ROW_TILE)
    state_tiles = tuple(b * tiles_per_seq + tiles_per_seq - tail_tiles + j
                        for b in range(batch) for j in range(tail_tiles)) + (n_p // ROW_TILE,)

    xp = x_prompt.reshape(n_p, d)
    xs = x_sample.reshape(n_s, d)
    rnn_args = (conv_w[0].astype(F32), row2(conv_b[0]),
                _block_diag(w_rec_gate[0]).astype(BF16), row2(b_rec_gate[0]),
                _block_diag(w_in_gate[0]).astype(BF16), row2(b_in_gate[0]), row2(lru_lambda[0]))
    q, k, v, kvt, ga, gr, rnn_p, conv_p, h_p, rx_s, rgate_s = _proj(
        xp, xs, row2(norm1_g[0]), w_in[0].astype(BF16), row2(b_in[0]), rnn_args, att_w, rnn_w, state_tiles, batch)

    att_p = _attn_prompt(q, k, v, _bias_base(rel_bias[0], Q_BLOCK, (KEY_BLOCKS - 1) * Q_BLOCK), batch, seq)
    att_s, new_k_s, new_v_s = _attn_sample(
        q, k, v, kvt, cache_k[0].reshape(dec_b, cache_len, att_w), cache_v[0].reshape(dec_b, cache_len, att_w),
        _bias_base(rel_bias[0], dec_t, cache_len), n_p, batch * tail_tiles * ROW_TILE, dec_b, dec_t)

    pad_state = lambda s: jnp.pad(s.astype(F32), ((0, 0), (SUBLANES - (CONV_W - 1), 0), (0, 0)))
    rnn_s, conv_s, h_s = _rnn(rx_s, rgate_s, pad_state(state_conv[0]), state_h[0].astype(F32)[:, None, :],
                              *rnn_args, 0, dec_b, dec_t, dec_t)

    w_r = jnp.zeros((d, LANES), F32).at[:, :N_GROUPS].set(w_router_group[0])
    w_r = w_r.at[:, N_GROUPS:N_GROUPS + N_EXPERTS].set(w_router_expert[0]).astype(BF16)
    b_r = jnp.zeros((1, LANES), F32).at[0, :N_GROUPS].set(b_router_group[0])
    b_r = b_r.at[0, N_GROUPS:N_GROUPS + N_EXPERTS].set(b_router_expert[0])
    h, hn, route, wt, cnt = _merge(xp, xs, att_p, att_s, rnn_p, rnn_s, ga, gr,
                                   w_att_branch[0].astype(BF16), w_rnn_branch[0].astype(BF16), w_out[0].astype(BF16),
                                   row2(norm2_g[0]), w_r, b_r)

    counts = cnt[0, :N_EXPERTS].astype(jnp.int32)
    padded = (counts + EXPERT_BLOCK - 1) // EXPERT_BLOCK * EXPERT_BLOCK
    pad_end = jnp.cumsum(padded)
    pad_start = pad_end - padded
    expert = route[0:TOP_K].astype(jnp.int32)
    rank = route[TOP_K:2 * TOP_K].astype(jnp.int32)
    ids = jnp.arange(N_EXPERTS, dtype=jnp.int32)[:, None, None]
    pos = jnp.sum(jnp.where(expert[None] == ids, pad_start[:, None, None], 0), axis=0) + rank
    n_blocks = -(-(n * TOP_K) // EXPERT_BLOCK) + N_EXPERTS
    rows = n_blocks * EXPERT_BLOCK
    block_start = jnp.arange(n_blocks, dtype=jnp.int32) * EXPERT_BLOCK
    block_expert = jnp.minimum(jnp.sum((block_start[:, None] >= pad_end[None, :]).astype(jnp.int32), axis=1),
                               N_EXPERTS - 1)
    n_used = (pad_end[-1:] // EXPERT_BLOCK).astype(jnp.int32)

    x_sorted = _dispatch(pad_end, padded, pos[0], pos[1], hn, rows)
    y_sorted = _experts(block_expert, n_used, x_sorted, w_e_gate[0], w_e_up[0], w_e_down[0])
    gf = row2(final_norm_g)
    y_p = _combine(pos[0], pos[1], h, wt, gf, y_sorted, 0, n_p)
    y_s = _combine(pos[0], pos[1], h, wt, gf, y_sorted, n_p, n_s)

    def kv_state(col0):
        span = tail_tiles * ROW_TILE
        tails = [kvt[b * span + span - keep:(b + 1) * span, col0:col0 + att_w] for b in range(batch)]
        return jnp.stack(tails).reshape(1, batch, keep, N_HEADS, HEAD_DIM)

    heads = lambda a: a.reshape(1, dec_b, cache_len, N_HEADS, HEAD_DIM)
    return (y_p.reshape(batch, seq, d), y_s.reshape(dec_b, dec_t, d),
            kv_state(0), kv_state(att_w),
            conv_p[:, SUBLANES - (CONV_W - 1):][None], h_p[:, 0][None],
            heads(new_k_s), heads(new_v_s),
            conv_s[:, SUBLANES - (CONV_W - 1):][None], h_s[:, 0][None])
```

```python
import functools

import jax
import jax.numpy as jnp
from jax import lax
from jax.experimental import pallas as pl
from jax.experimental.pallas import tpu as pltpu

CHUNK = 64
LEFT_CHUNKS = 8
BAND_PAST = LEFT_CHUNKS * CHUNK
N_HEADS = 8
HEAD_DIM = 64
MAX_REL = 256
NEG_INF = -1e30
RNN_BLOCKS = 8
CONV_W = 4
RG_C = 8.0
N_GROUPS = 4
EXPERTS_PER_GROUP = 8
N_EXPERTS = N_GROUPS * EXPERTS_PER_GROUP
TOP_K = 2
EPS = 1e-6

LANES = 128
SUBLANES = 8
ROW_TILE = 512
Q_BLOCK = 256
KEY_BLOCKS = 3
KEY_WINDOW = KEY_BLOCKS * Q_BLOCK
BIAS_BASE = 1024
EXPERT_BLOCK = 512
VMEM_LIMIT = 56 * 1024 * 1024

BF16 = jnp.bfloat16
F32 = jnp.float32


def _params(n_axes, vmem=VMEM_LIMIT):
    return pltpu.CompilerParams(dimension_semantics=("arbitrary",) * n_axes, vmem_limit_bytes=vmem)


def _const_spec(shape):
    zeros = (0,) * len(shape)
    return pl.BlockSpec(shape, lambda *_: zeros, pipeline_mode=pl.Buffered(1))


def _rmsnorm(x, g):
    return x * lax.rsqrt(jnp.mean(x * x, axis=-1, keepdims=True) + EPS) * g


def _pack_bf16_halves(x):
    c = x.shape[1] // 2
    lo = lax.bitcast_convert_type(x[:, :c].astype(BF16).astype(F32), jnp.uint32)
    hi = lax.bitcast_convert_type(x[:, c:].astype(BF16).astype(F32), jnp.uint32)
    return (lo >> 16) | (hi & jnp.uint32(0xFFFF0000))


def _unpack_bf16_halves(p):
    lo = lax.bitcast_convert_type(p << 16, F32)
    hi = lax.bitcast_convert_type(p & jnp.uint32(0xFFFF0000), F32)
    return lo, hi


def _sigmoid(x):
    return 0.5 * jnp.tanh(0.5 * x) + 0.5


def _proj_kernel(n_prompt_tiles, tiles_per_seq, state_tiles, att_w, rnn_w, d_model,
                 xp_ref, xs_ref, g_ref, w_ref, b_ref, cw_ref, cb_ref, wa_ref, ba_ref, wx_ref, bx_ref, lam_ref,
                 q_ref, k_ref, v_ref, kvt_ref, ga_ref, gr_ref, rnn_ref, conv_out_ref, h_out_ref, rxs_ref, rgs_ref,
                 rx_ref, tail_ref, hl_ref, p_ref, hs_ref, h_ref):
    i = pl.program_id(0)
    tm = xp_ref.shape[0]
    x = jnp.where(i < n_prompt_tiles, xp_ref[...], xs_ref[...])
    xn = _rmsnorm(x, g_ref[...]).astype(BF16)

    def seg(lo, width):
        return jnp.dot(xn, w_ref[:, lo:lo + width], preferred_element_type=F32) + b_ref[:, lo:lo + width]

    q = seg(0, att_w)
    k = seg(att_w, att_w)
    v = seg(2 * att_w, att_w)
    q_ref[...] = (q * (HEAD_DIM ** -0.5)).astype(BF16)
    k_ref[...] = k.astype(BF16)
    v_ref[...] = v.astype(BF16)

    @pl.when(functools.reduce(jnp.logical_or, [i == t for t in state_tiles]))
    def _():
        kvt_ref[:, :att_w] = k
        kvt_ref[:, att_w:] = v

    o = 3 * att_w
    ncol = rnn_w // LANES
    rx = seg(o, rnn_w)
    for c in range(ncol):
        rx_ref[c] = rx[:, c * LANES:(c + 1) * LANES]
    rg = seg(o + rnn_w, rnn_w)

    @pl.when(i % tiles_per_seq == 0)
    def _():
        tail_ref[...] = jnp.zeros_like(tail_ref)
        h_ref[...] = jnp.zeros_like(h_ref)

    hseq, tail, h_last = _rglru_prompt_tile(tm, rx_ref, cw_ref, cb_ref, wa_ref, ba_ref, wx_ref, bx_ref, lam_ref,
                                            tail_ref, hl_ref, p_ref, hs_ref, h_ref)
    rnn_ref[...] = (hseq * jax.nn.gelu(rg)).astype(BF16)
    ga_ref[...] = seg(o + 2 * rnn_w, d_model).astype(BF16)
    gr_ref[...] = seg(o + 2 * rnn_w + d_model, d_model).astype(BF16)

    @pl.when(i < n_prompt_tiles)
    def _():
        conv_out_ref[0] = tail
        h_out_ref[0] = h_last

    @pl.when(i == n_prompt_tiles)
    def _():
        for c in range(ncol):
            rxs_ref[c] = rx_ref[c]
        rgs_ref[...] = rg.astype(BF16)


def _proj(xp, xs, g1, w_in, b_in, rnn_args, att_w, rnn_w, state_tiles, batch):
    n_p, d = xp.shape
    n_s = xs.shape[0]
    assert n_p % ROW_TILE == 0 and n_s == ROW_TILE
    assert ROW_TILE % (SUBLANES * SUBLANES) == 0 and ROW_TILE // SUBLANES >= CONV_W
    n = n_p + n_s
    npt = n_p // ROW_TILE
    tps = npt // batch
    pw = w_in.shape[1]
    ncol = rnn_w // LANES
    row = lambda w: pl.BlockSpec((ROW_TILE, w), lambda i: (i, 0))
    rows = lambda w, dt: jax.ShapeDtypeStruct((n, w), dt)
    slot = lambda i: sum(jnp.where(i > t, 1, 0) for t in state_tiles)
    state = lambda r: pl.BlockSpec((1, r, rnn_w), lambda i: (jnp.minimum(i // tps, batch - 1), 0, 0))
    out_specs = [row(att_w), row(att_w), row(att_w),
                 pl.BlockSpec((ROW_TILE, 2 * att_w), lambda i: (slot(i), 0)),
                 row(d), row(d), row(rnn_w), state(SUBLANES), state(1),
                 pl.BlockSpec((ncol, ROW_TILE, LANES), lambda i: (0, 0, 0)),
                 pl.BlockSpec((ROW_TILE, rnn_w), lambda i: (0, 0))]
    out_shape = [rows(att_w, BF16), rows(att_w, BF16), rows(att_w, BF16),
                 jax.ShapeDtypeStruct((len(state_tiles) * ROW_TILE, 2 * att_w), F32),
                 rows(d, BF16), rows(d, BF16), rows(rnn_w, BF16),
                 jax.ShapeDtypeStruct((batch, SUBLANES, rnn_w), F32), jax.ShapeDtypeStruct((batch, 1, rnn_w), F32),
                 jax.ShapeDtypeStruct((ncol, n_s, LANES), F32), jax.ShapeDtypeStruct((n_s, rnn_w), BF16)]
    tile = lambda: pltpu.VMEM((ROW_TILE, rnn_w), F32)
    cols = lambda: pltpu.VMEM((ncol, ROW_TILE, LANES), F32)
    return pl.pallas_call(
        functools.partial(_proj_kernel, npt, tps, state_tiles, att_w, rnn_w, d),
        grid=(n // ROW_TILE,),
        in_specs=[pl.BlockSpec((ROW_TILE, d), lambda i: (jnp.minimum(i, npt - 1), 0)),
                  pl.BlockSpec((ROW_TILE, d), lambda i: (0, 0)),
                  _const_spec((1, d)), _const_spec((d, pw)), _const_spec((1, pw))]
                 + [_const_spec(a.shape) for a in rnn_args],
        out_specs=out_specs,
        out_shape=out_shape,
        scratch_shapes=[cols(), pltpu.VMEM((SUBLANES, rnn_w), F32), tile(), tile(), cols(),
                        pltpu.VMEM((1, rnn_w), F32)],
        compiler_params=_params(1),
        name="proj_rglru",
    )(xp, xs, g1, w_in, b_in, *rnn_args)


def _attn_p_kernel(q_ref, k0_ref, k1_ref, k2_ref, v0_ref, v1_ref, v2_ref, base_ref, o_ref, bias_ref):
    b = pl.program_id(0)
    i = pl.program_id(1)

    @pl.when((b == 0) & (i == 0))
    def _():
        r = lax.broadcasted_iota(jnp.int32, (Q_BLOCK, KEY_WINDOW), 0) // CHUNK
        c = lax.broadcasted_iota(jnp.int32, (Q_BLOCK, KEY_WINDOW), 1) // CHUNK
        in_band = (c >= r) & (c <= r + LEFT_CHUNKS)
        for h in range(N_HEADS):
            rows = jnp.broadcast_to(base_ref[h:h + 1, :], (Q_BLOCK, BIAS_BASE))
            toeplitz = pltpu.roll(rows, BIAS_BASE - (Q_BLOCK - 1), axis=1, stride=1, stride_axis=0)
            bias_ref[h] = jnp.where(in_band, toeplitz[:, :KEY_WINDOW], NEG_INF)

    kwin = jnp.concatenate([k0_ref[...], k1_ref[...], k2_ref[...]], axis=0)
    vwin = jnp.concatenate([v0_ref[...], v1_ref[...], v2_ref[...]], axis=0)
    lane = lax.broadcasted_iota(jnp.int32, (1, LANES), 1)

    def heads(at_sequence_start):
        if at_sequence_start:
            col = lax.broadcasted_iota(jnp.int32, (1, KEY_WINDOW), 1)
            start_mask = jnp.where(col // Q_BLOCK + i - (KEY_BLOCKS - 1) >= 0, 0.0, NEG_INF).astype(F32)
        for pair in range(N_HEADS // 2):
            sl = slice(pair * LANES, (pair + 1) * LANES)
            q2 = q_ref[:, sl]
            k2 = kwin[:, sl]
            v2 = vwin[:, sl]
            acc = jnp.zeros((Q_BLOCK, LANES), F32)
            for half in range(2):
                hmask = (lane // HEAD_DIM) == half
                qm = jnp.where(hmask, q2, jnp.zeros_like(q2))
                vm = jnp.where(hmask, v2, jnp.zeros_like(v2))
                s = lax.dot_general(qm, k2, (((1,), (1,)), ((), ())), preferred_element_type=F32)
                s = s + bias_ref[2 * pair + half]
                if at_sequence_start:
                    s = s + start_mask
                m = jnp.max(s, axis=-1, keepdims=True)
                e = jnp.exp(s - m)
                l = jnp.sum(e, axis=-1, keepdims=True)
                o = jnp.dot(e.astype(BF16), vm, preferred_element_type=F32)
                acc = acc + o / l
            o_ref[:, sl] = acc.astype(BF16)

    pl.when(i < KEY_BLOCKS - 1)(lambda: heads(True))
    pl.when(i >= KEY_BLOCKS - 1)(lambda: heads(False))


def _attn_prompt(q, k, v, base, batch, seq):
    w = q.shape[1]
    nqb = seq // Q_BLOCK
    kspec = lambda back: pl.BlockSpec((Q_BLOCK, w), lambda b, i: (b * nqb + jnp.maximum(i - back, 0), 0))
    return pl.pallas_call(
        _attn_p_kernel,
        grid=(batch, nqb),
        in_specs=[pl.BlockSpec((Q_BLOCK, w), lambda b, i: (b * nqb + i, 0)),
                  kspec(2), kspec(1), kspec(0), kspec(2), kspec(1), kspec(0),
                  _const_spec(base.shape)],
        out_specs=pl.BlockSpec((Q_BLOCK, w), lambda b, i: (b * nqb + i, 0)),
        out_shape=jax.ShapeDtypeStruct((batch * seq, w), BF16),
        scratch_shapes=[pltpu.VMEM((N_HEADS, Q_BLOCK, KEY_WINDOW), F32)],
        compiler_params=_params(2),
        name="attn_prompt",
    )(q, k, k, k, v, v, v, base)


def _attn_s_kernel(t, q_ref, k_ref, v_ref, kvt_ref, ck_ref, cv_ref, base_ref, o_ref, nk_ref, nv_ref, bias_ref):
    cache_len = ck_ref.shape[1]
    w = q_ref.shape[1]

    @pl.when(pl.program_id(0) == 0)
    def _():
        col = lax.broadcasted_iota(jnp.int32, (t, cache_len + t), 1)
        for h in range(N_HEADS):
            rows = jnp.broadcast_to(base_ref[h:h + 1, :], (t, BIAS_BASE))
            toeplitz = pltpu.roll(rows, BIAS_BASE - (t - 1), axis=1, stride=1, stride_axis=0)
            bias_ref[h * t:(h + 1) * t, :] = jnp.where(col >= cache_len - BAND_PAST, toeplitz[:, :cache_len + t],
                                                       NEG_INF)

    ck = ck_ref[0]
    cv = cv_ref[0]
    kall = jnp.concatenate([ck.astype(BF16), k_ref[...]], axis=0)
    vall = jnp.concatenate([cv.astype(BF16), v_ref[...]], axis=0)
    q = q_ref[...]
    head_of_lane = lax.broadcasted_iota(jnp.int32, (1, w), 1) // HEAD_DIM
    qs = jnp.concatenate([jnp.where(head_of_lane == h, q, jnp.zeros_like(q)) for h in range(N_HEADS)], axis=0)
    s = lax.dot_general(qs, kall, (((1,), (1,)), ((), ())), preferred_element_type=F32) + bias_ref[...]
    m = jnp.max(s, axis=-1, keepdims=True)
    e = jnp.exp(s - m)
    l = jnp.sum(e, axis=-1, keepdims=True)
    o_all = jnp.dot(e.astype(BF16), vall, preferred_element_type=F32) / l
    o = jnp.zeros((t, w), F32)
    for h in range(N_HEADS):
        o = o + jnp.where(head_of_lane == h, o_all[h * t:(h + 1) * t], 0.0)
    o_ref[...] = o.astype(BF16)
    nk_ref[0, :cache_len - t, :] = ck[t:]
    nk_ref[0, cache_len - t:, :] = kvt_ref[:, :w]
    nv_ref[0, :cache_len - t, :] = cv[t:]
    nv_ref[0, cache_len - t:, :] = kvt_ref[:, w:]


def _attn_sample(q, k, v, kvt, cache_k, cache_v, base, row0, kvt_row0, dec_b, t):
    w = q.shape[1]
    cache_len = cache_k.shape[1]
    assert row0 % t == 0 and kvt_row0 % t == 0 and cache_len + 2 * t - 1 <= BIAS_BASE
    tok = lambda width, r0: pl.BlockSpec((t, width), lambda b: (r0 // t + b, 0))
    cache = pl.BlockSpec((1, cache_len, w), lambda b: (b, 0, 0))
    return pl.pallas_call(
        functools.partial(_attn_s_kernel, t),
        grid=(dec_b,),
        in_specs=[tok(w, row0), tok(w, row0), tok(w, row0), tok(2 * w, kvt_row0), cache, cache,
                  _const_spec(base.shape)],
        out_specs=[pl.BlockSpec((t, w), lambda b: (b, 0)), cache, cache],
        out_shape=[jax.ShapeDtypeStruct((dec_b * t, w), BF16),
                   jax.ShapeDtypeStruct(cache_k.shape, F32), jax.ShapeDtypeStruct(cache_v.shape, F32)],
        scratch_shapes=[pltpu.VMEM((N_HEADS * t, cache_len + t), F32)],
        compiler_params=_params(1),
        name="attn_sample",
    )(q, k, v, kvt, cache_k, cache_v, base)


def _rnn_sample_kernel(t, rx_ref, rg_ref, conv0_ref, h0_ref, cw_ref, cb_ref, wa_ref, ba_ref, wx_ref, bx_ref, lam_ref,
                       rnn_ref, conv_out_ref, h_out_ref, ext_ref, hs_ref):
    streams = conv0_ref.shape[0]
    groups = t // SUBLANES
    rx = jnp.concatenate([rx_ref[c] for c in range(rx_ref.shape[0])], axis=1)
    cw = cw_ref[...]
    xc = []
    for s in range(streams):
        ext_ref[s, :SUBLANES, :] = conv0_ref[s]
        ext_ref[s, SUBLANES:, :] = rx[s * t:(s + 1) * t]
        acc = cb_ref[...]
        for tap in range(CONV_W):
            off = SUBLANES - (CONV_W - 1) + tap
            acc = acc + ext_ref[s, off:off + t, :] * cw[tap:tap + 1, :]
        xc.append(acc)
        conv_out_ref[s] = ext_ref[s, t:t + SUBLANES, :]

    a, b = _rglru_coeffs(jnp.concatenate(xc, axis=0), wa_ref, ba_ref, wx_ref, bx_ref, lam_ref)

    sub = lax.broadcasted_iota(jnp.int32, a.shape, 0) % SUBLANES
    shift = 1
    while shift < SUBLANES:
        keep = sub >= shift
        a_prev = pltpu.roll(a, shift, axis=0)
        b_prev = pltpu.roll(b, shift, axis=0)
        b = jnp.where(keep, a * b_prev + b, b)
        a = jnp.where(keep, a * a_prev, a)
        shift *= 2

    for s in range(streams):
        h = h0_ref[s]
        for g in range(groups):
            rows = slice(s * t + g * SUBLANES, s * t + (g + 1) * SUBLANES)
            hg = b[rows] + a[rows] * h
            hs_ref[rows, :] = hg
            h = hg[SUBLANES - 1:SUBLANES, :]
        h_out_ref[s] = h
    rnn_ref[...] = (hs_ref[...] * jax.nn.gelu(rg_ref[...].astype(F32))).astype(BF16)


def _rnn_sample(rx, rgate, conv0, h0, cw, cb, wa, ba, wx, bx, lam, streams, t):
    n, w = rgate.shape
    assert n == streams * t and t % SUBLANES == 0 and rx.shape == (w // LANES, n, LANES)
    full = lambda shape: pl.BlockSpec(shape, lambda i: (0,) * len(shape))
    return pl.pallas_call(
        functools.partial(_rnn_sample_kernel, t),
        grid=(1,),
        in_specs=[full(rx.shape), full((n, w)), full((streams, SUBLANES, w)), full((streams, 1, w)),
                  full(cw.shape), full((1, w)), full((w, w)), full((1, w)), full((w, w)), full((1, w)), full((1, w))],
        out_specs=[full((n, w)), full((streams, SUBLANES, w)), full((streams, 1, w))],
        out_shape=[jax.ShapeDtypeStruct((n, w), BF16),
                   jax.ShapeDtypeStruct((streams, SUBLANES, w), F32),
                   jax.ShapeDtypeStruct((streams, 1, w), F32)],
        scratch_shapes=[pltpu.VMEM((streams, t + SUBLANES, w), F32), pltpu.VMEM((n, w), F32)],
        compiler_params=_params(1),
        name="rnn_sample",
    )(rx, rgate, conv0, h0, cw, cb, wa, ba, wx, bx, lam)


def _rglru_coeffs(xc, wa_ref, ba_ref, wx_ref, bx_ref, lam_ref):
    xb = xc.astype(BF16)
    r = _sigmoid(jnp.dot(xb, wa_ref[...], preferred_element_type=F32) + ba_ref[...])
    gi = _sigmoid(jnp.dot(xb, wx_ref[...], preferred_element_type=F32) + bx_ref[...])
    neg_lam = -lam_ref[...]
    softplus = jnp.maximum(neg_lam, 0.0) + jnp.log1p(jnp.exp(-jnp.abs(neg_lam)))
    log_a = (-RG_C) * r * softplus
    a = jnp.exp(log_a)
    b = jnp.sqrt(-jnp.tanh(log_a) * (a * a + 1.0)) * (gi * xc)
    return a, b


def _rglru_prompt_tile(tm, rx_ref, cw_ref, cb_ref, wa_ref, ba_ref, wx_ref, bx_ref, lam_ref,
                       tail_ref, hl_ref, p_ref, hs_ref, h_ref):
    seg = tm // SUBLANES
    ncol = rx_ref.shape[0]
    w = ncol * LANES

    def strided_rows(ref, j):
        return jnp.concatenate([ref[c, pl.ds(j, SUBLANES, stride=seg), :] for c in range(ncol)], axis=1)

    slab = [strided_rows(rx_ref, j) for j in range(seg)]
    sub = lax.broadcasted_iota(jnp.int32, (SUBLANES, w), 0)

    def before(k):
        carry = jnp.broadcast_to(tail_ref[SUBLANES - k:SUBLANES - k + 1, :], (SUBLANES, w))
        return jnp.where(sub == 0, carry, pltpu.roll(slab[seg - k], 1, axis=0))

    lead = {k: before(k) for k in range(1, CONV_W)}
    cw = cw_ref[...]
    cb = cb_ref[...]
    xc = []
    for j in range(seg):
        acc = cb + slab[j] * cw[CONV_W - 1:CONV_W, :]
        for k in range(1, CONV_W):
            prev = slab[j - k] if j >= k else lead[k - j]
            acc = acc + prev * cw[CONV_W - 1 - k:CONV_W - k, :]
        xc.append(acc)
    tail = jnp.concatenate([rx_ref[c, tm - SUBLANES:tm, :] for c in range(ncol)], axis=1)
    tail_ref[...] = tail

    a, b = _rglru_coeffs(jnp.concatenate(xc, axis=0), wa_ref, ba_ref, wx_ref, bx_ref, lam_ref)

    h = jnp.zeros((SUBLANES, w), F32)
    p = jnp.ones((SUBLANES, w), F32)
    for j in range(seg):
        rows = slice(j * SUBLANES, (j + 1) * SUBLANES)
        h = a[rows] * h + b[rows]
        p = a[rows] * p
        hl_ref[rows, :] = h
        p_ref[rows, :] = p
    c = h_ref[...]
    enter = []
    for s in range(SUBLANES):
        enter.append(c)
        c = h[s:s + 1, :] + p[s:s + 1, :] * c
    last = c
    h_ref[...] = last
    enter = jnp.concatenate(enter, axis=0)
    for j in range(seg):
        rows = slice(j * SUBLANES, (j + 1) * SUBLANES)
        hj = hl_ref[rows, :] + p_ref[rows, :] * enter
        for col in range(ncol):
            hs_ref[col, pl.ds(j, SUBLANES, stride=seg), :] = hj[:, col * LANES:(col + 1) * LANES]
    hseq = jnp.concatenate([hs_ref[col] for col in range(ncol)], axis=1)
    return hseq, tail, last


ROUTE_ROWS = SUBLANES
MERGE_SPLIT = 1


def _merge_kernel(n_prompt_tiles,
                  xp_ref, xs_ref, attp_ref, atts_ref, rnnp_ref, rnns_ref, ga_ref, gr_ref,
                  watt_ref, wrnn_ref, wout_ref, g2_ref, wr_ref, br_ref,
                  h_ref, hn_ref, route_ref, wt_ref, cnt_ref, carry_ref):
    i = pl.program_id(0)
    is_p = i < n_prompt_tiles

    @pl.when(i == 0)
    def _():
        carry_ref[...] = jnp.zeros_like(carry_ref)

    carry = carry_ref[...]
    for g in range(MERGE_SPLIT):
        carry = _merge_rows(is_p, g * (xp_ref.shape[0] // MERGE_SPLIT), xp_ref.shape[0] // MERGE_SPLIT, carry,
                            xp_ref, xs_ref, attp_ref, atts_ref, rnnp_ref, rnns_ref, ga_ref, gr_ref,
                            watt_ref, wrnn_ref, wout_ref, g2_ref, wr_ref, br_ref,
                            h_ref, hn_ref, route_ref, wt_ref)
    carry_ref[...] = carry
    cnt_ref[...] = carry


def _merge_rows(is_p, r0, tm, carry,
                xp_ref, xs_ref, attp_ref, atts_ref, rnnp_ref, rnns_ref, ga_ref, gr_ref,
                watt_ref, wrnn_ref, wout_ref, g2_ref, wr_ref, br_ref,
                h_ref, hn_ref, route_ref, wt_ref):
    rows = slice(r0, r0 + tm)
    x = jnp.where(is_p, xp_ref[rows, :], xs_ref[rows, :])
    att = jnp.where(is_p, attp_ref[rows, :], atts_ref[rows, :])
    rnn = jnp.where(is_p, rnnp_ref[rows, :], rnns_ref[rows, :])
    ya = jnp.dot(att, watt_ref[...], preferred_element_type=F32)
    yr = jnp.dot(rnn, wrnn_ref[...], preferred_element_type=F32)
    mixed = _sigmoid(ga_ref[rows, :].astype(F32)) * ya + _sigmoid(gr_ref[rows, :].astype(F32)) * yr
    h = x + jnp.dot(mixed.astype(BF16), wout_ref[...], preferred_element_type=F32)
    h_ref[rows, :] = h
    hn = _rmsnorm(h, g2_ref[...])
    hn_ref[rows, :] = _pack_bf16_halves(hn)

    logits = jnp.dot(hn.astype(BF16), wr_ref[...], preferred_element_type=F32) + br_ref[...]
    lane = lax.broadcasted_iota(jnp.int32, (tm, LANES), 1)
    low = jnp.float32(-3e38)
    big = jnp.int32(1 << 20)
    lg = jnp.where(lane < N_GROUPS, logits, low)
    mg = jnp.max(lg, axis=-1, keepdims=True)
    p_grp = 1.0 / jnp.sum(jnp.exp(lg - mg), axis=-1, keepdims=True)
    grp = jnp.min(jnp.where(lg == mg, lane, big), axis=-1, keepdims=True)
    lo = N_GROUPS + EXPERTS_PER_GROUP * grp
    le = jnp.where((lane >= lo) & (lane < lo + EXPERTS_PER_GROUP), logits, low)
    m1 = jnp.max(le, axis=-1, keepdims=True)
    i1 = jnp.min(jnp.where(le == m1, lane, big), axis=-1, keepdims=True)
    le2 = jnp.where(lane == i1, low, le)
    m2 = jnp.max(le2, axis=-1, keepdims=True)
    i2 = jnp.min(jnp.where(le2 == m2, lane, big), axis=-1, keepdims=True)
    e2 = jnp.exp(m2 - m1)
    inv = 1.0 / (1.0 + e2)
    w1 = p_grp * inv
    w2 = p_grp * (e2 * inv)
    x1 = i1 - N_GROUPS
    x2 = i2 - N_GROUPS

    onehot = jnp.where((lane == x1) | (lane == x2), 1.0, 0.0).astype(BF16)
    rowi = lax.broadcasted_iota(jnp.int32, (tm, tm), 0)
    coli = lax.broadcasted_iota(jnp.int32, (tm, tm), 1)
    tri = jnp.where(coli < rowi, 1.0, 0.0).astype(BF16)
    before = jnp.dot(tri, onehot, preferred_element_type=F32) + carry
    r1 = jnp.sum(jnp.where(lane == x1, before, 0.0), axis=-1, keepdims=True)
    r2 = jnp.sum(jnp.where(lane == x2, before, 0.0), axis=-1, keepdims=True)

    fields = (x1.astype(F32), x2.astype(F32), r1, r2, w1, w2)
    slab = jnp.zeros((tm, LANES), F32)
    for idx, val in enumerate(fields):
        slab = jnp.where(lane == idx, val, slab)
    route_ref[:, rows] = slab.T[:ROUTE_ROWS, :]
    wt_ref[rows, :] = jnp.where(lane == 0, w1, jnp.where(lane == 1, w2, 0.0))
    return carry + jnp.sum(onehot.astype(F32), axis=0, keepdims=True)


def _merge(xp, xs, att_p, att_s, rnn_p, rnn_s, ga, gr, w_att, w_rnn, w_out, g2, w_r, b_r):
    n_p, d = xp.shape
    n = ga.shape[0]
    npt = n_p // ROW_TILE
    aw = att_p.shape[1]
    rw = rnn_p.shape[1]
    pspec = lambda w: pl.BlockSpec((ROW_TILE, w), lambda i: (jnp.minimum(i, npt - 1), 0))
    sspec = lambda w: pl.BlockSpec((ROW_TILE, w), lambda i: (0, 0))
    row = lambda w: pl.BlockSpec((ROW_TILE, w), lambda i: (i, 0))
    return pl.pallas_call(
        functools.partial(_merge_kernel, npt),
        grid=(n // ROW_TILE,),
        in_specs=[pspec(d), sspec(d), pspec(aw), sspec(aw), pspec(rw), sspec(rw), row(d), row(d),
                  _const_spec(w_att.shape), _const_spec(w_rnn.shape), _const_spec(w_out.shape),
                  _const_spec((1, d)), _const_spec(w_r.shape), _const_spec((1, LANES))],
        out_specs=[row(d), row(d // 2), pl.BlockSpec((ROUTE_ROWS, ROW_TILE), lambda i: (0, i)), row(LANES),
                   pl.BlockSpec((1, LANES), lambda i: (0, 0))],
        out_shape=[jax.ShapeDtypeStruct((n, d), F32), jax.ShapeDtypeStruct((n, d // 2), jnp.uint32),
                   jax.ShapeDtypeStruct((ROUTE_ROWS, n), F32), jax.ShapeDtypeStruct((n, LANES), F32),
                   jax.ShapeDtypeStruct((1, LANES), F32)],
        scratch_shapes=[pltpu.VMEM((1, LANES), F32)],
        compiler_params=_params(1),
        name="merge",
    )(xp, xs, att_p, att_s, rnn_p, rnn_s, ga, gr, w_att, w_rnn, w_out, g2, w_r, b_r)


def _row_copy(src_ref, src_row, dst_ref, dst_row, sem):
    return pltpu.make_async_copy(src_ref.at[pl.ds(src_row, 1)], dst_ref.at[pl.ds(dst_row, 1)], sem)


DISPATCH_SLOTS = 3


def _dispatch_kernel(n_tiles, pad_end_ref, padded_ref, pos0_ref, pos1_ref, hn_hbm_ref, xs_ref,
                     zero_ref, tile_ref, zero_sem, load_sem, scat_sem):
    i = pl.program_id(0)
    tm = tile_ref.shape[1]

    def load(tile, slot):
        rows = pl.ds(pl.multiple_of(tile * tm, tm), tm)
        return pltpu.make_async_copy(hn_hbm_ref.at[rows], tile_ref.at[slot], load_sem.at[slot])

    def scatter(slot):
        for t in range(tm):
            _row_copy(tile_ref.at[slot], t, xs_ref, pos0_ref[t], scat_sem.at[slot]).start(priority=0)
            _row_copy(tile_ref.at[slot], t, xs_ref, pos1_ref[t], scat_sem.at[slot]).start(priority=1)

    def drain_scatter(slot):
        for _ in range(TOP_K):
            pltpu.make_async_copy(tile_ref.at[slot], xs_ref.at[pl.ds(0, tm)], scat_sem.at[slot]).wait()

    @pl.when(i == 0)
    def _():
        load(0, 0).start()
        zero_ref[...] = jnp.zeros_like(zero_ref)

        def zero_block(start):
            start = pl.multiple_of(start, EXPERT_BLOCK)
            return pltpu.make_async_copy(zero_ref, xs_ref.at[pl.ds(start, EXPERT_BLOCK)], zero_sem)

        used_rows = pad_end_ref[N_EXPERTS - 1]
        total_rows = xs_ref.shape[0]
        for wait in (False, True):
            for e in range(N_EXPERTS):
                @pl.when(padded_ref[e] > 0)
                def _():
                    cp = zero_block(pad_end_ref[e] - EXPERT_BLOCK)
                    cp.wait() if wait else cp.start()

                @pl.when(used_rows + e * EXPERT_BLOCK < total_rows)
                def _():
                    cp = zero_block(used_rows + e * EXPERT_BLOCK)
                    cp.wait() if wait else cp.start()

    for slot in range(DISPATCH_SLOTS):
        @pl.when(((i + 1) % DISPATCH_SLOTS == slot) & (i + 1 < n_tiles))
        def _():
            pl.when(i >= DISPATCH_SLOTS - 1)(functools.partial(drain_scatter, slot))
            load(i + 1, slot).start()

    for slot in range(DISPATCH_SLOTS):
        @pl.when(i % DISPATCH_SLOTS == slot)
        def _():
            load(i, slot).wait()
            scatter(slot)

    @pl.when(i == n_tiles - 1)
    def _():
        for tile in range(max(0, n_tiles - DISPATCH_SLOTS), n_tiles):
            drain_scatter(tile % DISPATCH_SLOTS)


def _dispatch(pad_end, padded, pos0, pos1, hn, rows):
    n, d = hn.shape
    n_tiles = n // ROW_TILE
    grid_spec = pltpu.PrefetchScalarGridSpec(
        num_scalar_prefetch=2,
        grid=(n_tiles,),
        in_specs=[pl.BlockSpec((ROW_TILE,), lambda i, pe, pd: (i,), memory_space=pltpu.SMEM),
                  pl.BlockSpec((ROW_TILE,), lambda i, pe, pd: (i,), memory_space=pltpu.SMEM),
                  pl.BlockSpec(memory_space=pl.ANY)],
        out_specs=pl.BlockSpec(memory_space=pl.ANY),
        scratch_shapes=[pltpu.VMEM((EXPERT_BLOCK, d), hn.dtype), pltpu.VMEM((DISPATCH_SLOTS, ROW_TILE, d), hn.dtype),
                        pltpu.SemaphoreType.DMA(()), pltpu.SemaphoreType.DMA((DISPATCH_SLOTS,)),
                        pltpu.SemaphoreType.DMA((DISPATCH_SLOTS,))],
    )
    return pl.pallas_call(
        functools.partial(_dispatch_kernel, n_tiles),
        grid_spec=grid_spec,
        out_shape=jax.ShapeDtypeStruct((rows, d), hn.dtype),
        compiler_params=_params(1),
        name="dispatch",
    )(pad_end, padded, pos0, pos1, hn)


def _experts_kernel(be_ref, nu_ref, x_ref, wg_ref, wu_ref, wd_ref, y_ref, wg_bf, wu_bf, wd_bf):
    i = pl.program_id(0)
    used = i < nu_ref[0]

    @pl.when(used & ((i == 0) | (be_ref[i] != be_ref[jnp.maximum(i - 1, 0)])))
    def _():
        wg_bf[...] = wg_ref[0].astype(BF16)
        wu_bf[...] = wu_ref[0].astype(BF16)
        wd_bf[...] = wd_ref[0].astype(BF16)

    @pl.when(used)
    def _():
        half = wg_bf.shape[0] // 2
        x_lo, x_hi = (part.astype(BF16) for part in _unpack_bf16_halves(x_ref[...]))

        def up(w_bf):
            return (jnp.dot(x_lo, w_bf[:half, :], preferred_element_type=F32)
                    + jnp.dot(x_hi, w_bf[half:, :], preferred_element_type=F32))

        g = up(wg_bf)
        u = up(wu_bf)
        hid = (g * _sigmoid(g) * u).astype(BF16)
        y_ref[...] = _pack_bf16_halves(jnp.dot(hid, wd_bf[...], preferred_element_type=F32))

    @pl.when(jnp.logical_not(used))
    def _():
        y_ref[...] = jnp.zeros_like(y_ref)


def _experts(block_expert, n_used, xs, wg, wu, wd):
    rows, dp = xs.shape
    d = wg.shape[1]
    assert dp * 2 == d
    de = wg.shape[2]
    nb = rows // EXPERT_BLOCK
    last = lambda i, nu: jnp.minimum(i, nu[0] - 1)
    grid_spec = pltpu.PrefetchScalarGridSpec(
        num_scalar_prefetch=2,
        grid=(nb,),
        in_specs=[pl.BlockSpec((EXPERT_BLOCK, dp), lambda i, be, nu: (last(i, nu), 0)),
                  pl.BlockSpec((1, d, de), lambda i, be, nu: (be[last(i, nu)], 0, 0)),
                  pl.BlockSpec((1, d, de), lambda i, be, nu: (be[last(i, nu)], 0, 0)),
                  pl.BlockSpec((1, de, d), lambda i, be, nu: (be[last(i, nu)], 0, 0))],
        out_specs=pl.BlockSpec((EXPERT_BLOCK, dp), lambda i, be, nu: (i, 0)),
        scratch_shapes=[pltpu.VMEM((d, de), BF16), pltpu.VMEM((d, de), BF16), pltpu.VMEM((de, d), BF16)],
    )
    return pl.pallas_call(
        _experts_kernel,
        grid_spec=grid_spec,
        out_shape=jax.ShapeDtypeStruct((rows, dp), jnp.uint32),
        compiler_params=_params(1),
        name="experts",
    )(block_expert, n_used, xs, wg, wu, wd)


def _combine_kernel(n_tiles, pos0_ref, pos1_ref, next0_ref, next1_ref, h_ref, wt_ref, gf_ref, y_hbm_ref, out_ref,
                    buf_ref, sem):
    i = pl.program_id(0)
    tm = h_ref.shape[0]

    def request(p0_ref, p1_ref, slot):
        for t in range(tm):
            _row_copy(y_hbm_ref, p0_ref[t], buf_ref.at[slot, 0], t, sem.at[slot]).start(priority=0)
            _row_copy(y_hbm_ref, p1_ref[t], buf_ref.at[slot, 1], t, sem.at[slot]).start(priority=1)

    def consume(slot):
        for k in range(TOP_K):
            pltpu.make_async_copy(y_hbm_ref.at[pl.ds(0, tm)], buf_ref.at[slot, k], sem.at[slot]).wait()
        wt = wt_ref[...]
        lo0, hi0 = _unpack_bf16_halves(buf_ref[slot, 0])
        lo1, hi1 = _unpack_bf16_halves(buf_ref[slot, 1])
        w0 = wt[:, 0:1]
        w1 = wt[:, 1:2]
        moe = jnp.concatenate([lo0 * w0 + lo1 * w1, hi0 * w0 + hi1 * w1], axis=1)
        out_ref[...] = _rmsnorm(h_ref[...] + moe, gf_ref[...])

    pl.when(i == 0)(lambda: request(pos0_ref, pos1_ref, 0))
    for slot in range(2):
        pl.when((i + 1 < n_tiles) & ((i + 1) % 2 == slot))(functools.partial(request, next0_ref, next1_ref, slot))
    for slot in range(2):
        pl.when(i % 2 == slot)(functools.partial(consume, slot))


def _combine(pos0, pos1, h, wt, gf, y_sorted, row0, n_rows):
    d = h.shape[1]
    dp = y_sorted.shape[1]
    assert row0 % ROW_TILE == 0 and n_rows % ROW_TILE == 0
    blk0 = row0 // ROW_TILE
    n_tiles = n_rows // ROW_TILE
    last = blk0 + n_tiles - 1
    return pl.pallas_call(
        functools.partial(_combine_kernel, n_tiles),
        grid=(n_tiles,),
        in_specs=[pl.BlockSpec((ROW_TILE,), lambda i: (blk0 + i,), memory_space=pltpu.SMEM),
                  pl.BlockSpec((ROW_TILE,), lambda i: (blk0 + i,), memory_space=pltpu.SMEM),
                  pl.BlockSpec((ROW_TILE,), lambda i: (jnp.minimum(blk0 + i + 1, last),), memory_space=pltpu.SMEM),
                  pl.BlockSpec((ROW_TILE,), lambda i: (jnp.minimum(blk0 + i + 1, last),), memory_space=pltpu.SMEM),
                  pl.BlockSpec((ROW_TILE, d), lambda i: (blk0 + i, 0)),
                  pl.BlockSpec((ROW_TILE, LANES), lambda i: (blk0 + i, 0)),
                  _const_spec((1, d)),
                  pl.BlockSpec(memory_space=pl.ANY)],
        out_specs=pl.BlockSpec((ROW_TILE, d), lambda i: (i, 0)),
        out_shape=jax.ShapeDtypeStruct((n_rows, d), F32),
        scratch_shapes=[pltpu.VMEM((2, TOP_K, ROW_TILE, dp), jnp.uint32), pltpu.SemaphoreType.DMA((2,))],
        compiler_params=_params(1),
        name="combine",
    )(pos0, pos1, pos0, pos1, h, wt, gf, y_sorted)


def _bias_base(rel_bias, n_queries, key_offset):
    m = jnp.arange(BIAS_BASE)
    dist = (n_queries - 1) - m + key_offset
    idx = jnp.clip(dist, -MAX_REL, MAX_REL) + MAX_REL
    return rel_bias[:, idx].astype(F32)


def _block_diag(w):
    nb, bw, _ = w.shape
    eye = jnp.eye(nb, dtype=w.dtype)
    return jnp.einsum("ncd,nm->ncmd", w, eye).reshape(nb * bw, nb * bw)


def kernel(x_prompt, x_sample, cache_k, cache_v, state_conv, state_h, norm1_g, w_in, b_in, rel_bias, w_att_branch, conv_w, conv_b, w_rec_gate, b_rec_gate, w_in_gate, b_in_gate, lru_lambda, w_rnn_branch, w_out, norm2_g, w_router_group, b_router_group, w_router_expert, b_router_expert, w_e_gate, w_e_up, w_e_down, final_norm_g):
    batch, seq, d = x_prompt.shape
    dec_b, dec_t, _ = x_sample.shape
    depth = w_in.shape[0]
    assert depth == 1
    att_w = N_HEADS * HEAD_DIM
    rnn_w = conv_w.shape[2]
    cache_len = cache_k.shape[2]
    n_p = batch * seq
    n_s = dec_b * dec_t
    n = n_p + n_s
    keep = min(BAND_PAST + CHUNK, seq)
    row2 = lambda a: a.reshape(1, -1).astype(F32)

    tiles_per_seq = seq // ROW_TILE
    tail_tiles = -(-keep // ROW_TILE)
    state_tiles = tuple(b * tiles_per_seq + tiles_per_seq - tail_tiles + j
                        for b in range(batch) for j in range(tail_tiles)) + (n_p // ROW_TILE,)

    xp = x_prompt.reshape(n_p, d)
    xs = x_sample.reshape(n_s, d)
    rnn_args = (conv_w[0].astype(F32), row2(conv_b[0]),
                _block_diag(w_rec_gate[0]).astype(BF16), row2(b_rec_gate[0]),
                _block_diag(w_in_gate[0]).astype(BF16), row2(b_in_gate[0]), row2(lru_lambda[0]))
    q, k, v, kvt, ga, gr, rnn_p, conv_p, h_p, rx_s, rgate_s = _proj(
        xp, xs, row2(norm1_g[0]), w_in[0].astype(BF16), row2(b_in[0]), rnn_args, att_w, rnn_w, state_tiles, batch)

    att_p = _attn_prompt(q, k, v, _bias_base(rel_bias[0], Q_BLOCK, (KEY_BLOCKS - 1) * Q_BLOCK), batch, seq)
    att_s, new_k_s, new_v_s = _attn_sample(
        q, k, v, kvt, cache_k[0].reshape(dec_b, cache_len, att_w), cache_v[0].reshape(dec_b, cache_len, att_w),
        _bias_base(rel_bias[0], dec_t, cache_len), n_p, batch * tail_tiles * ROW_TILE, dec_b, dec_t)

    pad_state = lambda s: jnp.pad(s.astype(F32), ((0, 0), (SUBLANES - (CONV_W - 1), 0), (0, 0)))
    rnn_s, conv_s, h_s = _rnn_sample(rx_s, rgate_s, pad_state(state_conv[0]), state_h[0].astype(F32)[:, None, :],
                                     *rnn_args, dec_b, dec_t)

    w_r = jnp.zeros((d, LANES), F32).at[:, :N_GROUPS].set(w_router_group[0])
    w_r = w_r.at[:, N_GROUPS:N_GROUPS + N_EXPERTS].set(w_router_expert[0]).astype(BF16)
    b_r = jnp.zeros((1, LANES), F32).at[0, :N_GROUPS].set(b_router_group[0])
    b_r = b_r.at[0, N_GROUPS:N_GROUPS + N_EXPERTS].set(b_router_expert[0])
    h, hn, route, wt, cnt = _merge(xp, xs, att_p, att_s, rnn_p, rnn_s, ga, gr,
                                   w_att_branch[0].astype(BF16), w_rnn_branch[0].astype(BF16), w_out[0].astype(BF16),
                                   row2(norm2_g[0]), w_r, b_r)

    counts = cnt[0, :N_EXPERTS].astype(jnp.int32)
    padded = (counts + EXPERT_BLOCK - 1) // EXPERT_BLOCK * EXPERT_BLOCK
    pad_end = jnp.cumsum(padded)
    pad_start = pad_end - padded
    expert = route[0:TOP_K].astype(jnp.int32)
    rank = route[TOP_K:2 * TOP_K].astype(jnp.int32)
    ids = jnp.arange(N_EXPERTS, dtype=jnp.int32)[:, None, None]
    pos = jnp.sum(jnp.where(expert[None] == ids, pad_start[:, None, None], 0), axis=0) + rank
    n_blocks = -(-(n * TOP_K) // EXPERT_BLOCK) + N_EXPERTS
    rows = n_blocks * EXPERT_BLOCK
    block_start = jnp.arange(n_blocks, dtype=jnp.int32) * EXPERT_BLOCK
    block_expert = jnp.minimum(jnp.sum((block_start[:, None] >= pad_end[None, :]).astype(jnp.int32), axis=1),
                               N_EXPERTS - 1)
    n_used = (pad_end[-1:] // EXPERT_BLOCK).astype(jnp.int32)

    x_sorted = _dispatch(pad_end, padded, pos[0], pos[1], hn, rows)
    y_sorted = _experts(block_expert, n_used, x_sorted, w_e_gate[0], w_e_up[0], w_e_down[0])
    gf = row2(final_norm_g)
    y_p = _combine(pos[0], pos[1], h, wt, gf, y_sorted, 0, n_p)
    y_s = _combine(pos[0], pos[1], h, wt, gf, y_sorted, n_p, n_s)

    def kv_state(col0):
        span = tail_tiles * ROW_TILE
        tails = [kvt[b * span + span - keep:(b + 1) * span, col0:col0 + att_w] for b in range(batch)]
        return jnp.stack(tails).reshape(1, batch, keep, N_HEADS, HEAD_DIM)

    heads = lambda a: a.reshape(1, dec_b, cache_len, N_HEADS, HEAD_DIM)
    return (y_p.reshape(batch, seq, d), y_s.reshape(dec_b, dec_t, d),
            kv_state(0), kv_state(att_w),
            conv_p[:, SUBLANES - (CONV_W - 1):][None], h_p[:, 0][None],
            heads(new_k_s), heads(new_v_s),
            conv_s[:, SUBLANES - (CONV_W - 1):][None], h_s[:, 0][None])
```

```python
import functools

import jax
import jax.numpy as jnp
from jax import lax
from jax.experimental import pallas as pl
from jax.experimental.pallas import tpu as pltpu

CHUNK = 64
LEFT_CHUNKS = 8
BAND_PAST = LEFT_CHUNKS * CHUNK
N_HEADS = 8
HEAD_DIM = 64
MAX_REL = 256
NEG_INF = -1e30
RNN_BLOCKS = 8
CONV_W = 4
RG_C = 8.0
N_GROUPS = 4
EXPERTS_PER_GROUP = 8
N_EXPERTS = N_GROUPS * EXPERTS_PER_GROUP
TOP_K = 2
EPS = 1e-6

LANES = 128
SUBLANES = 8
ROW_TILE = 512
Q_BLOCK = 256
KEY_BLOCKS = 3
KEY_WINDOW = KEY_BLOCKS * Q_BLOCK
BIAS_BASE = 1024
EXPERT_BLOCK = 512
VMEM_LIMIT = 56 * 1024 * 1024

BF16 = jnp.bfloat16
F32 = jnp.float32


def _params(n_axes, vmem=VMEM_LIMIT):
    return pltpu.CompilerParams(dimension_semantics=("arbitrary",) * n_axes, vmem_limit_bytes=vmem)


def _const_spec(shape):
    zeros = (0,) * len(shape)
    return pl.BlockSpec(shape, lambda *_: zeros, pipeline_mode=pl.Buffered(1))


def _rmsnorm(x, g):
    return x * lax.rsqrt(jnp.mean(x * x, axis=-1, keepdims=True) + EPS) * g


def _pack_bf16_halves(x):
    c = x.shape[1] // 2
    lo = lax.bitcast_convert_type(x[:, :c].astype(BF16).astype(F32), jnp.uint32)
    hi = lax.bitcast_convert_type(x[:, c:].astype(BF16).astype(F32), jnp.uint32)
    return (lo >> 16) | (hi & jnp.uint32(0xFFFF0000))


def _unpack_bf16_halves(p):
    lo = lax.bitcast_convert_type(p << 16, F32)
    hi = lax.bitcast_convert_type(p & jnp.uint32(0xFFFF0000), F32)
    return lo, hi


def _fetch_row_tile(i, n_first, n_tiles, first_hbm_ref, second_hbm_ref, buf_ref, sem):
    tm = buf_ref.shape[1]

    def fetch(tile, slot):
        def start(src_ref, first_tile):
            rows = pl.ds(pl.multiple_of((tile - first_tile) * tm, tm), tm)
            pltpu.make_async_copy(src_ref.at[rows], buf_ref.at[slot], sem.at[slot]).start()

        pl.when(tile < n_first)(functools.partial(start, first_hbm_ref, 0))
        pl.when(tile >= n_first)(functools.partial(start, second_hbm_ref, n_first))

    slot = i % 2
    pl.when(i == 0)(lambda: fetch(i, slot))
    pl.when(i + 1 < n_tiles)(lambda: fetch(i + 1, 1 - slot))
    pltpu.make_async_copy(first_hbm_ref.at[pl.ds(0, tm)], buf_ref.at[slot], sem.at[slot]).wait()
    return slot


def _sigmoid(x):
    return 0.5 * jnp.tanh(0.5 * x) + 0.5


def _proj_kernel(n_prompt_tiles, n_tiles, tiles_per_seq, state_tiles, att_w, rnn_w, d_model,
                 xp_hbm_ref, xs_hbm_ref, g_ref, w_ref, b_ref,
                 cw_ref, cb_ref, wa_ref, ba_ref, wx_ref, bx_ref, lam_ref,
                 q_ref, k_ref, v_ref, kvt_ref, ga_ref, gr_ref, rnn_ref, conv_out_ref, h_out_ref, rxs_ref, rgs_ref,
                 rx_ref, tail_ref, hl_ref, p_ref, hs_ref, h_ref, x_ref, x_sem):
    i = pl.program_id(0)
    tm = x_ref.shape[1]
    slot = _fetch_row_tile(i, n_prompt_tiles, n_tiles, xp_hbm_ref, xs_hbm_ref, x_ref, x_sem)
    xn = _rmsnorm(x_ref[slot], g_ref[...]).astype(BF16)

    def seg(lo, width):
        return jnp.dot(xn, w_ref[:, lo:lo + width], preferred_element_type=F32) + b_ref[:, lo:lo + width]

    q = seg(0, att_w)
    k = seg(att_w, att_w)
    v = seg(2 * att_w, att_w)
    q_ref[...] = (q * (HEAD_DIM ** -0.5)).astype(BF16)
    k_ref[...] = k.astype(BF16)
    v_ref[...] = v.astype(BF16)

    @pl.when(functools.reduce(jnp.logical_or, [i == t for t in state_tiles]))
    def _():
        kvt_ref[:, :att_w] = k
        kvt_ref[:, att_w:] = v

    o = 3 * att_w
    ncol = rnn_w // LANES
    rx = seg(o, rnn_w)
    for c in range(ncol):
        rx_ref[c] = rx[:, c * LANES:(c + 1) * LANES]
    rg = seg(o + rnn_w, rnn_w)

    @pl.when(i % tiles_per_seq == 0)
    def _():
        tail_ref[...] = jnp.zeros_like(tail_ref)
        h_ref[...] = jnp.zeros_like(h_ref)

    hseq, tail, h_last = _rglru_prompt_tile(tm, rx_ref, cw_ref, cb_ref, wa_ref, ba_ref, wx_ref, bx_ref, lam_ref,
                                            tail_ref, hl_ref, p_ref, hs_ref, h_ref)
    rnn_ref[...] = (hseq * jax.nn.gelu(rg)).astype(BF16)
    ga_ref[...] = seg(o + 2 * rnn_w, d_model).astype(BF16)
    gr_ref[...] = seg(o + 2 * rnn_w + d_model, d_model).astype(BF16)

    @pl.when(i < n_prompt_tiles)
    def _():
        conv_out_ref[0] = tail
        h_out_ref[0] = h_last

    @pl.when(i == n_prompt_tiles)
    def _():
        for c in range(ncol):
            rxs_ref[c] = rx_ref[c]
        rgs_ref[...] = rg.astype(BF16)


def _proj(xp, xs, g1, w_in, b_in, rnn_args, att_w, rnn_w, state_tiles, batch):
    n_p, d = xp.shape
    n_s = xs.shape[0]
    assert n_p % ROW_TILE == 0 and n_s == ROW_TILE
    assert ROW_TILE % (SUBLANES * SUBLANES) == 0 and ROW_TILE // SUBLANES >= CONV_W
    n = n_p + n_s
    npt = n_p // ROW_TILE
    tps = npt // batch
    pw = w_in.shape[1]
    ncol = rnn_w // LANES
    row = lambda w: pl.BlockSpec((ROW_TILE, w), lambda i: (i, 0))
    rows = lambda w, dt: jax.ShapeDtypeStruct((n, w), dt)
    slot = lambda i: sum(jnp.where(i > t, 1, 0) for t in state_tiles)
    state = lambda r: pl.BlockSpec((1, r, rnn_w), lambda i: (jnp.minimum(i // tps, batch - 1), 0, 0))
    out_specs = [row(att_w), row(att_w), row(att_w),
                 pl.BlockSpec((ROW_TILE, 2 * att_w), lambda i: (slot(i), 0)),
                 row(d), row(d), row(rnn_w), state(SUBLANES), state(1),
                 pl.BlockSpec((ncol, ROW_TILE, LANES), lambda i: (0, 0, 0)),
                 pl.BlockSpec((ROW_TILE, rnn_w), lambda i: (0, 0))]
    out_shape = [rows(att_w, BF16), rows(att_w, BF16), rows(att_w, BF16),
                 jax.ShapeDtypeStruct((len(state_tiles) * ROW_TILE, 2 * att_w), F32),
                 rows(d, BF16), rows(d, BF16), rows(rnn_w, BF16),
                 jax.ShapeDtypeStruct((batch, SUBLANES, rnn_w), F32), jax.ShapeDtypeStruct((batch, 1, rnn_w), F32),
                 jax.ShapeDtypeStruct((ncol, n_s, LANES), F32), jax.ShapeDtypeStruct((n_s, rnn_w), BF16)]
    tile = lambda: pltpu.VMEM((ROW_TILE, rnn_w), F32)
    cols = lambda: pltpu.VMEM((ncol, ROW_TILE, LANES), F32)
    return pl.pallas_call(
        functools.partial(_proj_kernel, npt, n // ROW_TILE, tps, state_tiles, att_w, rnn_w, d),
        grid=(n // ROW_TILE,),
        in_specs=[pl.BlockSpec(memory_space=pl.ANY), pl.BlockSpec(memory_space=pl.ANY),
                  _const_spec((1, d)), _const_spec((d, pw)), _const_spec((1, pw))]
                 + [_const_spec(a.shape) for a in rnn_args],
        out_specs=out_specs,
        out_shape=out_shape,
        scratch_shapes=[cols(), pltpu.VMEM((SUBLANES, rnn_w), F32), tile(), tile(), cols(),
                        pltpu.VMEM((1, rnn_w), F32), pltpu.VMEM((2, ROW_TILE, d), F32),
                        pltpu.SemaphoreType.DMA((2,))],
        compiler_params=_params(1),
        name="proj_rglru",
    )(xp, xs, g1, w_in, b_in, *rnn_args)


def _attn_p_kernel(q_ref, k0_ref, k1_ref, k2_ref, v0_ref, v1_ref, v2_ref, base_ref, o_ref, bias_ref):
    b = pl.program_id(0)
    i = pl.program_id(1)

    @pl.when((b == 0) & (i == 0))
    def _():
        r = lax.broadcasted_iota(jnp.int32, (Q_BLOCK, KEY_WINDOW), 0) // CHUNK
        c = lax.broadcasted_iota(jnp.int32, (Q_BLOCK, KEY_WINDOW), 1) // CHUNK
        in_band = (c >= r) & (c <= r + LEFT_CHUNKS)
        for h in range(N_HEADS):
            rows = jnp.broadcast_to(base_ref[h:h + 1, :], (Q_BLOCK, BIAS_BASE))
            toeplitz = pltpu.roll(rows, BIAS_BASE - (Q_BLOCK - 1), axis=1, stride=1, stride_axis=0)
            bias_ref[h] = jnp.where(in_band, toeplitz[:, :KEY_WINDOW], NEG_INF)

    kwin = jnp.concatenate([k0_ref[...], k1_ref[...], k2_ref[...]], axis=0)
    vwin = jnp.concatenate([v0_ref[...], v1_ref[...], v2_ref[...]], axis=0)
    lane = lax.broadcasted_iota(jnp.int32, (1, LANES), 1)

    def heads(at_sequence_start):
        if at_sequence_start:
            col = lax.broadcasted_iota(jnp.int32, (1, KEY_WINDOW), 1)
            start_mask = jnp.where(col // Q_BLOCK + i - (KEY_BLOCKS - 1) >= 0, 0.0, NEG_INF).astype(F32)
        for pair in range(N_HEADS // 2):
            sl = slice(pair * LANES, (pair + 1) * LANES)
            q2 = q_ref[:, sl]
            k2 = kwin[:, sl]
            v2 = vwin[:, sl]
            acc = jnp.zeros((Q_BLOCK, LANES), F32)
            for half in range(2):
                hmask = (lane // HEAD_DIM) == half
                qm = jnp.where(hmask, q2, jnp.zeros_like(q2))
                vm = jnp.where(hmask, v2, jnp.zeros_like(v2))
                s = lax.dot_general(qm, k2, (((1,), (1,)), ((), ())), preferred_element_type=F32)
                s = s + bias_ref[2 * pair + half]
                if at_sequence_start:
                    s = s + start_mask
                m = jnp.max(s, axis=-1, keepdims=True)
                e = jnp.exp(s - m)
                l = jnp.sum(e, axis=-1, keepdims=True)
                o = jnp.dot(e.astype(BF16), vm, preferred_element_type=F32)
                acc = acc + o / l
            o_ref[:, sl] = acc.astype(BF16)

    pl.when(i < KEY_BLOCKS - 1)(lambda: heads(True))
    pl.when(i >= KEY_BLOCKS - 1)(lambda: heads(False))


def _attn_prompt(q, k, v, base, batch, seq):
    w = q.shape[1]
    nqb = seq // Q_BLOCK
    kspec = lambda back: pl.BlockSpec((Q_BLOCK, w), lambda b, i: (b * nqb + jnp.maximum(i - back, 0), 0))
    return pl.pallas_call(
        _attn_p_kernel,
        grid=(batch, nqb),
        in_specs=[pl.BlockSpec((Q_BLOCK, w), lambda b, i: (b * nqb + i, 0)),
                  kspec(2), kspec(1), kspec(0), kspec(2), kspec(1), kspec(0),
                  _const_spec(base.shape)],
        out_specs=pl.BlockSpec((Q_BLOCK, w), lambda b, i: (b * nqb + i, 0)),
        out_shape=jax.ShapeDtypeStruct((batch * seq, w), BF16),
        scratch_shapes=[pltpu.VMEM((N_HEADS, Q_BLOCK, KEY_WINDOW), F32)],
        compiler_params=_params(2),
        name="attn_prompt",
    )(q, k, k, k, v, v, v, base)


def _attn_s_kernel(t, q_ref, k_ref, v_ref, kvt_ref, ck_ref, cv_ref, base_ref, o_ref, nk_ref, nv_ref, bias_ref):
    cache_len = ck_ref.shape[1]
    w = q_ref.shape[1]

    @pl.when(pl.program_id(0) == 0)
    def _():
        col = lax.broadcasted_iota(jnp.int32, (t, cache_len + t), 1)
        for h in range(N_HEADS):
            rows = jnp.broadcast_to(base_ref[h:h + 1, :], (t, BIAS_BASE))
            toeplitz = pltpu.roll(rows, BIAS_BASE - (t - 1), axis=1, stride=1, stride_axis=0)
            bias_ref[h * t:(h + 1) * t, :] = jnp.where(col >= cache_len - BAND_PAST, toeplitz[:, :cache_len + t],
                                                       NEG_INF)

    ck = ck_ref[0]
    cv = cv_ref[0]
    kall = jnp.concatenate([ck.astype(BF16), k_ref[...]], axis=0)
    vall = jnp.concatenate([cv.astype(BF16), v_ref[...]], axis=0)
    q = q_ref[...]
    head_of_lane = lax.broadcasted_iota(jnp.int32, (1, w), 1) // HEAD_DIM
    qs = jnp.concatenate([jnp.where(head_of_lane == h, q, jnp.zeros_like(q)) for h in range(N_HEADS)], axis=0)
    s = lax.dot_general(qs, kall, (((1,), (1,)), ((), ())), preferred_element_type=F32) + bias_ref[...]
    m = jnp.max(s, axis=-1, keepdims=True)
    e = jnp.exp(s - m)
    l = jnp.sum(e, axis=-1, keepdims=True)
    o_all = jnp.dot(e.astype(BF16), vall, preferred_element_type=F32) / l
    o = jnp.zeros((t, w), F32)
    for h in range(N_HEADS):
        o = o + jnp.where(head_of_lane == h, o_all[h * t:(h + 1) * t], 0.0)
    o_ref[...] = o.astype(BF16)
    nk_ref[0, :cache_len - t, :] = ck[t:]
    nk_ref[0, cache_len - t:, :] = kvt_ref[:, :w]
    nv_ref[0, :cache_len - t, :] = cv[t:]
    nv_ref[0, cache_len - t:, :] = kvt_ref[:, w:]


def _attn_sample(q, k, v, kvt, cache_k, cache_v, base, row0, kvt_row0, dec_b, t):
    w = q.shape[1]
    cache_len = cache_k.shape[1]
    assert row0 % t == 0 and kvt_row0 % t == 0 and cache_len + 2 * t - 1 <= BIAS_BASE
    tok = lambda width, r0: pl.BlockSpec((t, width), lambda b: (r0 // t + b, 0))
    cache = pl.BlockSpec((1, cache_len, w), lambda b: (b, 0, 0))
    return pl.pallas_call(
        functools.partial(_attn_s_kernel, t),
        grid=(dec_b,),
        in_specs=[tok(w, row0), tok(w, row0), tok(w, row0), tok(2 * w, kvt_row0), cache, cache,
                  _const_spec(base.shape)],
        out_specs=[pl.BlockSpec((t, w), lambda b: (b, 0)), cache, cache],
        out_shape=[jax.ShapeDtypeStruct((dec_b * t, w), BF16),
                   jax.ShapeDtypeStruct(cache_k.shape, F32), jax.ShapeDtypeStruct(cache_v.shape, F32)],
        scratch_shapes=[pltpu.VMEM((N_HEADS * t, cache_len + t), F32)],
        compiler_params=_params(1),
        name="attn_sample",
    )(q, k, v, kvt, cache_k, cache_v, base)


def _rnn_sample_kernel(t, rx_ref, rg_ref, conv0_ref, h0_ref, cw_ref, cb_ref, wa_ref, ba_ref, wx_ref, bx_ref, lam_ref,
                       rnn_ref, conv_out_ref, h_out_ref, ext_ref, hs_ref):
    streams = conv0_ref.shape[0]
    groups = t // SUBLANES
    rx = jnp.concatenate([rx_ref[c] for c in range(rx_ref.shape[0])], axis=1)
    cw = cw_ref[...]
    xc = []
    for s in range(streams):
        ext_ref[s, :SUBLANES, :] = conv0_ref[s]
        ext_ref[s, SUBLANES:, :] = rx[s * t:(s + 1) * t]
        acc = cb_ref[...]
        for tap in range(CONV_W):
            off = SUBLANES - (CONV_W - 1) + tap
            acc = acc + ext_ref[s, off:off + t, :] * cw[tap:tap + 1, :]
        xc.append(acc)
        conv_out_ref[s] = ext_ref[s, t:t + SUBLANES, :]

    a, b = _rglru_coeffs(jnp.concatenate(xc, axis=0), wa_ref, ba_ref, wx_ref, bx_ref, lam_ref)

    sub = lax.broadcasted_iota(jnp.int32, a.shape, 0) % SUBLANES
    shift = 1
    while shift < SUBLANES:
        keep = sub >= shift
        a_prev = pltpu.roll(a, shift, axis=0)
        b_prev = pltpu.roll(b, shift, axis=0)
        b = jnp.where(keep, a * b_prev + b, b)
        a = jnp.where(keep, a * a_prev, a)
        shift *= 2

    for s in range(streams):
        h = h0_ref[s]
        for g in range(groups):
            rows = slice(s * t + g * SUBLANES, s * t + (g + 1) * SUBLANES)
            hg = b[rows] + a[rows] * h
            hs_ref[rows, :] = hg
            h = hg[SUBLANES - 1:SUBLANES, :]
        h_out_ref[s] = h
    rnn_ref[...] = (hs_ref[...] * jax.nn.gelu(rg_ref[...].astype(F32))).astype(BF16)


def _rnn_sample(rx, rgate, conv0, h0, cw, cb, wa, ba, wx, bx, lam, streams, t):
    n, w = rgate.shape
    assert n == streams * t and t % SUBLANES == 0 and rx.shape == (w // LANES, n, LANES)
    full = lambda shape: pl.BlockSpec(shape, lambda i: (0,) * len(shape))
    return pl.pallas_call(
        functools.partial(_rnn_sample_kernel, t),
        grid=(1,),
        in_specs=[full(rx.shape), full((n, w)), full((streams, SUBLANES, w)), full((streams, 1, w)),
                  full(cw.shape), full((1, w)), full((w, w)), full((1, w)), full((w, w)), full((1, w)), full((1, w))],
        out_specs=[full((n, w)), full((streams, SUBLANES, w)), full((streams, 1, w))],
        out_shape=[jax.ShapeDtypeStruct((n, w), BF16),
                   jax.ShapeDtypeStruct((streams, SUBLANES, w), F32),
                   jax.ShapeDtypeStruct((streams, 1, w), F32)],
        scratch_shapes=[pltpu.VMEM((streams, t + SUBLANES, w), F32), pltpu.VMEM((n, w), F32)],
        compiler_params=_params(1),
        name="rnn_sample",
    )(rx, rgate, conv0, h0, cw, cb, wa, ba, wx, bx, lam)


def _rglru_coeffs(xc, wa_ref, ba_ref, wx_ref, bx_ref, lam_ref):
    xb = xc.astype(BF16)
    r = _sigmoid(jnp.dot(xb, wa_ref[...], preferred_element_type=F32) + ba_ref[...])
    gi = _sigmoid(jnp.dot(xb, wx_ref[...], preferred_element_type=F32) + bx_ref[...])
    neg_lam = -lam_ref[...]
    softplus = jnp.maximum(neg_lam, 0.0) + jnp.log1p(jnp.exp(-jnp.abs(neg_lam)))
    log_a = (-RG_C) * r * softplus
    a = jnp.exp(log_a)
    b = jnp.sqrt(-jnp.tanh(log_a) * (a * a + 1.0)) * (gi * xc)
    return a, b


def _rglru_prompt_tile(tm, rx_ref, cw_ref, cb_ref, wa_ref, ba_ref, wx_ref, bx_ref, lam_ref,
                       tail_ref, hl_ref, p_ref, hs_ref, h_ref):
    seg = tm // SUBLANES
    ncol = rx_ref.shape[0]
    w = ncol * LANES

    def strided_rows(ref, j):
        return jnp.concatenate([ref[c, pl.ds(j, SUBLANES, stride=seg), :] for c in range(ncol)], axis=1)

    slab = [strided_rows(rx_ref, j) for j in range(seg)]
    sub = lax.broadcasted_iota(jnp.int32, (SUBLANES, w), 0)

    def before(k):
        carry = jnp.broadcast_to(tail_ref[SUBLANES - k:SUBLANES - k + 1, :], (SUBLANES, w))
        return jnp.where(sub == 0, carry, pltpu.roll(slab[seg - k], 1, axis=0))

    lead = {k: before(k) for k in range(1, CONV_W)}
    cw = cw_ref[...]
    cb = cb_ref[...]
    xc = []
    for j in range(seg):
        acc = cb + slab[j] * cw[CONV_W - 1:CONV_W, :]
        for k in range(1, CONV_W):
            prev = slab[j - k] if j >= k else lead[k - j]
            acc = acc + prev * cw[CONV_W - 1 - k:CONV_W - k, :]
        xc.append(acc)
    tail = jnp.concatenate([rx_ref[c, tm - SUBLANES:tm, :] for c in range(ncol)], axis=1)
    tail_ref[...] = tail

    a, b = _rglru_coeffs(jnp.concatenate(xc, axis=0), wa_ref, ba_ref, wx_ref, bx_ref, lam_ref)

    h = jnp.zeros((SUBLANES, w), F32)
    p = jnp.ones((SUBLANES, w), F32)
    for j in range(seg):
        rows = slice(j * SUBLANES, (j + 1) * SUBLANES)
        h = a[rows] * h + b[rows]
        p = a[rows] * p
        hl_ref[rows, :] = h
        p_ref[rows, :] = p
    c = h_ref[...]
    enter = []
    for s in range(SUBLANES):
        enter.append(c)
        c = h[s:s + 1, :] + p[s:s + 1, :] * c
    last = c
    h_ref[...] = last
    enter = jnp.concatenate(enter, axis=0)
    for j in range(seg):
        rows = slice(j * SUBLANES, (j + 1) * SUBLANES)
        hj = hl_ref[rows, :] + p_ref[rows, :] * enter
        for col in range(ncol):
            hs_ref[col, pl.ds(j, SUBLANES, stride=seg), :] = hj[:, col * LANES:(col + 1) * LANES]
    hseq = jnp.concatenate([hs_ref[col] for col in range(ncol)], axis=1)
    return hseq, tail, last


ROUTE_ROWS = SUBLANES


def _merge_kernel(n_prompt_tiles, n_tiles,
                  xp_hbm_ref, xs_hbm_ref, attp_ref, atts_ref, rnnp_ref, rnns_ref, ga_ref, gr_ref,
                  watt_ref, wrnn_ref, wout_ref, g2_ref, wr_ref, br_ref,
                  h_ref, hn_ref, route_ref, wt_ref, cnt_ref, carry_ref, tri_ref, x_ref, x_sem):
    i = pl.program_id(0)
    is_p = i < n_prompt_tiles
    tm = x_ref.shape[1]

    @pl.when(i == 0)
    def _():
        carry_ref[...] = jnp.zeros_like(carry_ref)
        rowi = lax.broadcasted_iota(jnp.int32, (tm, tm), 0)
        coli = lax.broadcasted_iota(jnp.int32, (tm, tm), 1)
        tri_ref[...] = jnp.where(coli < rowi, 1.0, 0.0).astype(BF16)

    slot = _fetch_row_tile(i, n_prompt_tiles, n_tiles, xp_hbm_ref, xs_hbm_ref, x_ref, x_sem)

    carry = carry_ref[...]
    att = jnp.where(is_p, attp_ref[...], atts_ref[...])
    rnn = jnp.where(is_p, rnnp_ref[...], rnns_ref[...])
    ya = jnp.dot(att, watt_ref[...], preferred_element_type=F32)
    yr = jnp.dot(rnn, wrnn_ref[...], preferred_element_type=F32)
    mixed = _sigmoid(ga_ref[...].astype(F32)) * ya + _sigmoid(gr_ref[...].astype(F32)) * yr
    h = x_ref[slot] + jnp.dot(mixed.astype(BF16), wout_ref[...], preferred_element_type=F32)
    h_ref[...] = h
    hn = _rmsnorm(h, g2_ref[...])
    hn_ref[...] = _pack_bf16_halves(hn)

    logits = jnp.dot(hn.astype(BF16), wr_ref[...], preferred_element_type=F32) + br_ref[...]
    lane = lax.broadcasted_iota(jnp.int32, (tm, LANES), 1)
    low = jnp.float32(-3e38)
    big = jnp.int32(1 << 20)
    lg = jnp.where(lane < N_GROUPS, logits, low)
    mg = jnp.max(lg, axis=-1, keepdims=True)
    p_grp = 1.0 / jnp.sum(jnp.exp(lg - mg), axis=-1, keepdims=True)
    grp = jnp.min(jnp.where(lg == mg, lane, big), axis=-1, keepdims=True)
    lo = N_GROUPS + EXPERTS_PER_GROUP * grp
    le = jnp.where((lane >= lo) & (lane < lo + EXPERTS_PER_GROUP), logits, low)
    m1 = jnp.max(le, axis=-1, keepdims=True)
    i1 = jnp.min(jnp.where(le == m1, lane, big), axis=-1, keepdims=True)
    le2 = jnp.where(lane == i1, low, le)
    m2 = jnp.max(le2, axis=-1, keepdims=True)
    i2 = jnp.min(jnp.where(le2 == m2, lane, big), axis=-1, keepdims=True)
    e2 = jnp.exp(m2 - m1)
    inv = 1.0 / (1.0 + e2)
    w1 = p_grp * inv
    w2 = p_grp * (e2 * inv)
    x1 = i1 - N_GROUPS
    x2 = i2 - N_GROUPS

    onehot = jnp.where((lane == x1) | (lane == x2), 1.0, 0.0).astype(BF16)
    before = jnp.dot(tri_ref[...], onehot, preferred_element_type=F32) + carry
    r1 = jnp.sum(jnp.where(lane == x1, before, 0.0), axis=-1, keepdims=True)
    r2 = jnp.sum(jnp.where(lane == x2, before, 0.0), axis=-1, keepdims=True)
    carry = carry + jnp.sum(onehot.astype(F32), axis=0, keepdims=True)
    carry_ref[...] = carry
    cnt_ref[...] = carry

    fields = (x1.astype(F32), x2.astype(F32), r1, r2, w1, w2)
    slab = jnp.zeros((tm, LANES), F32)
    for idx, val in enumerate(fields):
        slab = jnp.where(lane == idx, val, slab)
    route_ref[...] = slab.T[:ROUTE_ROWS, :]
    wt_ref[...] = jnp.where(lane == 0, w1, jnp.where(lane == 1, w2, 0.0))


def _merge(xp, xs, att_p, att_s, rnn_p, rnn_s, ga, gr, w_att, w_rnn, w_out, g2, w_r, b_r):
    n_p, d = xp.shape
    n = ga.shape[0]
    npt = n_p // ROW_TILE
    n_tiles = n // ROW_TILE
    aw = att_p.shape[1]
    rw = rnn_p.shape[1]
    pspec = lambda w: pl.BlockSpec((ROW_TILE, w), lambda i: (jnp.minimum(i, npt - 1), 0))
    sspec = lambda w: pl.BlockSpec((ROW_TILE, w), lambda i: (0, 0))
    row = lambda w: pl.BlockSpec((ROW_TILE, w), lambda i: (i, 0))
    hbm = pl.BlockSpec(memory_space=pl.ANY)
    return pl.pallas_call(
        functools.partial(_merge_kernel, npt, n_tiles),
        grid=(n_tiles,),
        in_specs=[hbm, hbm, pspec(aw), sspec(aw), pspec(rw), sspec(rw), row(d), row(d),
                  _const_spec(w_att.shape), _const_spec(w_rnn.shape), _const_spec(w_out.shape),
                  _const_spec((1, d)), _const_spec(w_r.shape), _const_spec((1, LANES))],
        out_specs=[row(d), row(d // 2), pl.BlockSpec((ROUTE_ROWS, ROW_TILE), lambda i: (0, i)), row(LANES),
                   pl.BlockSpec((1, LANES), lambda i: (0, 0))],
        out_shape=[jax.ShapeDtypeStruct((n, d), F32), jax.ShapeDtypeStruct((n, d // 2), jnp.uint32),
                   jax.ShapeDtypeStruct((ROUTE_ROWS, n), F32), jax.ShapeDtypeStruct((n, LANES), F32),
                   jax.ShapeDtypeStruct((1, LANES), F32)],
        scratch_shapes=[pltpu.VMEM((1, LANES), F32), pltpu.VMEM((ROW_TILE, ROW_TILE), BF16),
                        pltpu.VMEM((2, ROW_TILE, d), F32), pltpu.SemaphoreType.DMA((2,))],
        compiler_params=_params(1),
        name="merge",
    )(xp, xs, att_p, att_s, rnn_p, rnn_s, ga, gr, w_att, w_rnn, w_out, g2, w_r, b_r)


def _row_copy(src_ref, src_row, dst_ref, dst_row, sem):
    return pltpu.make_async_copy(src_ref.at[pl.ds(src_row, 1)], dst_ref.at[pl.ds(dst_row, 1)], sem)


DISPATCH_SLOTS = 3


def _dispatch_kernel(n_tiles, pad_end_ref, padded_ref, pos0_ref, pos1_ref, hn_hbm_ref, xs_ref,
                     zero_ref, tile_ref, zero_sem, load_sem, scat_sem):
    i = pl.program_id(0)
    tm = tile_ref.shape[1]

    def load(tile, slot):
        rows = pl.ds(pl.multiple_of(tile * tm, tm), tm)
        return pltpu.make_async_copy(hn_hbm_ref.at[rows], tile_ref.at[slot], load_sem.at[slot])

    def scatter(slot):
        for t in range(tm):
            _row_copy(tile_ref.at[slot], t, xs_ref, pos0_ref[t], scat_sem.at[slot]).start(priority=0)
            _row_copy(tile_ref.at[slot], t, xs_ref, pos1_ref[t], scat_sem.at[slot]).start(priority=1)

    def drain_scatter(slot):
        for _ in range(TOP_K):
            pltpu.make_async_copy(tile_ref.at[slot], xs_ref.at[pl.ds(0, tm)], scat_sem.at[slot]).wait()

    @pl.when(i == 0)
    def _():
        load(0, 0).start()
        zero_ref[...] = jnp.zeros_like(zero_ref)

        def zero_block(start):
            start = pl.multiple_of(start, EXPERT_BLOCK)
            return pltpu.make_async_copy(zero_ref, xs_ref.at[pl.ds(start, EXPERT_BLOCK)], zero_sem)

        used_rows = pad_end_ref[N_EXPERTS - 1]
        total_rows = xs_ref.shape[0]
        for wait in (False, True):
            for e in range(N_EXPERTS):
                @pl.when(padded_ref[e] > 0)
                def _():
                    cp = zero_block(pad_end_ref[e] - EXPERT_BLOCK)
                    cp.wait() if wait else cp.start()

                @pl.when(used_rows + e * EXPERT_BLOCK < total_rows)
                def _():
                    cp = zero_block(used_rows + e * EXPERT_BLOCK)
                    cp.wait() if wait else cp.start()

    for slot in range(DISPATCH_SLOTS):
        @pl.when(((i + 1) % DISPATCH_SLOTS == slot) & (i + 1 < n_tiles))
        def _():
            pl.when(i >= DISPATCH_SLOTS - 1)(functools.partial(drain_scatter, slot))
            load(i + 1, slot).start()

    for slot in range(DISPATCH_SLOTS):
        @pl.when(i % DISPATCH_SLOTS == slot)
        def _():
            load(i, slot).wait()
            scatter(slot)

    @pl.when(i == n_tiles - 1)
    def _():
        for tile in range(max(0, n_tiles - DISPATCH_SLOTS), n_tiles):
            drain_scatter(tile % DISPATCH_SLOTS)


def _dispatch(pad_end, padded, pos0, pos1, hn, rows):
    n, d = hn.shape
    n_tiles = n // ROW_TILE
    grid_spec = pltpu.PrefetchScalarGridSpec(
        num_scalar_prefetch=2,
        grid=(n_tiles,),
        in_specs=[pl.BlockSpec((ROW_TILE,), lambda i, pe, pd: (i,), memory_space=pltpu.SMEM),
                  pl.BlockSpec((ROW_TILE,), lambda i, pe, pd: (i,), memory_space=pltpu.SMEM),
                  pl.BlockSpec(memory_space=pl.ANY)],
        out_specs=pl.BlockSpec(memory_space=pl.ANY),
        scratch_shapes=[pltpu.VMEM((EXPERT_BLOCK, d), hn.dtype), pltpu.VMEM((DISPATCH_SLOTS, ROW_TILE, d), hn.dtype),
                        pltpu.SemaphoreType.DMA(()), pltpu.SemaphoreType.DMA((DISPATCH_SLOTS,)),
                        pltpu.SemaphoreType.DMA((DISPATCH_SLOTS,))],
    )
    return pl.pallas_call(
        functools.partial(_dispatch_kernel, n_tiles),
        grid_spec=grid_spec,
        out_shape=jax.ShapeDtypeStruct((rows, d), hn.dtype),
        compiler_params=_params(1),
        name="dispatch",
    )(pad_end, padded, pos0, pos1, hn)


def _experts_kernel(be_ref, nu_ref, x_ref, wg_ref, wu_ref, wd_ref, y_ref, wg_bf, wu_bf, wd_bf):
    i = pl.program_id(0)
    used = i < nu_ref[0]

    @pl.when(used & ((i == 0) | (be_ref[i] != be_ref[jnp.maximum(i - 1, 0)])))
    def _():
        wg_bf[...] = wg_ref[0].astype(BF16)
        wu_bf[...] = wu_ref[0].astype(BF16)
        wd_bf[...] = wd_ref[0].astype(BF16)

    @pl.when(used)
    def _():
        half = wg_bf.shape[0] // 2
        x_lo, x_hi = (part.astype(BF16) for part in _unpack_bf16_halves(x_ref[...]))

        def up(w_bf):
            return (jnp.dot(x_lo, w_bf[:half, :], preferred_element_type=F32)
                    + jnp.dot(x_hi, w_bf[half:, :], preferred_element_type=F32))

        g = up(wg_bf)
        u = up(wu_bf)
        hid = (g * _sigmoid(g) * u).astype(BF16)
        y_ref[...] = _pack_bf16_halves(jnp.dot(hid, wd_bf[...], preferred_element_type=F32))

    @pl.when(jnp.logical_not(used))
    def _():
        y_ref[...] = jnp.zeros_like(y_ref)


def _experts(block_expert, n_used, xs, wg, wu, wd):
    rows, dp = xs.shape
    d = wg.shape[1]
    assert dp * 2 == d
    de = wg.shape[2]
    nb = rows // EXPERT_BLOCK
    last = lambda i, nu: jnp.minimum(i, nu[0] - 1)
    grid_spec = pltpu.PrefetchScalarGridSpec(
        num_scalar_prefetch=2,
        grid=(nb,),
        in_specs=[pl.BlockSpec((EXPERT_BLOCK, dp), lambda i, be, nu: (last(i, nu), 0)),
                  pl.BlockSpec((1, d, de), lambda i, be, nu: (be[last(i, nu)], 0, 0)),
                  pl.BlockSpec((1, d, de), lambda i, be, nu: (be[last(i, nu)], 0, 0)),
                  pl.BlockSpec((1, de, d), lambda i, be, nu: (be[last(i, nu)], 0, 0))],
        out_specs=pl.BlockSpec((EXPERT_BLOCK, dp), lambda i, be, nu: (i, 0)),
        scratch_shapes=[pltpu.VMEM((d, de), BF16), pltpu.VMEM((d, de), BF16), pltpu.VMEM((de, d), BF16)],
    )
    return pl.pallas_call(
        _experts_kernel,
        grid_spec=grid_spec,
        out_shape=jax.ShapeDtypeStruct((rows, dp), jnp.uint32),
        compiler_params=_params(1),
        name="experts",
    )(block_expert, n_used, xs, wg, wu, wd)


def _combine_kernel(n_tiles, pos0_ref, pos1_ref, next0_ref, next1_ref, h_ref, wt_ref, gf_ref, y_hbm_ref, out_ref,
                    buf_ref, sem):
    i = pl.program_id(0)
    tm = h_ref.shape[0]

    def request(p0_ref, p1_ref, slot):
        for t in range(tm):
            _row_copy(y_hbm_ref, p0_ref[t], buf_ref.at[slot, 0], t, sem.at[slot]).start(priority=0)
            _row_copy(y_hbm_ref, p1_ref[t], buf_ref.at[slot, 1], t, sem.at[slot]).start(priority=1)

    def consume(slot):
        for k in range(TOP_K):
            pltpu.make_async_copy(y_hbm_ref.at[pl.ds(0, tm)], buf_ref.at[slot, k], sem.at[slot]).wait()
        wt = wt_ref[...]
        lo0, hi0 = _unpack_bf16_halves(buf_ref[slot, 0])
        lo1, hi1 = _unpack_bf16_halves(buf_ref[slot, 1])
        w0 = wt[:, 0:1]
        w1 = wt[:, 1:2]
        moe = jnp.concatenate([lo0 * w0 + lo1 * w1, hi0 * w0 + hi1 * w1], axis=1)
        out_ref[...] = _rmsnorm(h_ref[...] + moe, gf_ref[...])

    pl.when(i == 0)(lambda: request(pos0_ref, pos1_ref, 0))
    for slot in range(2):
        pl.when((i + 1 < n_tiles) & ((i + 1) % 2 == slot))(functools.partial(request, next0_ref, next1_ref, slot))
    for slot in range(2):
        pl.when(i % 2 == slot)(functools.partial(consume, slot))


def _combine(pos0, pos1, h, wt, gf, y_sorted, row0, n_rows):
    d = h.shape[1]
    dp = y_sorted.shape[1]
    assert row0 % ROW_TILE == 0 and n_rows % ROW_TILE == 0
    blk0 = row0 // ROW_TILE
    n_tiles = n_rows // ROW_TILE
    last = blk0 + n_tiles - 1
    return pl.pallas_call(
        functools.partial(_combine_kernel, n_tiles),
        grid=(n_tiles,),
        in_specs=[pl.BlockSpec((ROW_TILE,), lambda i: (blk0 + i,), memory_space=pltpu.SMEM),
                  pl.BlockSpec((ROW_TILE,), lambda i: (blk0 + i,), memory_space=pltpu.SMEM),
                  pl.BlockSpec((ROW_TILE,), lambda i: (jnp.minimum(blk0 + i + 1, last),), memory_space=pltpu.SMEM),
                  pl.BlockSpec((ROW_TILE,), lambda i: (jnp.minimum(blk0 + i + 1, last),), memory_space=pltpu.SMEM),
                  pl.BlockSpec((ROW_TILE, d), lambda i: (blk0 + i, 0)),
                  pl.BlockSpec((ROW_TILE, LANES), lambda i: (blk0 + i, 0)),
                  _const_spec((1, d)),
                  pl.BlockSpec(memory_space=pl.ANY)],
        out_specs=pl.BlockSpec((ROW_TILE, d), lambda i: (i, 0)),
        out_shape=jax.ShapeDtypeStruct((n_rows, d), F32),
        scratch_shapes=[pltpu.VMEM((2, TOP_K, ROW_TILE, dp), jnp.uint32), pltpu.SemaphoreType.DMA((2,))],
        compiler_params=_params(1),
        name="combine",
    )(pos0, pos1, pos0, pos1, h, wt, gf, y_sorted)


def _bias_base(rel_bias, n_queries, key_offset):
    m = jnp.arange(BIAS_BASE)
    dist = (n_queries - 1) - m + key_offset
    idx = jnp.clip(dist, -MAX_REL, MAX_REL) + MAX_REL
    return rel_bias[:, idx].astype(F32)


def _block_diag(w):
    nb, bw, _ = w.shape
    eye = jnp.eye(nb, dtype=w.dtype)
    return jnp.einsum("ncd,nm->ncmd", w, eye).reshape(nb * bw, nb * bw)


def kernel(x_prompt, x_sample, cache_k, cache_v, state_conv, state_h, norm1_g, w_in, b_in, rel_bias, w_att_branch, conv_w, conv_b, w_rec_gate, b_rec_gate, w_in_gate, b_in_gate, lru_lambda, w_rnn_branch, w_out, norm2_g, w_router_group, b_router_group, w_router_expert, b_router_expert, w_e_gate, w_e_up, w_e_down, final_norm_g):
    batch, seq, d = x_prompt.shape
    dec_b, dec_t, _ = x_sample.shape
    depth = w_in.shape[0]
    assert depth == 1
    att_w = N_HEADS * HEAD_DIM
    rnn_w = conv_w.shape[2]
    cache_len = cache_k.shape[2]
    n_p = batch * seq
    n_s = dec_b * dec_t
    n = n_p + n_s
    keep = min(BAND_PAST + CHUNK, seq)
    row2 = lambda a: a.reshape(1, -1).astype(F32)

    tiles_per_seq = seq // ROW_TILE
    tail_tiles = -(-keep // ROW_TILE)
    state_tiles = tuple(b * tiles_per_seq + tiles_per_seq - tail_tiles + j
                        for b in range(batch) for j in range(tail_tiles)) + (n_p // ROW_TILE,)

    xp = x_prompt.reshape(n_p, d)
    xs = x_sample.reshape(n_s, d)
    rnn_args = (conv_w[0].astype(F32), row2(conv_b[0]),
                _block_diag(w_rec_gate[0]).astype(BF16), row2(b_rec_gate[0]),
                _block_diag(w_in_gate[0]).astype(BF16), row2(b_in_gate[0]), row2(lru_lambda[0]))
    q, k, v, kvt, ga, gr, rnn_p, conv_p, h_p, rx_s, rgate_s = _proj(
        xp, xs, row2(norm1_g[0]), w_in[0].astype(BF16), row2(b_in[0]), rnn_args, att_w, rnn_w, state_tiles, batch)

    att_p = _attn_prompt(q, k, v, _bias_base(rel_bias[0], Q_BLOCK, (KEY_BLOCKS - 1) * Q_BLOCK), batch, seq)
    att_s, new_k_s, new_v_s = _attn_sample(
        q, k, v, kvt, cache_k[0].reshape(dec_b, cache_len, att_w), cache_v[0].reshape(dec_b, cache_len, att_w),
        _bias_base(rel_bias[0], dec_t, cache_len), n_p, batch * tail_tiles * ROW_TILE, dec_b, dec_t)

    pad_state = lambda s: jnp.pad(s.astype(F32), ((0, 0), (SUBLANES - (CONV_W - 1), 0), (0, 0)))
    rnn_s, conv_s, h_s = _rnn_sample(rx_s, rgate_s, pad_state(state_conv[0]), state_h[0].astype(F32)[:, None, :],
                                     *rnn_args, dec_b, dec_t)

    w_r = jnp.zeros((d, LANES), F32).at[:, :N_GROUPS].set(w_router_group[0])
    w_r = w_r.at[:, N_GROUPS:N_GROUPS + N_EXPERTS].set(w_router_expert[0]).astype(BF16)
    b_r = jnp.zeros((1, LANES), F32).at[0, :N_GROUPS].set(b_router_group[0])
    b_r = b_r.at[0, N_GROUPS:N_GROUPS + N_EXPERTS].set(b_router_expert[0])
    h, hn, route, wt, cnt = _merge(xp, xs, att_p, att_s, rnn_p, rnn_s, ga, gr,
                                   w_att_branch[0].astype(BF16), w_rnn_branch[0].astype(BF16), w_out[0].astype(BF16),
                                   row2(norm2_g[0]), w_r, b_r)

    counts = cnt[0, :N_EXPERTS].astype(jnp.int32)
    padded = (counts + EXPERT_BLOCK - 1) // EXPERT_BLOCK * EXPERT_BLOCK
    pad_end = jnp.cumsum(padded)
    pad_start = pad_end - padded
    expert = route[0:TOP_K].astype(jnp.int32)
    rank = route[TOP_K:2 * TOP_K].astype(jnp.int32)
    ids = jnp.arange(N_EXPERTS, dtype=jnp.int32)[:, None, None]
    pos = jnp.sum(jnp.where(expert[None] == ids, pad_start[:, None, None], 0), axis=0) + rank
    n_blocks = -(-(n * TOP_K) // EXPERT_BLOCK) + N_EXPERTS
    rows = n_blocks * EXPERT_BLOCK
    block_start = jnp.arange(n_blocks, dtype=jnp.int32) * EXPERT_BLOCK
    block_expert = jnp.minimum(jnp.sum((block_start[:, None] >= pad_end[None, :]).astype(jnp.int32), axis=1),
                               N_EXPERTS - 1)
    n_used = (pad_end[-1:] // EXPERT_BLOCK).astype(jnp.int32)

    x_sorted = _dispatch(pad_end, padded, pos[0], pos[1], hn, rows)
    y_sorted = _experts(block_expert, n_used, x_sorted, w_e_gate[0], w_e_up[0], w_e_down[0])
    gf = row2(final_norm_g)
    y_p = _combine(pos[0], pos[1], h, wt, gf, y_sorted, 0, n_p)
    y_s = _combine(pos[0], pos[1], h, wt, gf, y_sorted, n_p, n_s)

    def kv_state(col0):
        span = tail_tiles * ROW_TILE
        tails = [kvt[b * span + span - keep:(b + 1) * span, col0:col0 + att_w] for b in range(batch)]
        return jnp.stack(tails).reshape(1, batch, keep, N_HEADS, HEAD_DIM)

    heads = lambda a: a.reshape(1, dec_b, cache_len, N_HEADS, HEAD_DIM)
    return (y_p.reshape(batch, seq, d), y_s.reshape(dec_b, dec_t, d),
            kv_state(0), kv_state(att_w),
            conv_p[:, SUBLANES - (CONV_W - 1):][None], h_p[:, 0][None],
            heads(new_k_s), heads(new_v_s),
            conv_s[:, SUBLANES - (CONV_W - 1):][None], h_s[:, 0][None])
```

```python
import functools

import jax
import jax.numpy as jnp
from jax import lax
from jax.experimental import pallas as pl
from jax.experimental.pallas import tpu as pltpu

CHUNK = 64
LEFT_CHUNKS = 8
BAND_PAST = LEFT_CHUNKS * CHUNK
N_HEADS = 8
HEAD_DIM = 64
MAX_REL = 256
NEG_INF = -1e30
CONV_W = 4
RG_C = 8.0
N_GROUPS = 4
EXPERTS_PER_GROUP = 8
N_EXPERTS = N_GROUPS * EXPERTS_PER_GROUP
TOP_K = 2
EPS = 1e-6

LANES = 128
SUBLANES = 8
ROW_TILE = 512
Q_BLOCK = 256
KEY_BLOCKS = 3
KEY_WINDOW = KEY_BLOCKS * Q_BLOCK
BIAS_BASE = 1024
EXPERT_BLOCK = 512
VMEM_LIMIT = 56 * 1024 * 1024

BF16 = jnp.bfloat16
F32 = jnp.float32


def _params(n_axes, vmem=VMEM_LIMIT):
    return pltpu.CompilerParams(dimension_semantics=("arbitrary",) * n_axes, vmem_limit_bytes=vmem)


def _const_spec(shape):
    zeros = (0,) * len(shape)
    return pl.BlockSpec(shape, lambda *_: zeros, pipeline_mode=pl.Buffered(1))


def _rmsnorm(x, g):
    return x * lax.rsqrt(jnp.mean(x * x, axis=-1, keepdims=True) + EPS) * g


def _pack_bf16_halves(x):
    c = x.shape[1] // 2
    lo = lax.bitcast_convert_type(x[:, :c].astype(BF16).astype(F32), jnp.uint32)
    hi = lax.bitcast_convert_type(x[:, c:].astype(BF16).astype(F32), jnp.uint32)
    return (lo >> 16) | (hi & jnp.uint32(0xFFFF0000))


def _unpack_bf16_halves(p):
    lo = lax.bitcast_convert_type(p << 16, F32)
    hi = lax.bitcast_convert_type(p & jnp.uint32(0xFFFF0000), F32)
    return lo, hi


def _sigmoid(x):
    return 0.5 * jnp.tanh(0.5 * x) + 0.5


def _proj_kernel(n_prompt_tiles, tiles_per_seq, state_tiles, att_w, rnn_w, d_model,
                 xp_ref, xs_ref, g_ref, w_ref, b_ref, cw_ref, cb_ref, wa_ref, ba_ref, wx_ref, bx_ref, lam_ref,
                 q_ref, k_ref, v_ref, kvt_ref, ga_ref, gr_ref, rnn_ref, conv_out_ref, h_out_ref, rxs_ref, rgs_ref,
                 rx_ref, tail_ref, hl_ref, p_ref, hs_ref, h_ref):
    i = pl.program_id(0)
    tm = xp_ref.shape[0]
    x = jnp.where(i < n_prompt_tiles, xp_ref[...], xs_ref[...])
    xn = _rmsnorm(x, g_ref[...]).astype(BF16)

    def seg(lo, width):
        return jnp.dot(xn, w_ref[:, lo:lo + width], preferred_element_type=F32) + b_ref[:, lo:lo + width]

    q = seg(0, att_w)
    k = seg(att_w, att_w)
    v = seg(2 * att_w, att_w)
    q_ref[...] = (q * (HEAD_DIM ** -0.5)).astype(BF16)
    k_ref[...] = k.astype(BF16)
    v_ref[...] = v.astype(BF16)

    @pl.when(functools.reduce(jnp.logical_or, [i == t for t in state_tiles]))
    def _():
        kvt_ref[:, :att_w] = k
        kvt_ref[:, att_w:] = v

    o = 3 * att_w
    ncol = rnn_w // LANES
    rx = seg(o, rnn_w)
    for c in range(ncol):
        rx_ref[c] = rx[:, c * LANES:(c + 1) * LANES]
    rg = seg(o + rnn_w, rnn_w)

    @pl.when(i % tiles_per_seq == 0)
    def _():
        tail_ref[...] = jnp.zeros_like(tail_ref)
        h_ref[...] = jnp.zeros_like(h_ref)

    hseq, tail, h_last = _rglru_prompt_tile(tm, rx_ref, cw_ref, cb_ref, wa_ref, ba_ref, wx_ref, bx_ref, lam_ref,
                                            tail_ref, hl_ref, p_ref, hs_ref, h_ref)
    rnn_ref[...] = (hseq * jax.nn.gelu(rg)).astype(BF16)
    ga_ref[...] = seg(o + 2 * rnn_w, d_model).astype(BF16)
    gr_ref[...] = seg(o + 2 * rnn_w + d_model, d_model).astype(BF16)

    @pl.when(i < n_prompt_tiles)
    def _():
        conv_out_ref[0] = tail
        h_out_ref[0] = h_last

    @pl.when(i == n_prompt_tiles)
    def _():
        for c in range(ncol):
            rxs_ref[c] = rx_ref[c]
        rgs_ref[...] = rg.astype(BF16)


def _proj(xp, xs, g1, w_in, b_in, rnn_args, att_w, rnn_w, state_tiles, batch):
    n_p, d = xp.shape
    n_s = xs.shape[0]
    assert n_p % ROW_TILE == 0 and n_s == ROW_TILE
    assert ROW_TILE % (SUBLANES * SUBLANES) == 0 and ROW_TILE // SUBLANES >= CONV_W
    n = n_p + n_s
    npt = n_p // ROW_TILE
    tps = npt // batch
    pw = w_in.shape[1]
    ncol = rnn_w // LANES
    row = lambda w: pl.BlockSpec((ROW_TILE, w), lambda i: (i, 0))
    rows = lambda w, dt: jax.ShapeDtypeStruct((n, w), dt)
    slot = lambda i: sum(jnp.where(i > t, 1, 0) for t in state_tiles)
    state = lambda r: pl.BlockSpec((1, r, rnn_w), lambda i: (jnp.minimum(i // tps, batch - 1), 0, 0))
    out_specs = [row(att_w), row(att_w), row(att_w),
                 pl.BlockSpec((ROW_TILE, 2 * att_w), lambda i: (slot(i), 0)),
                 row(d), row(d), row(rnn_w), state(SUBLANES), state(1),
                 pl.BlockSpec((ncol, ROW_TILE, LANES), lambda i: (0, 0, 0)),
                 pl.BlockSpec((ROW_TILE, rnn_w), lambda i: (0, 0))]
    out_shape = [rows(att_w, BF16), rows(att_w, BF16), rows(att_w, BF16),
                 jax.ShapeDtypeStruct((len(state_tiles) * ROW_TILE, 2 * att_w), F32),
                 rows(d, BF16), rows(d, BF16), rows(rnn_w, BF16),
                 jax.ShapeDtypeStruct((batch, SUBLANES, rnn_w), F32), jax.ShapeDtypeStruct((batch, 1, rnn_w), F32),
                 jax.ShapeDtypeStruct((ncol, n_s, LANES), F32), jax.ShapeDtypeStruct((n_s, rnn_w), BF16)]
    tile = lambda: pltpu.VMEM((ROW_TILE, rnn_w), F32)
    cols = lambda: pltpu.VMEM((ncol, ROW_TILE, LANES), F32)
    return pl.pallas_call(
        functools.partial(_proj_kernel, npt, tps, state_tiles, att_w, rnn_w, d),
        grid=(n // ROW_TILE,),
        in_specs=[pl.BlockSpec((ROW_TILE, d), lambda i: (jnp.minimum(i, npt - 1), 0)),
                  pl.BlockSpec((ROW_TILE, d), lambda i: (0, 0)),
                  _const_spec((1, d)), _const_spec((d, pw)), _const_spec((1, pw))]
                 + [_const_spec(a.shape) for a in rnn_args],
        out_specs=out_specs,
        out_shape=out_shape,
        scratch_shapes=[cols(), pltpu.VMEM((SUBLANES, rnn_w), F32), tile(), tile(), cols(),
                        pltpu.VMEM((1, rnn_w), F32)],
        compiler_params=_params(1),
        name="proj_rglru",
    )(xp, xs, g1, w_in, b_in, *rnn_args)


def _attn_p_kernel(q_ref, k0_ref, k1_ref, k2_ref, v0_ref, v1_ref, v2_ref, base_ref, o_ref, bias_ref):
    b = pl.program_id(0)
    i = pl.program_id(1)

    @pl.when((b == 0) & (i == 0))
    def _():
        r = lax.broadcasted_iota(jnp.int32, (Q_BLOCK, KEY_WINDOW), 0) // CHUNK
        c = lax.broadcasted_iota(jnp.int32, (Q_BLOCK, KEY_WINDOW), 1) // CHUNK
        in_band = (c >= r) & (c <= r + LEFT_CHUNKS)
        for h in range(N_HEADS):
            rows = jnp.broadcast_to(base_ref[h:h + 1, :], (Q_BLOCK, BIAS_BASE))
            toeplitz = pltpu.roll(rows, BIAS_BASE - (Q_BLOCK - 1), axis=1, stride=1, stride_axis=0)
            bias_ref[h] = jnp.where(in_band, toeplitz[:, :KEY_WINDOW], NEG_INF)

    kwin = jnp.concatenate([k0_ref[...], k1_ref[...], k2_ref[...]], axis=0)
    vwin = jnp.concatenate([v0_ref[...], v1_ref[...], v2_ref[...]], axis=0)
    lane = lax.broadcasted_iota(jnp.int32, (1, LANES), 1)

    def heads(at_sequence_start):
        if at_sequence_start:
            col = lax.broadcasted_iota(jnp.int32, (1, KEY_WINDOW), 1)
            start_mask = jnp.where(col // Q_BLOCK + i - (KEY_BLOCKS - 1) >= 0, 0.0, NEG_INF).astype(F32)
        for pair in range(N_HEADS // 2):
            sl = slice(pair * LANES, (pair + 1) * LANES)
            q2 = q_ref[:, sl]
            k2 = kwin[:, sl]
            v2 = vwin[:, sl]
            acc = jnp.zeros((Q_BLOCK, LANES), F32)
            for half in range(2):
                hmask = (lane // HEAD_DIM) == half
                qm = jnp.where(hmask, q2, jnp.zeros_like(q2))
                vm = jnp.where(hmask, v2, jnp.zeros_like(v2))
                s = lax.dot_general(qm, k2, (((1,), (1,)), ((), ())), preferred_element_type=F32)
                s = s + bias_ref[2 * pair + half]
                if at_sequence_start:
                    s = s + start_mask
                m = jnp.max(s, axis=-1, keepdims=True)
                e = jnp.exp(s - m)
                l = jnp.sum(e, axis=-1, keepdims=True)
                o = jnp.dot(e.astype(BF16), vm, preferred_element_type=F32)
                acc = acc + o / l
            o_ref[:, sl] = acc.astype(BF16)

    pl.when(i < KEY_BLOCKS - 1)(lambda: heads(True))
    pl.when(i >= KEY_BLOCKS - 1)(lambda: heads(False))


def _attn_prompt(q, k, v, base, batch, seq):
    w = q.shape[1]
    nqb = seq // Q_BLOCK
    kspec = lambda back: pl.BlockSpec((Q_BLOCK, w), lambda b, i: (b * nqb + jnp.maximum(i - back, 0), 0))
    return pl.pallas_call(
        _attn_p_kernel,
        grid=(batch, nqb),
        in_specs=[pl.BlockSpec((Q_BLOCK, w), lambda b, i: (b * nqb + i, 0)),
                  kspec(2), kspec(1), kspec(0), kspec(2), kspec(1), kspec(0),
                  _const_spec(base.shape)],
        out_specs=pl.BlockSpec((Q_BLOCK, w), lambda b, i: (b * nqb + i, 0)),
        out_shape=jax.ShapeDtypeStruct((batch * seq, w), BF16),
        scratch_shapes=[pltpu.VMEM((N_HEADS, Q_BLOCK, KEY_WINDOW), F32)],
        compiler_params=_params(2),
        name="attn_prompt",
    )(q, k, k, k, v, v, v, base)


def _attn_s_kernel(t, q_ref, k_ref, v_ref, kvt_ref, ck_ref, cv_ref, base_ref, o_ref, nk_ref, nv_ref, bias_ref):
    cache_len = ck_ref.shape[1]
    w = q_ref.shape[1]

    @pl.when(pl.program_id(0) == 0)
    def _():
        col = lax.broadcasted_iota(jnp.int32, (t, cache_len + t), 1)
        for h in range(N_HEADS):
            rows = jnp.broadcast_to(base_ref[h:h + 1, :], (t, BIAS_BASE))
            toeplitz = pltpu.roll(rows, BIAS_BASE - (t - 1), axis=1, stride=1, stride_axis=0)
            bias_ref[h * t:(h + 1) * t, :] = jnp.where(col >= cache_len - BAND_PAST, toeplitz[:, :cache_len + t],
                                                       NEG_INF)

    ck = ck_ref[0]
    cv = cv_ref[0]
    kall = jnp.concatenate([ck.astype(BF16), k_ref[...]], axis=0)
    vall = jnp.concatenate([cv.astype(BF16), v_ref[...]], axis=0)
    q = q_ref[...]
    head_of_lane = lax.broadcasted_iota(jnp.int32, (1, w), 1) // HEAD_DIM
    qs = jnp.concatenate([jnp.where(head_of_lane == h, q, jnp.zeros_like(q)) for h in range(N_HEADS)], axis=0)
    s = lax.dot_general(qs, kall, (((1,), (1,)), ((), ())), preferred_element_type=F32) + bias_ref[...]
    m = jnp.max(s, axis=-1, keepdims=True)
    e = jnp.exp(s - m)
    l = jnp.sum(e, axis=-1, keepdims=True)
    o_all = jnp.dot(e.astype(BF16), vall, preferred_element_type=F32) / l
    o = jnp.zeros((t, w), F32)
    for h in range(N_HEADS):
        o = o + jnp.where(head_of_lane == h, o_all[h * t:(h + 1) * t], 0.0)
    o_ref[...] = o.astype(BF16)
    nk_ref[0, :cache_len - t, :] = ck[t:]
    nk_ref[0, cache_len - t:, :] = kvt_ref[:, :w]
    nv_ref[0, :cache_len - t, :] = cv[t:]
    nv_ref[0, cache_len - t:, :] = kvt_ref[:, w:]


def _attn_sample(q, k, v, kvt, cache_k, cache_v, base, row0, kvt_row0, dec_b, t):
    w = q.shape[1]
    cache_len = cache_k.shape[1]
    assert row0 % t == 0 and kvt_row0 % t == 0 and cache_len + 2 * t - 1 <= BIAS_BASE
    tok = lambda width, r0: pl.BlockSpec((t, width), lambda b: (r0 // t + b, 0))
    cache = pl.BlockSpec((1, cache_len, w), lambda b: (b, 0, 0))
    return pl.pallas_call(
        functools.partial(_attn_s_kernel, t),
        grid=(dec_b,),
        in_specs=[tok(w, row0), tok(w, row0), tok(w, row0), tok(2 * w, kvt_row0), cache, cache,
                  _const_spec(base.shape)],
        out_specs=[pl.BlockSpec((t, w), lambda b: (b, 0)), cache, cache],
        out_shape=[jax.ShapeDtypeStruct((dec_b * t, w), BF16),
                   jax.ShapeDtypeStruct(cache_k.shape, F32), jax.ShapeDtypeStruct(cache_v.shape, F32)],
        scratch_shapes=[pltpu.VMEM((N_HEADS * t, cache_len + t), F32)],
        compiler_params=_params(1),
        name="attn_sample",
    )(q, k, v, kvt, cache_k, cache_v, base)


def _rnn_sample_kernel(t, rx_ref, rg_ref, conv0_ref, h0_ref, cw_ref, cb_ref, wa_ref, ba_ref, wx_ref, bx_ref, lam_ref,
                       rnn_ref, conv_out_ref, h_out_ref, ext_ref, hs_ref):
    streams = conv0_ref.shape[0]
    groups = t // SUBLANES
    rx = jnp.concatenate([rx_ref[c] for c in range(rx_ref.shape[0])], axis=1)
    cw = cw_ref[...]
    xc = []
    for s in range(streams):
        ext_ref[s, :SUBLANES, :] = conv0_ref[s]
        ext_ref[s, SUBLANES:, :] = rx[s * t:(s + 1) * t]
        acc = cb_ref[...]
        for tap in range(CONV_W):
            off = SUBLANES - (CONV_W - 1) + tap
            acc = acc + ext_ref[s, off:off + t, :] * cw[tap:tap + 1, :]
        xc.append(acc)
        conv_out_ref[s] = ext_ref[s, t:t + SUBLANES, :]

    a, b = _rglru_coeffs(jnp.concatenate(xc, axis=0), wa_ref, ba_ref, wx_ref, bx_ref, lam_ref)

    sub = lax.broadcasted_iota(jnp.int32, a.shape, 0) % SUBLANES
    shift = 1
    while shift < SUBLANES:
        keep = sub >= shift
        a_prev = pltpu.roll(a, shift, axis=0)
        b_prev = pltpu.roll(b, shift, axis=0)
        b = jnp.where(keep, a * b_prev + b, b)
        a = jnp.where(keep, a * a_prev, a)
        shift *= 2

    for s in range(streams):
        h = h0_ref[s]
        for g in range(groups):
            rows = slice(s * t + g * SUBLANES, s * t + (g + 1) * SUBLANES)
            hg = b[rows] + a[rows] * h
            hs_ref[rows, :] = hg
            h = hg[SUBLANES - 1:SUBLANES, :]
        h_out_ref[s] = h
    rnn_ref[...] = (hs_ref[...] * jax.nn.gelu(rg_ref[...].astype(F32))).astype(BF16)


def _rnn_sample(rx, rgate, conv0, h0, cw, cb, wa, ba, wx, bx, lam, streams, t):
    n, w = rgate.shape
    assert n == streams * t and t % SUBLANES == 0 and rx.shape == (w // LANES, n, LANES)
    full = lambda shape: pl.BlockSpec(shape, lambda i: (0,) * len(shape))
    return pl.pallas_call(
        functools.partial(_rnn_sample_kernel, t),
        grid=(1,),
        in_specs=[full(rx.shape), full((n, w)), full((streams, SUBLANES, w)), full((streams, 1, w)),
                  full(cw.shape), full((1, w)), full((w, w)), full((1, w)), full((w, w)), full((1, w)), full((1, w))],
        out_specs=[full((n, w)), full((streams, SUBLANES, w)), full((streams, 1, w))],
        out_shape=[jax.ShapeDtypeStruct((n, w), BF16),
                   jax.ShapeDtypeStruct((streams, SUBLANES, w), F32),
                   jax.ShapeDtypeStruct((streams, 1, w), F32)],
        scratch_shapes=[pltpu.VMEM((streams, t + SUBLANES, w), F32), pltpu.VMEM((n, w), F32)],
        compiler_params=_params(1),
        name="rnn_sample",
    )(rx, rgate, conv0, h0, cw, cb, wa, ba, wx, bx, lam)


def _rglru_coeffs(xc, wa_ref, ba_ref, wx_ref, bx_ref, lam_ref):
    xb = xc.astype(BF16)
    r = _sigmoid(jnp.dot(xb, wa_ref[...], preferred_element_type=F32) + ba_ref[...])
    gi = _sigmoid(jnp.dot(xb, wx_ref[...], preferred_element_type=F32) + bx_ref[...])
    neg_lam = -lam_ref[...]
    softplus = jnp.maximum(neg_lam, 0.0) + jnp.log1p(jnp.exp(-jnp.abs(neg_lam)))
    log_a = (-RG_C) * r * softplus
    a = jnp.exp(log_a)
    b = jnp.sqrt(-jnp.tanh(log_a) * (a * a + 1.0)) * (gi * xc)
    return a, b


def _rglru_prompt_tile(tm, rx_ref, cw_ref, cb_ref, wa_ref, ba_ref, wx_ref, bx_ref, lam_ref,
                       tail_ref, hl_ref, p_ref, hs_ref, h_ref):
    seg = tm // SUBLANES
    ncol = rx_ref.shape[0]
    w = ncol * LANES

    def strided_rows(ref, j):
        return jnp.concatenate([ref[c, pl.ds(j, SUBLANES, stride=seg), :] for c in range(ncol)], axis=1)

    slab = [strided_rows(rx_ref, j) for j in range(seg)]
    sub = lax.broadcasted_iota(jnp.int32, (SUBLANES, w), 0)

    def before(k):
        carry = jnp.broadcast_to(tail_ref[SUBLANES - k:SUBLANES - k + 1, :], (SUBLANES, w))
        return jnp.where(sub == 0, carry, pltpu.roll(slab[seg - k], 1, axis=0))

    lead = {k: before(k) for k in range(1, CONV_W)}
    cw = cw_ref[...]
    cb = cb_ref[...]
    xc = []
    for j in range(seg):
        acc = cb + slab[j] * cw[CONV_W - 1:CONV_W, :]
        for k in range(1, CONV_W):
            prev = slab[j - k] if j >= k else lead[k - j]
            acc = acc + prev * cw[CONV_W - 1 - k:CONV_W - k, :]
        xc.append(acc)
    tail = jnp.concatenate([rx_ref[c, tm - SUBLANES:tm, :] for c in range(ncol)], axis=1)
    tail_ref[...] = tail

    a, b = _rglru_coeffs(jnp.concatenate(xc, axis=0), wa_ref, ba_ref, wx_ref, bx_ref, lam_ref)

    h = jnp.zeros((SUBLANES, w), F32)
    p = jnp.ones((SUBLANES, w), F32)
    for j in range(seg):
        rows = slice(j * SUBLANES, (j + 1) * SUBLANES)
        h = a[rows] * h + b[rows]
        p = a[rows] * p
        hl_ref[rows, :] = h
        p_ref[rows, :] = p
    c = h_ref[...]
    enter = []
    for s in range(SUBLANES):
        enter.append(c)
        c = h[s:s + 1, :] + p[s:s + 1, :] * c
    last = c
    h_ref[...] = last
    enter = jnp.concatenate(enter, axis=0)
    for j in range(seg):
        rows = slice(j * SUBLANES, (j + 1) * SUBLANES)
        hj = hl_ref[rows, :] + p_ref[rows, :] * enter
        for col in range(ncol):
            hs_ref[col, pl.ds(j, SUBLANES, stride=seg), :] = hj[:, col * LANES:(col + 1) * LANES]
    hseq = jnp.concatenate([hs_ref[col] for col in range(ncol)], axis=1)
    return hseq, tail, last


ROUTE_ROWS = SUBLANES


def _merge_kernel(n_prompt_tiles,
                  xp_ref, xs_ref, attp_ref, atts_ref, rnnp_ref, rnns_ref, ga_ref, gr_ref,
                  watt_ref, wrnn_ref, wout_ref, g2_ref, wr_ref, br_ref,
                  h_ref, hn_ref, route_ref, wt_ref, cnt_ref, carry_ref, tri_ref):
    i = pl.program_id(0)
    is_p = i < n_prompt_tiles
    tm = xp_ref.shape[0]

    @pl.when(i == 0)
    def _():
        carry_ref[...] = jnp.zeros_like(carry_ref)
        rowi = lax.broadcasted_iota(jnp.int32, (tm, tm), 0)
        coli = lax.broadcasted_iota(jnp.int32, (tm, tm), 1)
        tri_ref[...] = jnp.where(coli < rowi, 1.0, 0.0).astype(BF16)

    carry = carry_ref[...]
    x = jnp.where(is_p, xp_ref[...], xs_ref[...])
    att = jnp.where(is_p, attp_ref[...], atts_ref[...])
    rnn = jnp.where(is_p, rnnp_ref[...], rnns_ref[...])
    ya = jnp.dot(att, watt_ref[...], preferred_element_type=F32)
    yr = jnp.dot(rnn, wrnn_ref[...], preferred_element_type=F32)
    mixed = _sigmoid(ga_ref[...].astype(F32)) * ya + _sigmoid(gr_ref[...].astype(F32)) * yr
    h = x + jnp.dot(mixed.astype(BF16), wout_ref[...], preferred_element_type=F32)
    h_ref[...] = h
    hn = _rmsnorm(h, g2_ref[...])
    hn_ref[...] = _pack_bf16_halves(hn)

    logits = jnp.dot(hn.astype(BF16), wr_ref[...], preferred_element_type=F32) + br_ref[...]
    lane = lax.broadcasted_iota(jnp.int32, (tm, LANES), 1)
    low = jnp.float32(-3e38)
    big = jnp.int32(1 << 20)
    lg = jnp.where(lane < N_GROUPS, logits, low)
    mg = jnp.max(lg, axis=-1, keepdims=True)
    p_grp = 1.0 / jnp.sum(jnp.exp(lg - mg), axis=-1, keepdims=True)
    grp = jnp.min(jnp.where(lg == mg, lane, big), axis=-1, keepdims=True)
    lo = N_GROUPS + EXPERTS_PER_GROUP * grp
    le = jnp.where((lane >= lo) & (lane < lo + EXPERTS_PER_GROUP), logits, low)
    m1 = jnp.max(le, axis=-1, keepdims=True)
    i1 = jnp.min(jnp.where(le == m1, lane, big), axis=-1, keepdims=True)
    le2 = jnp.where(lane == i1, low, le)
    m2 = jnp.max(le2, axis=-1, keepdims=True)
    i2 = jnp.min(jnp.where(le2 == m2, lane, big), axis=-1, keepdims=True)
    e2 = jnp.exp(m2 - m1)
    inv = 1.0 / (1.0 + e2)
    w1 = p_grp * inv
    w2 = p_grp * (e2 * inv)
    x1 = i1 - N_GROUPS
    x2 = i2 - N_GROUPS

    onehot = jnp.where((lane == x1) | (lane == x2), 1.0, 0.0).astype(BF16)
    before = jnp.dot(tri_ref[...], onehot, preferred_element_type=F32) + carry
    r1 = jnp.sum(jnp.where(lane == x1, before, 0.0), axis=-1, keepdims=True)
    r2 = jnp.sum(jnp.where(lane == x2, before, 0.0), axis=-1, keepdims=True)
    carry = carry + jnp.sum(onehot.astype(F32), axis=0, keepdims=True)
    carry_ref[...] = carry
    cnt_ref[...] = carry

    fields = (x1.astype(F32), x2.astype(F32), r1, r2, w1, w2)
    slab = jnp.zeros((tm, LANES), F32)
    for idx, val in enumerate(fields):
        slab = jnp.where(lane == idx, val, slab)
    route_ref[...] = slab.T[:ROUTE_ROWS, :]
    wt_ref[...] = jnp.where(lane == 0, w1, jnp.where(lane == 1, w2, 0.0))


def _merge(xp, xs, att_p, att_s, rnn_p, rnn_s, ga, gr, w_att, w_rnn, w_out, g2, w_r, b_r):
    n_p, d = xp.shape
    n = ga.shape[0]
    npt = n_p // ROW_TILE
    aw = att_p.shape[1]
    rw = rnn_p.shape[1]
    pspec = lambda w: pl.BlockSpec((ROW_TILE, w), lambda i: (jnp.minimum(i, npt - 1), 0))
    sspec = lambda w: pl.BlockSpec((ROW_TILE, w), lambda i: (0, 0))
    row = lambda w: pl.BlockSpec((ROW_TILE, w), lambda i: (i, 0))
    return pl.pallas_call(
        functools.partial(_merge_kernel, npt),
        grid=(n // ROW_TILE,),
        in_specs=[pspec(d), sspec(d), pspec(aw), sspec(aw), pspec(rw), sspec(rw), row(d), row(d),
                  _const_spec(w_att.shape), _const_spec(w_rnn.shape), _const_spec(w_out.shape),
                  _const_spec((1, d)), _const_spec(w_r.shape), _const_spec((1, LANES))],
        out_specs=[row(d), row(d // 2), pl.BlockSpec((ROUTE_ROWS, ROW_TILE), lambda i: (0, i)), row(LANES),
                   pl.BlockSpec((1, LANES), lambda i: (0, 0))],
        out_shape=[jax.ShapeDtypeStruct((n, d), F32), jax.ShapeDtypeStruct((n, d // 2), jnp.uint32),
                   jax.ShapeDtypeStruct((ROUTE_ROWS, n), F32), jax.ShapeDtypeStruct((n, LANES), F32),
                   jax.ShapeDtypeStruct((1, LANES), F32)],
        scratch_shapes=[pltpu.VMEM((1, LANES), F32), pltpu.VMEM((ROW_TILE, ROW_TILE), BF16)],
        compiler_params=_params(1),
        name="merge",
    )(xp, xs, att_p, att_s, rnn_p, rnn_s, ga, gr, w_att, w_rnn, w_out, g2, w_r, b_r)


def _row_copy(src_ref, src_row, dst_ref, dst_row, sem):
    return pltpu.make_async_copy(src_ref.at[pl.ds(src_row, 1)], dst_ref.at[pl.ds(dst_row, 1)], sem)


DISPATCH_SLOTS = 3


def _dispatch_kernel(n_tiles, pad_end_ref, padded_ref, pos0_ref, pos1_ref, hn_hbm_ref, xs_ref,
                     zero_ref, tile_ref, zero_sem, load_sem, scat_sem):
    i = pl.program_id(0)
    tm = tile_ref.shape[1]

    def load(tile, slot):
        rows = pl.ds(pl.multiple_of(tile * tm, tm), tm)
        return pltpu.make_async_copy(hn_hbm_ref.at[rows], tile_ref.at[slot], load_sem.at[slot])

    def scatter(slot):
        for t in range(tm):
            _row_copy(tile_ref.at[slot], t, xs_ref, pos0_ref[t], scat_sem.at[slot]).start(priority=0)
            _row_copy(tile_ref.at[slot], t, xs_ref, pos1_ref[t], scat_sem.at[slot]).start(priority=1)

    def drain_scatter(slot):
        for _ in range(TOP_K):
            pltpu.make_async_copy(tile_ref.at[slot], xs_ref.at[pl.ds(0, tm)], scat_sem.at[slot]).wait()

    @pl.when(i == 0)
    def _():
        load(0, 0).start()
        zero_ref[...] = jnp.zeros_like(zero_ref)

        def zero_block(start):
            start = pl.multiple_of(start, EXPERT_BLOCK)
            return pltpu.make_async_copy(zero_ref, xs_ref.at[pl.ds(start, EXPERT_BLOCK)], zero_sem)

        used_rows = pad_end_ref[N_EXPERTS - 1]
        total_rows = xs_ref.shape[0]
        for wait in (False, True):
            for e in range(N_EXPERTS):
                @pl.when(padded_ref[e] > 0)
                def _():
                    cp = zero_block(pad_end_ref[e] - EXPERT_BLOCK)
                    cp.wait() if wait else cp.start()

                @pl.when(used_rows + e * EXPERT_BLOCK < total_rows)
                def _():
                    cp = zero_block(used_rows + e * EXPERT_BLOCK)
                    cp.wait() if wait else cp.start()

    for slot in range(DISPATCH_SLOTS):
        @pl.when(((i + 1) % DISPATCH_SLOTS == slot) & (i + 1 < n_tiles))
        def _():
            pl.when(i >= DISPATCH_SLOTS - 1)(functools.partial(drain_scatter, slot))
            load(i + 1, slot).start()

    for slot in range(DISPATCH_SLOTS):
        @pl.when(i % DISPATCH_SLOTS == slot)
        def _():
            load(i, slot).wait()
            scatter(slot)

    @pl.when(i == n_tiles - 1)
    def _():
        for tile in range(max(0, n_tiles - DISPATCH_SLOTS), n_tiles):
            drain_scatter(tile % DISPATCH_SLOTS)


def _dispatch(pad_end, padded, pos0, pos1, hn, rows):
    n, d = hn.shape
    n_tiles = n // ROW_TILE
    grid_spec = pltpu.PrefetchScalarGridSpec(
        num_scalar_prefetch=2,
        grid=(n_tiles,),
        in_specs=[pl.BlockSpec((ROW_TILE,), lambda i, pe, pd: (i,), memory_space=pltpu.SMEM),
                  pl.BlockSpec((ROW_TILE,), lambda i, pe, pd: (i,), memory_space=pltpu.SMEM),
                  pl.BlockSpec(memory_space=pl.ANY)],
        out_specs=pl.BlockSpec(memory_space=pl.ANY),
        scratch_shapes=[pltpu.VMEM((EXPERT_BLOCK, d), hn.dtype), pltpu.VMEM((DISPATCH_SLOTS, ROW_TILE, d), hn.dtype),
                        pltpu.SemaphoreType.DMA(()), pltpu.SemaphoreType.DMA((DISPATCH_SLOTS,)),
                        pltpu.SemaphoreType.DMA((DISPATCH_SLOTS,))],
    )
    return pl.pallas_call(
        functools.partial(_dispatch_kernel, n_tiles),
        grid_spec=grid_spec,
        out_shape=jax.ShapeDtypeStruct((rows, d), hn.dtype),
        compiler_params=_params(1),
        name="dispatch",
    )(pad_end, padded, pos0, pos1, hn)


def _experts_kernel(be_ref, nu_ref, x_ref, wg_ref, wu_ref, wd_ref, y_ref, wg_bf, wu_bf, wd_bf):
    i = pl.program_id(0)
    used = i < nu_ref[0]

    @pl.when(used & ((i == 0) | (be_ref[i] != be_ref[jnp.maximum(i - 1, 0)])))
    def _():
        wg_bf[...] = wg_ref[0].astype(BF16)
        wu_bf[...] = wu_ref[0].astype(BF16)
        wd_bf[...] = wd_ref[0].astype(BF16)

    @pl.when(used)
    def _():
        half = wg_bf.shape[0] // 2
        x_lo, x_hi = (part.astype(BF16) for part in _unpack_bf16_halves(x_ref[...]))

        def up(w_bf):
            return (jnp.dot(x_lo, w_bf[:half, :], preferred_element_type=F32)
                    + jnp.dot(x_hi, w_bf[half:, :], preferred_element_type=F32))

        g = up(wg_bf)
        u = up(wu_bf)
        hid = (g * _sigmoid(g) * u).astype(BF16)
        y_ref[...] = _pack_bf16_halves(jnp.dot(hid, wd_bf[...], preferred_element_type=F32))

    @pl.when(jnp.logical_not(used))
    def _():
        y_ref[...] = jnp.zeros_like(y_ref)


def _experts(block_expert, n_used, xs, wg, wu, wd):
    rows, dp = xs.shape
    d = wg.shape[1]
    assert dp * 2 == d
    de = wg.shape[2]
    nb = rows // EXPERT_BLOCK
    last = lambda i, nu: jnp.minimum(i, nu[0] - 1)
    grid_spec = pltpu.PrefetchScalarGridSpec(
        num_scalar_prefetch=2,
        grid=(nb,),
        in_specs=[pl.BlockSpec((EXPERT_BLOCK, dp), lambda i, be, nu: (last(i, nu), 0)),
                  pl.BlockSpec((1, d, de), lambda i, be, nu: (be[last(i, nu)], 0, 0)),
                  pl.BlockSpec((1, d, de), lambda i, be, nu: (be[last(i, nu)], 0, 0)),
                  pl.BlockSpec((1, de, d), lambda i, be, nu: (be[last(i, nu)], 0, 0))],
        out_specs=pl.BlockSpec((EXPERT_BLOCK, dp), lambda i, be, nu: (i, 0)),
        scratch_shapes=[pltpu.VMEM((d, de), BF16), pltpu.VMEM((d, de), BF16), pltpu.VMEM((de, d), BF16)],
    )
    return pl.pallas_call(
        _experts_kernel,
        grid_spec=grid_spec,
        out_shape=jax.ShapeDtypeStruct((rows, dp), jnp.uint32),
        compiler_params=_params(1),
        name="experts",
    )(block_expert, n_used, xs, wg, wu, wd)


def _combine_kernel(n_tiles, pos0_ref, pos1_ref, next0_ref, next1_ref, h_ref, wt_ref, gf_ref, y_hbm_ref, out_ref,
                    buf_ref, sem):
    i = pl.program_id(0)
    tm = h_ref.shape[0]

    def request(p0_ref, p1_ref, slot):
        for t in range(tm):
            _row_copy(y_hbm_ref, p0_ref[t], buf_ref.at[slot, 0], t, sem.at[slot]).start(priority=0)
            _row_copy(y_hbm_ref, p1_ref[t], buf_ref.at[slot, 1], t, sem.at[slot]).start(priority=1)

    def consume(slot):
        for k in range(TOP_K):
            pltpu.make_async_copy(y_hbm_ref.at[pl.ds(0, tm)], buf_ref.at[slot, k], sem.at[slot]).wait()
        wt = wt_ref[...]
        lo0, hi0 = _unpack_bf16_halves(buf_ref[slot, 0])
        lo1, hi1 = _unpack_bf16_halves(buf_ref[slot, 1])
        w0 = wt[:, 0:1]
        w1 = wt[:, 1:2]
        moe = jnp.concatenate([lo0 * w0 + lo1 * w1, hi0 * w0 + hi1 * w1], axis=1)
        out_ref[...] = _rmsnorm(h_ref[...] + moe, gf_ref[...])

    pl.when(i == 0)(lambda: request(pos0_ref, pos1_ref, 0))
    for slot in range(2):
        pl.when((i + 1 < n_tiles) & ((i + 1) % 2 == slot))(functools.partial(request, next0_ref, next1_ref, slot))
    for slot in range(2):
        pl.when(i % 2 == slot)(functools.partial(consume, slot))


def _combine(pos0, pos1, h, wt, gf, y_sorted, row0, n_rows):
    d = h.shape[1]
    dp = y_sorted.shape[1]
    assert row0 % ROW_TILE == 0 and n_rows % ROW_TILE == 0
    blk0 = row0 // ROW_TILE
    n_tiles = n_rows // ROW_TILE
    last = blk0 + n_tiles - 1
    return pl.pallas_call(
        functools.partial(_combine_kernel, n_tiles),
        grid=(n_tiles,),
        in_specs=[pl.BlockSpec((ROW_TILE,), lambda i: (blk0 + i,), memory_space=pltpu.SMEM),
                  pl.BlockSpec((ROW_TILE,), lambda i: (blk0 + i,), memory_space=pltpu.SMEM),
                  pl.BlockSpec((ROW_TILE,), lambda i: (jnp.minimum(blk0 + i + 1, last),), memory_space=pltpu.SMEM),
                  pl.BlockSpec((ROW_TILE,), lambda i: (jnp.minimum(blk0 + i + 1, last),), memory_space=pltpu.SMEM),
                  pl.BlockSpec((ROW_TILE, d), lambda i: (blk0 + i, 0)),
                  pl.BlockSpec((ROW_TILE, LANES), lambda i: (blk0 + i, 0)),
                  _const_spec((1, d)),
                  pl.BlockSpec(memory_space=pl.ANY)],
        out_specs=pl.BlockSpec((ROW_TILE, d), lambda i: (i, 0)),
        out_shape=jax.ShapeDtypeStruct((n_rows, d), F32),
        scratch_shapes=[pltpu.VMEM((2, TOP_K, ROW_TILE, dp), jnp.uint32), pltpu.SemaphoreType.DMA((2,))],
        compiler_params=_params(1),
        name="combine",
    )(pos0, pos1, pos0, pos1, h, wt, gf, y_sorted)


def _bias_base(rel_bias, n_queries, key_offset):
    m = jnp.arange(BIAS_BASE)
    dist = (n_queries - 1) - m + key_offset
    idx = jnp.clip(dist, -MAX_REL, MAX_REL) + MAX_REL
    return rel_bias[:, idx].astype(F32)


def _block_diag(w):
    nb, bw, _ = w.shape
    eye = jnp.eye(nb, dtype=w.dtype)
    return jnp.einsum("ncd,nm->ncmd", w, eye).reshape(nb * bw, nb * bw)


def kernel(x_prompt, x_sample, cache_k, cache_v, state_conv, state_h, norm1_g, w_in, b_in, rel_bias, w_att_branch, conv_w, conv_b, w_rec_gate, b_rec_gate, w_in_gate, b_in_gate, lru_lambda, w_rnn_branch, w_out, norm2_g, w_router_group, b_router_group, w_router_expert, b_router_expert, w_e_gate, w_e_up, w_e_down, final_norm_g):
    batch, seq, d = x_prompt.shape
    dec_b, dec_t, _ = x_sample.shape
    depth = w_in.shape[0]
    assert depth == 1
    att_w = N_HEADS * HEAD_DIM
    rnn_w = conv_w.shape[2]
    cache_len = cache_k.shape[2]
    n_p = batch * seq
    n_s = dec_b * dec_t
    n = n_p + n_s
    keep = min(BAND_PAST + CHUNK, seq)
    row2 = lambda a: a.reshape(1, -1).astype(F32)

    tiles_per_seq = seq // ROW_TILE
    tail_tiles = -(-keep // ROW_TILE)
    state_tiles = tuple(b * tiles_per_seq + tiles_per_seq - tail_tiles + j
                        for b in range(batch) for j in range(tail_tiles)) + (n_p // ROW_TILE,)

    xp = x_prompt.reshape(n_p, d)
    xs = x_sample.reshape(n_s, d)
    rnn_args = (conv_w[0].astype(F32), row2(conv_b[0]),
                _block_diag(w_rec_gate[0]).astype(BF16), row2(b_rec_gate[0]),
                _block_diag(w_in_gate[0]).astype(BF16), row2(b_in_gate[0]), row2(lru_lambda[0]))
    q, k, v, kvt, ga, gr, rnn_p, conv_p, h_p, rx_s, rgate_s = _proj(
        xp, xs, row2(norm1_g[0]), w_in[0].astype(BF16), row2(b_in[0]), rnn_args, att_w, rnn_w, state_tiles, batch)

    att_p = _attn_prompt(q, k, v, _bias_base(rel_bias[0], Q_BLOCK, (KEY_BLOCKS - 1) * Q_BLOCK), batch, seq)
    att_s, new_k_s, new_v_s = _attn_sample(
        q, k, v, kvt, cache_k[0].reshape(dec_b, cache_len, att_w), cache_v[0].reshape(dec_b, cache_len, att_w),
        _bias_base(rel_bias[0], dec_t, cache_len), n_p, batch * tail_tiles * ROW_TILE, dec_b, dec_t)

    pad_state = lambda s: jnp.pad(s.astype(F32), ((0, 0), (SUBLANES - (CONV_W - 1), 0), (0, 0)))
    rnn_s, conv_s, h_s = _rnn_sample(rx_s, rgate_s, pad_state(state_conv[0]), state_h[0].astype(F32)[:, None, :],
                                     *rnn_args, dec_b, dec_t)

    w_r = jnp.zeros((d, LANES), F32).at[:, :N_GROUPS].set(w_router_group[0])
    w_r = w_r.at[:, N_GROUPS:N_GROUPS + N_EXPERTS].set(w_router_expert[0]).astype(BF16)
    b_r = jnp.zeros((1, LANES), F32).at[0, :N_GROUPS].set(b_router_group[0])
    b_r = b_r.at[0, N_GROUPS:N_GROUPS + N_EXPERTS].set(b_router_expert[0])
    h, hn, route, wt, cnt = _merge(xp, xs, att_p, att_s, rnn_p, rnn_s, ga, gr,
                                   w_att_branch[0].astype(BF16), w_rnn_branch[0].astype(BF16), w_out[0].astype(BF16),
                                   row2(norm2_g[0]), w_r, b_r)

    counts = cnt[0, :N_EXPERTS].astype(jnp.int32)
    padded = (counts + EXPERT_BLOCK - 1) // EXPERT_BLOCK * EXPERT_BLOCK
    pad_end = jnp.cumsum(padded)
    pad_start = pad_end - padded
    expert = route[0:TOP_K].astype(jnp.int32)
    rank = route[TOP_K:2 * TOP_K].astype(jnp.int32)
    ids = jnp.arange(N_EXPERTS, dtype=jnp.int32)[:, None, None]
    pos = jnp.sum(jnp.where(expert[None] == ids, pad_start[:, None, None], 0), axis=0) + rank
    n_blocks = -(-(n * TOP_K) // EXPERT_BLOCK) + N_EXPERTS
    rows = n_blocks * EXPERT_BLOCK
    block_start = jnp.arange(n_blocks, dtype=jnp.int32) * EXPERT_BLOCK
    block_expert = jnp.minimum(jnp.sum((block_start[:, None] >= pad_end[None, :]).astype(jnp.int32), axis=1),
                               N_EXPERTS - 1)
    n_used = (pad_end[-1:] // EXPERT_BLOCK).astype(jnp.int32)

    x_sorted = _dispatch(pad_end, padded, pos[0], pos[1], hn, rows)
    y_sorted = _experts(block_expert, n_used, x_sorted, w_e_gate[0], w_e_up[0], w_e_down[0])
    gf = row2(final_norm_g)
    y_p = _combine(pos[0], pos[1], h, wt, gf, y_sorted, 0, n_p)
    y_s = _combine(pos[0], pos[1], h, wt, gf, y_sorted, n_p, n_s)

    def kv_state(col0):
        span = tail_tiles * ROW_TILE
        tails = [kvt[b * span + span - keep:(b + 1) * span, col0:col0 + att_w] for b in range(batch)]
        return jnp.stack(tails).reshape(1, batch, keep, N_HEADS, HEAD_DIM)

    heads = lambda a: a.reshape(1, dec_b, cache_len, N_HEADS, HEAD_DIM)
    return (y_p.reshape(batch, seq, d), y_s.reshape(dec_b, dec_t, d),
            kv_state(0), kv_state(att_w),
            conv_p[:, SUBLANES - (CONV_W - 1):][None], h_p[:, 0][None],
            heads(new_k_s), heads(new_v_s),
            conv_s[:, SUBLANES - (CONV_W - 1):][None], h_s[:, 0][None])
```

```python
import functools

import jax
import jax.numpy as jnp
from jax import lax
from jax.experimental import pallas as pl
from jax.experimental.pallas import tpu as pltpu

CHUNK = 64
LEFT_CHUNKS = 8
BAND_PAST = LEFT_CHUNKS * CHUNK
N_HEADS = 8
HEAD_DIM = 64
MAX_REL = 256
NEG_INF = -1e30
CONV_W = 4
RG_C = 8.0
N_GROUPS = 4
EXPERTS_PER_GROUP = 8
N_EXPERTS = N_GROUPS * EXPERTS_PER_GROUP
TOP_K = 2
EPS = 1e-6

LANES = 128
SUBLANES = 8
ROW_TILE = 512
Q_BLOCK = 256
KEY_BLOCKS = 3
KEY_WINDOW = KEY_BLOCKS * Q_BLOCK
BIAS_BASE = 1024
SOFTMAX_SPLIT = 2
EXPERT_BLOCK = 512
VMEM_LIMIT = 56 * 1024 * 1024

BF16 = jnp.bfloat16
F32 = jnp.float32


def _params(n_axes, vmem=VMEM_LIMIT):
    return pltpu.CompilerParams(dimension_semantics=("arbitrary",) * n_axes, vmem_limit_bytes=vmem)


def _const_spec(shape):
    zeros = (0,) * len(shape)
    return pl.BlockSpec(shape, lambda *_: zeros, pipeline_mode=pl.Buffered(1))


def _rmsnorm(x, g):
    return x * lax.rsqrt(jnp.mean(x * x, axis=-1, keepdims=True) + EPS) * g


def _pack_bf16_halves(x):
    c = x.shape[1] // 2
    lo = lax.bitcast_convert_type(x[:, :c].astype(BF16).astype(F32), jnp.uint32)
    hi = lax.bitcast_convert_type(x[:, c:].astype(BF16).astype(F32), jnp.uint32)
    return (lo >> 16) | (hi & jnp.uint32(0xFFFF0000))


def _unpack_bf16_halves(p):
    lo = lax.bitcast_convert_type(p << 16, F32)
    hi = lax.bitcast_convert_type(p & jnp.uint32(0xFFFF0000), F32)
    return lo, hi


def _sigmoid(x):
    return 0.5 * jnp.tanh(0.5 * x) + 0.5


def _proj_kernel(n_prompt_tiles, tiles_per_seq, state_tiles, att_w, rnn_w, d_model,
                 xp_ref, xs_ref, g_ref, w_ref, b_ref, cw_ref, cb_ref, wa_ref, ba_ref, wx_ref, bx_ref, lam_ref,
                 q_ref, k_ref, v_ref, kvt_ref, ga_ref, gr_ref, rnn_ref, conv_out_ref, h_out_ref, rxs_ref, rgs_ref,
                 rx_ref, tail_ref, hl_ref, p_ref, hs_ref, h_ref):
    i = pl.program_id(0)
    tm = xp_ref.shape[0]
    x = jnp.where(i < n_prompt_tiles, xp_ref[...], xs_ref[...])
    xn = _rmsnorm(x, g_ref[...]).astype(BF16)

    def seg(lo, width):
        return jnp.dot(xn, w_ref[:, lo:lo + width], preferred_element_type=F32) + b_ref[:, lo:lo + width]

    q = seg(0, att_w)
    k = seg(att_w, att_w)
    v = seg(2 * att_w, att_w)
    q_ref[...] = (q * (HEAD_DIM ** -0.5)).astype(BF16)
    k_ref[...] = k.astype(BF16)
    v_ref[...] = v.astype(BF16)

    @pl.when(functools.reduce(jnp.logical_or, [i == t for t in state_tiles]))
    def _():
        kvt_ref[:, :att_w] = k
        kvt_ref[:, att_w:] = v

    o = 3 * att_w
    ncol = rnn_w // LANES
    rx = seg(o, rnn_w)
    for c in range(ncol):
        rx_ref[c] = rx[:, c * LANES:(c + 1) * LANES]
    rg = seg(o + rnn_w, rnn_w)

    @pl.when(i % tiles_per_seq == 0)
    def _():
        tail_ref[...] = jnp.zeros_like(tail_ref)
        h_ref[...] = jnp.zeros_like(h_ref)

    hseq, tail, h_last = _rglru_prompt_tile(tm, rx_ref, cw_ref, cb_ref, wa_ref, ba_ref, wx_ref, bx_ref, lam_ref,
                                            tail_ref, hl_ref, p_ref, hs_ref, h_ref)
    rnn_ref[...] = (hseq * jax.nn.gelu(rg)).astype(BF16)
    ga_ref[...] = seg(o + 2 * rnn_w, d_model).astype(BF16)
    gr_ref[...] = seg(o + 2 * rnn_w + d_model, d_model).astype(BF16)

    @pl.when(i < n_prompt_tiles)
    def _():
        conv_out_ref[0] = tail
        h_out_ref[0] = h_last

    @pl.when(i == n_prompt_tiles)
    def _():
        for c in range(ncol):
            rxs_ref[c] = rx_ref[c]
        rgs_ref[...] = rg.astype(BF16)


def _proj(xp, xs, g1, w_in, b_in, rnn_args, att_w, rnn_w, state_tiles, batch):
    n_p, d = xp.shape
    n_s = xs.shape[0]
    assert n_p % ROW_TILE == 0 and n_s == ROW_TILE
    assert ROW_TILE % (SUBLANES * SUBLANES) == 0 and ROW_TILE // SUBLANES >= CONV_W
    n = n_p + n_s
    npt = n_p // ROW_TILE
    tps = npt // batch
    pw = w_in.shape[1]
    ncol = rnn_w // LANES
    row = lambda w: pl.BlockSpec((ROW_TILE, w), lambda i: (i, 0))
    rows = lambda w, dt: jax.ShapeDtypeStruct((n, w), dt)
    slot = lambda i: sum(jnp.where(i > t, 1, 0) for t in state_tiles)
    state = lambda r: pl.BlockSpec((1, r, rnn_w), lambda i: (jnp.minimum(i // tps, batch - 1), 0, 0))
    out_specs = [row(att_w), row(att_w), row(att_w),
                 pl.BlockSpec((ROW_TILE, 2 * att_w), lambda i: (slot(i), 0)),
                 row(d), row(d), row(rnn_w), state(SUBLANES), state(1),
                 pl.BlockSpec((ncol, ROW_TILE, LANES), lambda i: (0, 0, 0)),
                 pl.BlockSpec((ROW_TILE, rnn_w), lambda i: (0, 0))]
    out_shape = [rows(att_w, BF16), rows(att_w, BF16), rows(att_w, BF16),
                 jax.ShapeDtypeStruct((len(state_tiles) * ROW_TILE, 2 * att_w), F32),
                 rows(d, BF16), rows(d, BF16), rows(rnn_w, BF16),
                 jax.ShapeDtypeStruct((batch, SUBLANES, rnn_w), F32), jax.ShapeDtypeStruct((batch, 1, rnn_w), F32),
                 jax.ShapeDtypeStruct((ncol, n_s, LANES), F32), jax.ShapeDtypeStruct((n_s, rnn_w), BF16)]
    tile = lambda: pltpu.VMEM((ROW_TILE, rnn_w), F32)
    cols = lambda: pltpu.VMEM((ncol, ROW_TILE, LANES), F32)
    return pl.pallas_call(
        functools.partial(_proj_kernel, npt, tps, state_tiles, att_w, rnn_w, d),
        grid=(n // ROW_TILE,),
        in_specs=[pl.BlockSpec((ROW_TILE, d), lambda i: (jnp.minimum(i, npt - 1), 0)),
                  pl.BlockSpec((ROW_TILE, d), lambda i: (0, 0)),
                  _const_spec((1, d)), _const_spec((d, pw)), _const_spec((1, pw))]
                 + [_const_spec(a.shape) for a in rnn_args],
        out_specs=out_specs,
        out_shape=out_shape,
        scratch_shapes=[cols(), pltpu.VMEM((SUBLANES, rnn_w), F32), tile(), tile(), cols(),
                        pltpu.VMEM((1, rnn_w), F32)],
        compiler_params=_params(1),
        name="proj_rglru",
    )(xp, xs, g1, w_in, b_in, *rnn_args)


def _attn_p_kernel(q_ref, k0_ref, k1_ref, k2_ref, v0_ref, v1_ref, v2_ref, base_ref, o_ref, bias_ref):
    b = pl.program_id(0)
    i = pl.program_id(1)

    @pl.when((b == 0) & (i == 0))
    def _():
        r = lax.broadcasted_iota(jnp.int32, (Q_BLOCK, KEY_WINDOW), 0) // CHUNK
        c = lax.broadcasted_iota(jnp.int32, (Q_BLOCK, KEY_WINDOW), 1) // CHUNK
        in_band = (c >= r) & (c <= r + LEFT_CHUNKS)
        for h in range(N_HEADS):
            rows = jnp.broadcast_to(base_ref[h:h + 1, :], (Q_BLOCK, BIAS_BASE))
            toeplitz = pltpu.roll(rows, BIAS_BASE - (Q_BLOCK - 1), axis=1, stride=1, stride_axis=0)
            bias_ref[h] = jnp.where(in_band, toeplitz[:, :KEY_WINDOW], NEG_INF)

    kwin = jnp.concatenate([k0_ref[...], k1_ref[...], k2_ref[...]], axis=0)
    vwin = jnp.concatenate([v0_ref[...], v1_ref[...], v2_ref[...]], axis=0)
    lane = lax.broadcasted_iota(jnp.int32, (1, LANES), 1)

    def heads(at_sequence_start):
        if at_sequence_start:
            col = lax.broadcasted_iota(jnp.int32, (1, KEY_WINDOW), 1)
            start_mask = jnp.where(col // Q_BLOCK + i - (KEY_BLOCKS - 1) >= 0, 0.0, NEG_INF).astype(F32)
        for pair in range(N_HEADS // 2):
            sl = slice(pair * LANES, (pair + 1) * LANES)
            q2 = q_ref[:, sl]
            k2 = kwin[:, sl]
            v2 = vwin[:, sl]
            acc = [jnp.zeros((Q_BLOCK // SOFTMAX_SPLIT, LANES), F32) for _ in range(SOFTMAX_SPLIT)]
            for half in range(2):
                hmask = (lane // HEAD_DIM) == half
                qm = jnp.where(hmask, q2, jnp.zeros_like(q2))
                vm = jnp.where(hmask, v2, jnp.zeros_like(v2))
                s_all = lax.dot_general(qm, k2, (((1,), (1,)), ((), ())), preferred_element_type=F32)
                for part in range(SOFTMAX_SPLIT):
                    rows = slice(part * (Q_BLOCK // SOFTMAX_SPLIT), (part + 1) * (Q_BLOCK // SOFTMAX_SPLIT))
                    s = s_all[rows] + bias_ref[2 * pair + half, rows, :]
                    if at_sequence_start:
                        s = s + start_mask
                    m = jnp.max(s, axis=-1, keepdims=True)
                    e = jnp.exp(s - m)
                    l = jnp.sum(e, axis=-1, keepdims=True)
                    o = jnp.dot(e.astype(BF16), vm, preferred_element_type=F32)
                    acc[part] = acc[part] + o / l
            o_ref[:, sl] = jnp.concatenate(acc, axis=0).astype(BF16)

    pl.when(i < KEY_BLOCKS - 1)(lambda: heads(True))
    pl.when(i >= KEY_BLOCKS - 1)(lambda: heads(False))


def _attn_prompt(q, k, v, base, batch, seq):
    w = q.shape[1]
    nqb = seq // Q_BLOCK
    kspec = lambda back: pl.BlockSpec((Q_BLOCK, w), lambda b, i: (b * nqb + jnp.maximum(i - back, 0), 0))
    return pl.pallas_call(
        _attn_p_kernel,
        grid=(batch, nqb),
        in_specs=[pl.BlockSpec((Q_BLOCK, w), lambda b, i: (b * nqb + i, 0)),
                  kspec(2), kspec(1), kspec(0), kspec(2), kspec(1), kspec(0),
                  _const_spec(base.shape)],
        out_specs=pl.BlockSpec((Q_BLOCK, w), lambda b, i: (b * nqb + i, 0)),
        out_shape=jax.ShapeDtypeStruct((batch * seq, w), BF16),
        scratch_shapes=[pltpu.VMEM((N_HEADS, Q_BLOCK, KEY_WINDOW), F32)],
        compiler_params=_params(2),
        name="attn_prompt",
    )(q, k, k, k, v, v, v, base)


def _attn_s_kernel(t, q_ref, k_ref, v_ref, kvt_ref, ck_ref, cv_ref, base_ref, o_ref, nk_ref, nv_ref, bias_ref):
    cache_len = ck_ref.shape[1]
    w = q_ref.shape[1]

    @pl.when(pl.program_id(0) == 0)
    def _():
        col = lax.broadcasted_iota(jnp.int32, (t, cache_len + t), 1)
        for h in range(N_HEADS):
            rows = jnp.broadcast_to(base_ref[h:h + 1, :], (t, BIAS_BASE))
            toeplitz = pltpu.roll(rows, BIAS_BASE - (t - 1), axis=1, stride=1, stride_axis=0)
            bias_ref[h * t:(h + 1) * t, :] = jnp.where(col >= cache_len - BAND_PAST, toeplitz[:, :cache_len + t],
                                                       NEG_INF)

    ck = ck_ref[0]
    cv = cv_ref[0]
    kall = jnp.concatenate([ck.astype(BF16), k_ref[...]], axis=0)
    vall = jnp.concatenate([cv.astype(BF16), v_ref[...]], axis=0)
    q = q_ref[...]
    head_of_lane = lax.broadcasted_iota(jnp.int32, (1, w), 1) // HEAD_DIM
    qs = jnp.concatenate([jnp.where(head_of_lane == h, q, jnp.zeros_like(q)) for h in range(N_HEADS)], axis=0)
    s = lax.dot_general(qs, kall, (((1,), (1,)), ((), ())), preferred_element_type=F32) + bias_ref[...]
    m = jnp.max(s, axis=-1, keepdims=True)
    e = jnp.exp(s - m)
    l = jnp.sum(e, axis=-1, keepdims=True)
    o_all = jnp.dot(e.astype(BF16), vall, preferred_element_type=F32) / l
    o = jnp.zeros((t, w), F32)
    for h in range(N_HEADS):
        o = o + jnp.where(head_of_lane == h, o_all[h * t:(h + 1) * t], 0.0)
    o_ref[...] = o.astype(BF16)
    nk_ref[0, :cache_len - t, :] = ck[t:]
    nk_ref[0, cache_len - t:, :] = kvt_ref[:, :w]
    nv_ref[0, :cache_len - t, :] = cv[t:]
    nv_ref[0, cache_len - t:, :] = kvt_ref[:, w:]


def _attn_sample(q, k, v, kvt, cache_k, cache_v, base, row0, kvt_row0, dec_b, t):
    w = q.shape[1]
    cache_len = cache_k.shape[1]
    assert row0 % t == 0 and kvt_row0 % t == 0 and cache_len + 2 * t - 1 <= BIAS_BASE
    tok = lambda width, r0: pl.BlockSpec((t, width), lambda b: (r0 // t + b, 0))
    cache = pl.BlockSpec((1, cache_len, w), lambda b: (b, 0, 0))
    return pl.pallas_call(
        functools.partial(_attn_s_kernel, t),
        grid=(dec_b,),
        in_specs=[tok(w, row0), tok(w, row0), tok(w, row0), tok(2 * w, kvt_row0), cache, cache,
                  _const_spec(base.shape)],
        out_specs=[pl.BlockSpec((t, w), lambda b: (b, 0)), cache, cache],
        out_shape=[jax.ShapeDtypeStruct((dec_b * t, w), BF16),
                   jax.ShapeDtypeStruct(cache_k.shape, F32), jax.ShapeDtypeStruct(cache_v.shape, F32)],
        scratch_shapes=[pltpu.VMEM((N_HEADS * t, cache_len + t), F32)],
        compiler_params=_params(1),
        name="attn_sample",
    )(q, k, v, kvt, cache_k, cache_v, base)


def _rnn_sample_kernel(t, rx_ref, rg_ref, conv0_ref, h0_ref, cw_ref, cb_ref, wa_ref, ba_ref, wx_ref, bx_ref, lam_ref,
                       rnn_ref, conv_out_ref, h_out_ref, ext_ref, hs_ref):
    streams = conv0_ref.shape[0]
    groups = t // SUBLANES
    rx = jnp.concatenate([rx_ref[c] for c in range(rx_ref.shape[0])], axis=1)
    cw = cw_ref[...]
    xc = []
    for s in range(streams):
        ext_ref[s, :SUBLANES, :] = conv0_ref[s]
        ext_ref[s, SUBLANES:, :] = rx[s * t:(s + 1) * t]
        acc = cb_ref[...]
        for tap in range(CONV_W):
            off = SUBLANES - (CONV_W - 1) + tap
            acc = acc + ext_ref[s, off:off + t, :] * cw[tap:tap + 1, :]
        xc.append(acc)
        conv_out_ref[s] = ext_ref[s, t:t + SUBLANES, :]

    a, b = _rglru_coeffs(jnp.concatenate(xc, axis=0), wa_ref, ba_ref, wx_ref, bx_ref, lam_ref)

    sub = lax.broadcasted_iota(jnp.int32, a.shape, 0) % SUBLANES
    shift = 1
    while shift < SUBLANES:
        keep = sub >= shift
        a_prev = pltpu.roll(a, shift, axis=0)
        b_prev = pltpu.roll(b, shift, axis=0)
        b = jnp.where(keep, a * b_prev + b, b)
        a = jnp.where(keep, a * a_prev, a)
        shift *= 2

    for s in range(streams):
        h = h0_ref[s]
        for g in range(groups):
            rows = slice(s * t + g * SUBLANES, s * t + (g + 1) * SUBLANES)
            hg = b[rows] + a[rows] * h
            hs_ref[rows, :] = hg
            h = hg[SUBLANES - 1:SUBLANES, :]
        h_out_ref[s] = h
    rnn_ref[...] = (hs_ref[...] * jax.nn.gelu(rg_ref[...].astype(F32))).astype(BF16)


def _rnn_sample(rx, rgate, conv0, h0, cw, cb, wa, ba, wx, bx, lam, streams, t):
    n, w = rgate.shape
    assert n == streams * t and t % SUBLANES == 0 and rx.shape == (w // LANES, n, LANES)
    full = lambda shape: pl.BlockSpec(shape, lambda i: (0,) * len(shape))
    return pl.pallas_call(
        functools.partial(_rnn_sample_kernel, t),
        grid=(1,),
        in_specs=[full(rx.shape), full((n, w)), full((streams, SUBLANES, w)), full((streams, 1, w)),
                  full(cw.shape), full((1, w)), full((w, w)), full((1, w)), full((w, w)), full((1, w)), full((1, w))],
        out_specs=[full((n, w)), full((streams, SUBLANES, w)), full((streams, 1, w))],
        out_shape=[jax.ShapeDtypeStruct((n, w), BF16),
                   jax.ShapeDtypeStruct((streams, SUBLANES, w), F32),
                   jax.ShapeDtypeStruct((streams, 1, w), F32)],
        scratch_shapes=[pltpu.VMEM((streams, t + SUBLANES, w), F32), pltpu.VMEM((n, w), F32)],
        compiler_params=_params(1),
        name="rnn_sample",
    )(rx, rgate, conv0, h0, cw, cb, wa, ba, wx, bx, lam)


def _rglru_coeffs(xc, wa_ref, ba_ref, wx_ref, bx_ref, lam_ref):
    xb = xc.astype(BF16)
    r = _sigmoid(jnp.dot(xb, wa_ref[...], preferred_element_type=F32) + ba_ref[...])
    gi = _sigmoid(jnp.dot(xb, wx_ref[...], preferred_element_type=F32) + bx_ref[...])
    neg_lam = -lam_ref[...]
    softplus = jnp.maximum(neg_lam, 0.0) + jnp.log1p(jnp.exp(-jnp.abs(neg_lam)))
    log_a = (-RG_C) * r * softplus
    a = jnp.exp(log_a)
    b = jnp.sqrt(-jnp.tanh(log_a) * (a * a + 1.0)) * (gi * xc)
    return a, b


def _rglru_prompt_tile(tm, rx_ref, cw_ref, cb_ref, wa_ref, ba_ref, wx_ref, bx_ref, lam_ref,
                       tail_ref, hl_ref, p_ref, hs_ref, h_ref):
    seg = tm // SUBLANES
    ncol = rx_ref.shape[0]
    w = ncol * LANES

    def strided_rows(ref, j):
        return jnp.concatenate([ref[c, pl.ds(j, SUBLANES, stride=seg), :] for c in range(ncol)], axis=1)

    slab = [strided_rows(rx_ref, j) for j in range(seg)]
    sub = lax.broadcasted_iota(jnp.int32, (SUBLANES, w), 0)

    def before(k):
        carry = jnp.broadcast_to(tail_ref[SUBLANES - k:SUBLANES - k + 1, :], (SUBLANES, w))
        return jnp.where(sub == 0, carry, pltpu.roll(slab[seg - k], 1, axis=0))

    lead = {k: before(k) for k in range(1, CONV_W)}
    cw = cw_ref[...]
    cb = cb_ref[...]
    xc = []
    for j in range(seg):
        acc = cb + slab[j] * cw[CONV_W - 1:CONV_W, :]
        for k in range(1, CONV_W):
            prev = slab[j - k] if j >= k else lead[k - j]
            acc = acc + prev * cw[CONV_W - 1 - k:CONV_W - k, :]
        xc.append(acc)
    tail = jnp.concatenate([rx_ref[c, tm - SUBLANES:tm, :] for c in range(ncol)], axis=1)
    tail_ref[...] = tail

    a, b = _rglru_coeffs(jnp.concatenate(xc, axis=0), wa_ref, ba_ref, wx_ref, bx_ref, lam_ref)

    h = jnp.zeros((SUBLANES, w), F32)
    p = jnp.ones((SUBLANES, w), F32)
    for j in range(seg):
        rows = slice(j * SUBLANES, (j + 1) * SUBLANES)
        h = a[rows] * h + b[rows]
        p = a[rows] * p
        hl_ref[rows, :] = h
        p_ref[rows, :] = p
    c = h_ref[...]
    enter = []
    for s in range(SUBLANES):
        enter.append(c)
        c = h[s:s + 1, :] + p[s:s + 1, :] * c
    last = c
    h_ref[...] = last
    enter = jnp.concatenate(enter, axis=0)
    for j in range(seg):
        rows = slice(j * SUBLANES, (j + 1) * SUBLANES)
        hj = hl_ref[rows, :] + p_ref[rows, :] * enter
        for col in range(ncol):
            hs_ref[col, pl.ds(j, SUBLANES, stride=seg), :] = hj[:, col * LANES:(col + 1) * LANES]
    hseq = jnp.concatenate([hs_ref[col] for col in range(ncol)], axis=1)
    return hseq, tail, last


ROUTE_ROWS = SUBLANES


def _merge_kernel(n_prompt_tiles,
                  xp_ref, xs_ref, attp_ref, atts_ref, rnnp_ref, rnns_ref, ga_ref, gr_ref,
                  watt_ref, wrnn_ref, wout_ref, g2_ref, wr_ref, br_ref,
                  h_ref, hn_ref, route_ref, wt_ref, cnt_ref, carry_ref, tri_ref):
    i = pl.program_id(0)
    is_p = i < n_prompt_tiles
    tm = xp_ref.shape[0]

    @pl.when(i == 0)
    def _():
        carry_ref[...] = jnp.zeros_like(carry_ref)
        rowi = lax.broadcasted_iota(jnp.int32, (tm, tm), 0)
        coli = lax.broadcasted_iota(jnp.int32, (tm, tm), 1)
        tri_ref[...] = jnp.where(coli < rowi, 1.0, 0.0).astype(BF16)

    carry = carry_ref[...]
    x = jnp.where(is_p, xp_ref[...], xs_ref[...])
    att = jnp.where(is_p, attp_ref[...], atts_ref[...])
    rnn = jnp.where(is_p, rnnp_ref[...], rnns_ref[...])
    ya = jnp.dot(att, watt_ref[...], preferred_element_type=F32)
    yr = jnp.dot(rnn, wrnn_ref[...], preferred_element_type=F32)
    mixed = _sigmoid(ga_ref[...].astype(F32)) * ya + _sigmoid(gr_ref[...].astype(F32)) * yr
    h = x + jnp.dot(mixed.astype(BF16), wout_ref[...], preferred_element_type=F32)
    h_ref[...] = h
    hn = _rmsnorm(h, g2_ref[...])
    hn_ref[...] = _pack_bf16_halves(hn)

    logits = jnp.dot(hn.astype(BF16), wr_ref[...], preferred_element_type=F32) + br_ref[...]
    lane = lax.broadcasted_iota(jnp.int32, (tm, LANES), 1)
    low = jnp.float32(-3e38)
    big = jnp.int32(1 << 20)
    lg = jnp.where(lane < N_GROUPS, logits, low)
    mg = jnp.max(lg, axis=-1, keepdims=True)
    p_grp = 1.0 / jnp.sum(jnp.exp(lg - mg), axis=-1, keepdims=True)
    grp = jnp.min(jnp.where(lg == mg, lane, big), axis=-1, keepdims=True)
    lo = N_GROUPS + EXPERTS_PER_GROUP * grp
    le = jnp.where((lane >= lo) & (lane < lo + EXPERTS_PER_GROUP), logits, low)
    m1 = jnp.max(le, axis=-1, keepdims=True)
    i1 = jnp.min(jnp.where(le == m1, lane, big), axis=-1, keepdims=True)
    le2 = jnp.where(lane == i1, low, le)
    m2 = jnp.max(le2, axis=-1, keepdims=True)
    i2 = jnp.min(jnp.where(le2 == m2, lane, big), axis=-1, keepdims=True)
    e2 = jnp.exp(m2 - m1)
    inv = 1.0 / (1.0 + e2)
    w1 = p_grp * inv
    w2 = p_grp * (e2 * inv)
    x1 = i1 - N_GROUPS
    x2 = i2 - N_GROUPS

    onehot = jnp.where((lane == x1) | (lane == x2), 1.0, 0.0).astype(BF16)
    before = jnp.dot(tri_ref[...], onehot, preferred_element_type=F32) + carry
    r1 = jnp.sum(jnp.where(lane == x1, before, 0.0), axis=-1, keepdims=True)
    r2 = jnp.sum(jnp.where(lane == x2, before, 0.0), axis=-1, keepdims=True)
    carry = carry + jnp.sum(onehot.astype(F32), axis=0, keepdims=True)
    carry_ref[...] = carry
    cnt_ref[...] = carry

    fields = (x1.astype(F32), x2.astype(F32), r1, r2, w1, w2)
    slab = jnp.zeros((tm, LANES), F32)
    for idx, val in enumerate(fields):
        slab = jnp.where(lane == idx, val, slab)
    route_ref[...] = slab.T[:ROUTE_ROWS, :]
    wt_ref[...] = jnp.where(lane == 0, w1, jnp.where(lane == 1, w2, 0.0))


def _merge(xp, xs, att_p, att_s, rnn_p, rnn_s, ga, gr, w_att, w_rnn, w_out, g2, w_r, b_r):
    n_p, d = xp.shape
    n = ga.shape[0]
    npt = n_p // ROW_TILE
    aw = att_p.shape[1]
    rw = rnn_p.shape[1]
    pspec = lambda w: pl.BlockSpec((ROW_TILE, w), lambda i: (jnp.minimum(i, npt - 1), 0))
    sspec = lambda w: pl.BlockSpec((ROW_TILE, w), lambda i: (0, 0))
    row = lambda w: pl.BlockSpec((ROW_TILE, w), lambda i: (i, 0))
    return pl.pallas_call(
        functools.partial(_merge_kernel, npt),
        grid=(n // ROW_TILE,),
        in_specs=[pspec(d), sspec(d), pspec(aw), sspec(aw), pspec(rw), sspec(rw), row(d), row(d),
                  _const_spec(w_att.shape), _const_spec(w_rnn.shape), _const_spec(w_out.shape),
                  _const_spec((1, d)), _const_spec(w_r.shape), _const_spec((1, LANES))],
        out_specs=[row(d), row(d // 2), pl.BlockSpec((ROUTE_ROWS, ROW_TILE), lambda i: (0, i)), row(LANES),
                   pl.BlockSpec((1, LANES), lambda i: (0, 0))],
        out_shape=[jax.ShapeDtypeStruct((n, d), F32), jax.ShapeDtypeStruct((n, d // 2), jnp.uint32),
                   jax.ShapeDtypeStruct((ROUTE_ROWS, n), F32), jax.ShapeDtypeStruct((n, LANES), F32),
                   jax.ShapeDtypeStruct((1, LANES), F32)],
        scratch_shapes=[pltpu.VMEM((1, LANES), F32), pltpu.VMEM((ROW_TILE, ROW_TILE), BF16)],
        compiler_params=_params(1),
        name="merge",
    )(xp, xs, att_p, att_s, rnn_p, rnn_s, ga, gr, w_att, w_rnn, w_out, g2, w_r, b_r)


def _row_copy(src_ref, src_row, dst_ref, dst_row, sem):
    return pltpu.make_async_copy(src_ref.at[pl.ds(src_row, 1)], dst_ref.at[pl.ds(dst_row, 1)], sem)


DISPATCH_SLOTS = 3


def _dispatch_kernel(n_tiles, pad_end_ref, padded_ref, pos0_ref, pos1_ref, hn_hbm_ref, xs_ref,
                     zero_ref, tile_ref, zero_sem, load_sem, scat_sem):
    i = pl.program_id(0)
    tm = tile_ref.shape[1]

    def load(tile, slot):
        rows = pl.ds(pl.multiple_of(tile * tm, tm), tm)
        return pltpu.make_async_copy(hn_hbm_ref.at[rows], tile_ref.at[slot], load_sem.at[slot])

    def scatter(slot):
        for t in range(tm):
            _row_copy(tile_ref.at[slot], t, xs_ref, pos0_ref[t], scat_sem.at[slot]).start(priority=0)
            _row_copy(tile_ref.at[slot], t, xs_ref, pos1_ref[t], scat_sem.at[slot]).start(priority=1)

    def drain_scatter(slot):
        for _ in range(TOP_K):
            pltpu.make_async_copy(tile_ref.at[slot], xs_ref.at[pl.ds(0, tm)], scat_sem.at[slot]).wait()

    @pl.when(i == 0)
    def _():
        load(0, 0).start()
        zero_ref[...] = jnp.zeros_like(zero_ref)

        def zero_block(start):
            start = pl.multiple_of(start, EXPERT_BLOCK)
            return pltpu.make_async_copy(zero_ref, xs_ref.at[pl.ds(start, EXPERT_BLOCK)], zero_sem)

        used_rows = pad_end_ref[N_EXPERTS - 1]
        total_rows = xs_ref.shape[0]
        for wait in (False, True):
            for e in range(N_EXPERTS):
                @pl.when(padded_ref[e] > 0)
                def _():
                    cp = zero_block(pad_end_ref[e] - EXPERT_BLOCK)
                    cp.wait() if wait else cp.start()

                @pl.when(used_rows + e * EXPERT_BLOCK < total_rows)
                def _():
                    cp = zero_block(used_rows + e * EXPERT_BLOCK)
                    cp.wait() if wait else cp.start()

    for slot in range(DISPATCH_SLOTS):
        @pl.when(((i + 1) % DISPATCH_SLOTS == slot) & (i + 1 < n_tiles))
        def _():
            pl.when(i >= DISPATCH_SLOTS - 1)(functools.partial(drain_scatter, slot))
            load(i + 1, slot).start()

    for slot in range(DISPATCH_SLOTS):
        @pl.when(i % DISPATCH_SLOTS == slot)
        def _():
            load(i, slot).wait()
            scatter(slot)

    @pl.when(i == n_tiles - 1)
    def _():
        for tile in range(max(0, n_tiles - DISPATCH_SLOTS), n_tiles):
            drain_scatter(tile % DISPATCH_SLOTS)


def _dispatch(pad_end, padded, pos0, pos1, hn, rows):
    n, d = hn.shape
    n_tiles = n // ROW_TILE
    grid_spec = pltpu.PrefetchScalarGridSpec(
        num_scalar_prefetch=2,
        grid=(n_tiles,),
        in_specs=[pl.BlockSpec((ROW_TILE,), lambda i, pe, pd: (i,), memory_space=pltpu.SMEM),
                  pl.BlockSpec((ROW_TILE,), lambda i, pe, pd: (i,), memory_space=pltpu.SMEM),
                  pl.BlockSpec(memory_space=pl.ANY)],
        out_specs=pl.BlockSpec(memory_space=pl.ANY),
        scratch_shapes=[pltpu.VMEM((EXPERT_BLOCK, d), hn.dtype), pltpu.VMEM((DISPATCH_SLOTS, ROW_TILE, d), hn.dtype),
                        pltpu.SemaphoreType.DMA(()), pltpu.SemaphoreType.DMA((DISPATCH_SLOTS,)),
                        pltpu.SemaphoreType.DMA((DISPATCH_SLOTS,))],
    )
    return pl.pallas_call(
        functools.partial(_dispatch_kernel, n_tiles),
        grid_spec=grid_spec,
        out_shape=jax.ShapeDtypeStruct((rows, d), hn.dtype),
        compiler_params=_params(1),
        name="dispatch",
    )(pad_end, padded, pos0, pos1, hn)


def _experts_kernel(be_ref, nu_ref, x_ref, wg_ref, wu_ref, wd_ref, y_ref, wg_bf, wu_bf, wd_bf):
    i = pl.program_id(0)
    used = i < nu_ref[0]

    @pl.when(used & ((i == 0) | (be_ref[i] != be_ref[jnp.maximum(i - 1, 0)])))
    def _():
        wg_bf[...] = wg_ref[0].astype(BF16)
        wu_bf[...] = wu_ref[0].astype(BF16)
        wd_bf[...] = wd_ref[0].astype(BF16)

    @pl.when(used)
    def _():
        half = wg_bf.shape[0] // 2
        x_lo, x_hi = (part.astype(BF16) for part in _unpack_bf16_halves(x_ref[...]))

        def up(w_bf):
            return (jnp.dot(x_lo, w_bf[:half, :], preferred_element_type=F32)
                    + jnp.dot(x_hi, w_bf[half:, :], preferred_element_type=F32))

        g = up(wg_bf)
        u = up(wu_bf)
        hid = (g * _sigmoid(g) * u).astype(BF16)
        y_ref[...] = _pack_bf16_halves(jnp.dot(hid, wd_bf[...], preferred_element_type=F32))

    @pl.when(jnp.logical_not(used))
    def _():
        y_ref[...] = jnp.zeros_like(y_ref)


def _experts(block_expert, n_used, xs, wg, wu, wd):
    rows, dp = xs.shape
    d = wg.shape[1]
    assert dp * 2 == d
    de = wg.shape[2]
    nb = rows // EXPERT_BLOCK
    last = lambda i, nu: jnp.minimum(i, nu[0] - 1)
    grid_spec = pltpu.PrefetchScalarGridSpec(
        num_scalar_prefetch=2,
        grid=(nb,),
        in_specs=[pl.BlockSpec((EXPERT_BLOCK, dp), lambda i, be, nu: (last(i, nu), 0)),
                  pl.BlockSpec((1, d, de), lambda i, be, nu: (be[last(i, nu)], 0, 0)),
                  pl.BlockSpec((1, d, de), lambda i, be, nu: (be[last(i, nu)], 0, 0)),
                  pl.BlockSpec((1, de, d), lambda i, be, nu: (be[last(i, nu)], 0, 0))],
        out_specs=pl.BlockSpec((EXPERT_BLOCK, dp), lambda i, be, nu: (i, 0)),
        scratch_shapes=[pltpu.VMEM((d, de), BF16), pltpu.VMEM((d, de), BF16), pltpu.VMEM((de, d), BF16)],
    )
    return pl.pallas_call(
        _experts_kernel,
        grid_spec=grid_spec,
        out_shape=jax.ShapeDtypeStruct((rows, dp), jnp.uint32),
        compiler_params=_params(1),
        name="experts",
    )(block_expert, n_used, xs, wg, wu, wd)


def _combine_kernel(n_tiles, pos0_ref, pos1_ref, next0_ref, next1_ref, h_ref, wt_ref, gf_ref, y_hbm_ref, out_ref,
                    buf_ref, sem):
    i = pl.program_id(0)
    tm = h_ref.shape[0]

    def request(p0_ref, p1_ref, slot):
        for t in range(tm):
            _row_copy(y_hbm_ref, p0_ref[t], buf_ref.at[slot, 0], t, sem.at[slot]).start(priority=0)
            _row_copy(y_hbm_ref, p1_ref[t], buf_ref.at[slot, 1], t, sem.at[slot]).start(priority=1)

    def consume(slot):
        for k in range(TOP_K):
            pltpu.make_async_copy(y_hbm_ref.at[pl.ds(0, tm)], buf_ref.at[slot, k], sem.at[slot]).wait()
        wt = wt_ref[...]
        lo0, hi0 = _unpack_bf16_halves(buf_ref[slot, 0])
        lo1, hi1 = _unpack_bf16_halves(buf_ref[slot, 1])
        w0 = wt[:, 0:1]
        w1 = wt[:, 1:2]
        moe = jnp.concatenate([lo0 * w0 + lo1 * w1, hi0 * w0 + hi1 * w1], axis=1)
        out_ref[...] = _rmsnorm(h_ref[...] + moe, gf_ref[...])

    pl.when(i == 0)(lambda: request(pos0_ref, pos1_ref, 0))
    for slot in range(2):
        pl.when((i + 1 < n_tiles) & ((i + 1) % 2 == slot))(functools.partial(request, next0_ref, next1_ref, slot))
    for slot in range(2):
        pl.when(i % 2 == slot)(functools.partial(consume, slot))


def _combine(pos0, pos1, h, wt, gf, y_sorted, row0, n_rows):
    d = h.shape[1]
    dp = y_sorted.shape[1]
    assert row0 % ROW_TILE == 0 and n_rows % ROW_TILE == 0
    blk0 = row0 // ROW_TILE
    n_tiles = n_rows // ROW_TILE
    last = blk0 + n_tiles - 1
    return pl.pallas_call(
        functools.partial(_combine_kernel, n_tiles),
        grid=(n_tiles,),
        in_specs=[pl.BlockSpec((ROW_TILE,), lambda i: (blk0 + i,), memory_space=pltpu.SMEM),
                  pl.BlockSpec((ROW_TILE,), lambda i: (blk0 + i,), memory_space=pltpu.SMEM),
                  pl.BlockSpec((ROW_TILE,), lambda i: (jnp.minimum(blk0 + i + 1, last),), memory_space=pltpu.SMEM),
                  pl.BlockSpec((ROW_TILE,), lambda i: (jnp.minimum(blk0 + i + 1, last),), memory_space=pltpu.SMEM),
                  pl.BlockSpec((ROW_TILE, d), lambda i: (blk0 + i, 0)),
                  pl.BlockSpec((ROW_TILE, LANES), lambda i: (blk0 + i, 0)),
                  _const_spec((1, d)),
                  pl.BlockSpec(memory_space=pl.ANY)],
        out_specs=pl.BlockSpec((ROW_TILE, d), lambda i: (i, 0)),
        out_shape=jax.ShapeDtypeStruct((n_rows, d), F32),
        scratch_shapes=[pltpu.VMEM((2, TOP_K, ROW_TILE, dp), jnp.uint32), pltpu.SemaphoreType.DMA((2,))],
        compiler_params=_params(1),
        name="combine",
    )(pos0, pos1, pos0, pos1, h, wt, gf, y_sorted)


def _bias_base(rel_bias, n_queries, key_offset):
    m = jnp.arange(BIAS_BASE)
    dist = (n_queries - 1) - m + key_offset
    idx = jnp.clip(dist, -MAX_REL, MAX_REL) + MAX_REL
    return rel_bias[:, idx].astype(F32)


def _block_diag(w):
    nb, bw, _ = w.shape
    eye = jnp.eye(nb, dtype=w.dtype)
    return jnp.einsum("ncd,nm->ncmd", w, eye).reshape(nb * bw, nb * bw)


def kernel(x_prompt, x_sample, cache_k, cache_v, state_conv, state_h, norm1_g, w_in, b_in, rel_bias, w_att_branch, conv_w, conv_b, w_rec_gate, b_rec_gate, w_in_gate, b_in_gate, lru_lambda, w_rnn_branch, w_out, norm2_g, w_router_group, b_router_group, w_router_expert, b_router_expert, w_e_gate, w_e_up, w_e_down, final_norm_g):
    batch, seq, d = x_prompt.shape
    dec_b, dec_t, _ = x_sample.shape
    depth = w_in.shape[0]
    assert depth == 1
    att_w = N_HEADS * HEAD_DIM
    rnn_w = conv_w.shape[2]
    cache_len = cache_k.shape[2]
    n_p = batch * seq
    n_s = dec_b * dec_t
    n = n_p + n_s
    keep = min(BAND_PAST + CHUNK, seq)
    row2 = lambda a: a.reshape(1, -1).astype(F32)

    tiles_per_seq = seq // ROW_TILE
    tail_tiles = -(-keep // ROW_TILE)
    state_tiles = tuple(b * tiles_per_seq + tiles_per_seq - tail_tiles + j
                        for b in range(batch) for j in range(tail_tiles)) + (n_p // ROW_TILE,)

    xp = x_prompt.reshape(n_p, d)
    xs = x_sample.reshape(n_s, d)
    rnn_args = (conv_w[0].astype(F32), row2(conv_b[0]),
                _block_diag(w_rec_gate[0]).astype(BF16), row2(b_rec_gate[0]),
                _block_diag(w_in_gate[0]).astype(BF16), row2(b_in_gate[0]), row2(lru_lambda[0]))
    q, k, v, kvt, ga, gr, rnn_p, conv_p, h_p, rx_s, rgate_s = _proj(
        xp, xs, row2(norm1_g[0]), w_in[0].astype(BF16), row2(b_in[0]), rnn_args, att_w, rnn_w, state_tiles, batch)

    att_p = _attn_prompt(q, k, v, _bias_base(rel_bias[0], Q_BLOCK, (KEY_BLOCKS - 1) * Q_BLOCK), batch, seq)
    att_s, new_k_s, new_v_s = _attn_sample(
        q, k, v, kvt, cache_k[0].reshape(dec_b, cache_len, att_w), cache_v[0].reshape(dec_b, cache_len, att_w),
        _bias_base(rel_bias[0], dec_t, cache_len), n_p, batch * tail_tiles * ROW_TILE, dec_b, dec_t)

    pad_state = lambda s: jnp.pad(s.astype(F32), ((0, 0), (SUBLANES - (CONV_W - 1), 0), (0, 0)))
    rnn_s, conv_s, h_s = _rnn_sample(rx_s, rgate_s, pad_state(state_conv[0]), state_h[0].astype(F32)[:, None, :],
                                     *rnn_args, dec_b, dec_t)

    w_r = jnp.zeros((d, LANES), F32).at[:, :N_GROUPS].set(w_router_group[0])
    w_r = w_r.at[:, N_GROUPS:N_GROUPS + N_EXPERTS].set(w_router_expert[0]).astype(BF16)
    b_r = jnp.zeros((1, LANES), F32).at[0, :N_GROUPS].set(b_router_group[0])
    b_r = b_r.at[0, N_GROUPS:N_GROUPS + N_EXPERTS].set(b_router_expert[0])
    h, hn, route, wt, cnt = _merge(xp, xs, att_p, att_s, rnn_p, rnn_s, ga, gr,
                                   w_att_branch[0].astype(BF16), w_rnn_branch[0].astype(BF16), w_out[0].astype(BF16),
                                   row2(norm2_g[0]), w_r, b_r)

    counts = cnt[0, :N_EXPERTS].astype(jnp.int32)
    padded = (counts + EXPERT_BLOCK - 1) // EXPERT_BLOCK * EXPERT_BLOCK
    pad_end = jnp.cumsum(padded)
    pad_start = pad_end - padded
    expert = route[0:TOP_K].astype(jnp.int32)
    rank = route[TOP_K:2 * TOP_K].astype(jnp.int32)
    ids = jnp.arange(N_EXPERTS, dtype=jnp.int32)[:, None, None]
    pos = jnp.sum(jnp.where(expert[None] == ids, pad_start[:, None, None], 0), axis=0) + rank
    n_blocks = -(-(n * TOP_K) // EXPERT_BLOCK) + N_EXPERTS
    rows = n_blocks * EXPERT_BLOCK
    block_start = jnp.arange(n_blocks, dtype=jnp.int32) * EXPERT_BLOCK
    block_expert = jnp.minimum(jnp.sum((block_start[:, None] >= pad_end[None, :]).astype(jnp.int32), axis=1),
                               N_EXPERTS - 1)
    n_used = (pad_end[-1:] // EXPERT_BLOCK).astype(jnp.int32)

    x_sorted = _dispatch(pad_end, padded, pos[0], pos[1], hn, rows)
    y_sorted = _experts(block_expert, n_used, x_sorted, w_e_gate[0], w_e_up[0], w_e_down[0])
    gf = row2(final_norm_g)
    y_p = _combine(pos[0], pos[1], h, wt, gf, y_sorted, 0, n_p)
    y_s = _combine(pos[0], pos[1], h, wt, gf, y_sorted, n_p, n_s)

    def kv_state(col0):
        span = tail_tiles * ROW_TILE
        tails = [kvt[b * span + span - keep:(b + 1) * span, col0:col0 + att_w] for b in range(batch)]
        return jnp.stack(tails).reshape(1, batch, keep, N_HEADS, HEAD_DIM)

    heads = lambda a: a.reshape(1, dec_b, cache_len, N_HEADS, HEAD_DIM)
    return (y_p.reshape(batch, seq, d), y_s.reshape(dec_b, dec_t, d),
            kv_state(0), kv_state(att_w),
            conv_p[:, SUBLANES - (CONV_W - 1):][None], h_p[:, 0][None],
            heads(new_k_s), heads(new_v_s),
            conv_s[:, SUBLANES - (CONV_W - 1):][None], h_s[:, 0][None])
```
